```python
import jax
import jax.numpy as jnp
from jax import lax
import numpy as np

D_MODEL = 4096
BATCH = 2
SEQ = 8192
DEPTH = 4

GRID_W = 64
CTX_LEN = 256
N_MIXERS = 4
N_MOD = 6
NORM_EPS = 1e-6
NEG_INF = -1e30
ROPE_THETA = 10000.0

NA_HEADS = 32
NA_HEAD_DIM = D_MODEL // NA_HEADS
NA_KH = 8
NA_KW = 16
NA_QW = 16
NA_KBW = NA_QW + NA_KW

RET_HEADS = 16
RET_HEAD_DIM = D_MODEL // RET_HEADS
RET_CHUNK = 128
RET_DECAY_EXP0 = 5.0

GQA_HEADS = 32
GQA_KV_HEADS = 8
GQA_HEAD_DIM = D_MODEL // GQA_HEADS
Q_BLOCK = 128

SWA_HEADS = 64
SWA_KV_HEADS = 8
SWA_HEAD_DIM = D_MODEL // SWA_HEADS
SWA_WINDOW = 128
SWA_BLOCK = 128

N_EXPERTS = 32
TOP_K = 4
D_EXPERT = 160
SWIGLU_ALPHA = 1.702
SWIGLU_LIMIT = 7.0
EXPERT_BLOCK = 128

kernel_name = 'hybrid_diffusion_trunk'


def n_layers_of(m):
    return len(range(m, DEPTH, N_MIXERS))


def rms_norm(x, g):
    xf = x.astype(jnp.float32)
    y = xf * lax.rsqrt(jnp.mean(xf * xf, axis=-1, keepdims=True) + NORM_EPS)
    return (y * g.astype(jnp.float32)).astype(x.dtype)


def head_rms(y):
    return y * lax.rsqrt(jnp.mean(y * y, axis=-1, keepdims=True) + NORM_EPS)


def modulate(h, shift, scale):
    return h * (1.0 + scale) + shift


def to_heads(t, n_heads):
    b, l, _ = t.shape
    return t.reshape(b, l, n_heads, -1).transpose(0, 2, 1, 3)


def from_heads(t):
    b, n, l, d = t.shape
    return t.transpose(0, 2, 1, 3).reshape(b, l, n * d)


def rope_2d(x, rows, cols):
    half = x.shape[-1] // 2
    n = half // 2
    freqs = ROPE_THETA ** (-jnp.arange(n, dtype=jnp.float32) / n)
    ang = jnp.concatenate([rows.astype(jnp.float32)[:, None] * freqs,
                           cols.astype(jnp.float32)[:, None] * freqs], axis=-1)
    cos, sin = jnp.cos(ang).astype(x.dtype), jnp.sin(ang).astype(x.dtype)
    x1, x2 = x[..., :half], x[..., half:]
    return jnp.concatenate([x1 * cos - x2 * sin, x2 * cos + x1 * sin], axis=-1)


def gqa_attend(q, k, v, bias=None, sink=None):
    s = jnp.einsum('bkgqd,bksd->bkgqs', q, k).astype(jnp.float32) * (q.shape[-1] ** -0.5)
    if bias is not None:
        s = s + bias
    if sink is not None:
        sink_col = jnp.broadcast_to(sink.astype(jnp.float32)[None, :, :, None, None], s.shape[:-1] + (1,))
        p = jax.nn.softmax(jnp.concatenate([s, sink_col], axis=-1), axis=-1)[..., :-1]
    else:
        p = jax.nn.softmax(s, axis=-1)
    return jnp.einsum('bkgqs,bksd->bkgqd', p.astype(v.dtype), v)


def gqa_project(h, w_qkv, n_heads, n_kv, head_dim, pos, q_gain=None, k_gain=None):
    b, l, _ = h.shape
    q, k, v = jnp.split(h @ w_qkv, [n_heads * head_dim, (n_heads + n_kv) * head_dim], axis=-1)
    q, k, v = to_heads(q, n_heads), to_heads(k, n_kv), to_heads(v, n_kv)
    if q_gain is not None:
        q, k = rms_norm(q, q_gain), rms_norm(k, k_gain)
    if pos is not None:
        q, k = rope_2d(q, *pos), rope_2d(k, *pos)
    return q.reshape(b, n_kv, n_heads // n_kv, l, head_dim), k, v


def mixer_neighbourhood(hx, hc, w_qkv, w_o, rpb, need_ctx):
    b, l, _ = hx.shape
    n_rows = l // GRID_W
    kh = min(NA_KH, n_rows)
    ncb = GRID_W // NA_QW
    qx, kx, vx = (to_heads(t, NA_HEADS) for t in jnp.split(hx @ w_qkv, 3, axis=-1))
    qc, kc, vc = (to_heads(t, NA_HEADS) for t in jnp.split(hc @ w_qkv, 3, axis=-1))
    scale = NA_HEAD_DIM ** -0.5
    qcol = np.arange(ncb)[:, None] * NA_QW + np.arange(NA_QW)[None, :]
    c0 = np.clip(qcol - NA_KW // 2, 0, GRID_W - NA_KW)
    kcol = (np.clip(np.arange(ncb) * NA_QW - NA_KW // 2, 0, GRID_W - NA_KBW)[:, None]
            + np.arange(NA_KBW)[None, :])
    col_ok = (kcol[:, None, :] >= c0[:, :, None]) & (kcol[:, None, :] < c0[:, :, None] + NA_KW)
    col_idx = np.clip(kcol[:, None, :] - qcol[:, :, None], 1 - NA_KW, NA_KW - 1) + NA_KW - 1
    rpb32 = rpb.astype(jnp.float32)
    qg = qx.reshape(b, NA_HEADS, n_rows, ncb, NA_QW, NA_HEAD_DIM)
    kg = kx.reshape(b, NA_HEADS, n_rows, GRID_W, NA_HEAD_DIM)
    vg = vx.reshape(b, NA_HEADS, n_rows, GRID_W, NA_HEAD_DIM)
    n_win = kh * NA_KBW

    def gather_rows(t, r0):
        t = lax.dynamic_slice_in_dim(t, r0, kh, axis=2)[:, :, :, kcol]
        return t.transpose(0, 1, 3, 2, 4, 5).reshape(b, NA_HEADS, ncb, n_win, NA_HEAD_DIM)

    def row_step(r):
        r0 = jnp.clip(r - kh // 2, 0, n_rows - kh)
        q_r = lax.dynamic_index_in_dim(qg, r, axis=2, keepdims=False)
        k_r, v_r = gather_rows(kg, r0), gather_rows(vg, r0)
        row_idx = r0 + jnp.arange(kh) - r + NA_KH - 1
        bias = rpb32[:, row_idx[None, None, :, None], col_idx[:, :, None, :]]
        bias = jnp.where(col_ok[:, :, None, :], bias, NEG_INF).reshape(NA_HEADS, ncb, NA_QW, n_win)
        s_win = jnp.einsum('bhcqd,bhckd->bhcqk', q_r, k_r).astype(jnp.float32) * scale + bias
        s_ctx = jnp.einsum('bhcqd,bhkd->bhcqk', q_r, kc).astype(jnp.float32) * scale
        p = jax.nn.softmax(jnp.concatenate([s_win, s_ctx], axis=-1), axis=-1).astype(vx.dtype)
        return (jnp.einsum('bhcqk,bhckd->bhcqd', p[..., :n_win], v_r)
                + jnp.einsum('bhcqk,bhkd->bhcqd', p[..., n_win:], vc))

    o = lax.map(row_step, jnp.arange(n_rows))
    o = o.transpose(1, 2, 0, 3, 4, 5).reshape(b, NA_HEADS, l, NA_HEAD_DIM)
    ox = from_heads(o) @ w_o
    oc = from_heads(gqa_attend(qc[:, :, None], kc, vc)[:, :, 0]) @ w_o if need_ctx else None
    return ox, oc


def retention_chunked(q, k, v, log_gamma, state):
    b, h, l, _ = q.shape
    n = l // RET_CHUNK
    pos = jnp.arange(RET_CHUNK, dtype=jnp.float32)
    diff = pos[:, None] - pos[None, :]
    intra = jnp.where(diff >= 0, jnp.exp(log_gamma[:, None, None] * jnp.maximum(diff, 0.0)), 0.0)
    q_dec = jnp.exp(log_gamma[:, None] * (pos + 1.0))[:, :, None]
    k_dec = jnp.exp(log_gamma[:, None] * (RET_CHUNK - 1.0 - pos))[:, :, None]
    c_dec = jnp.exp(log_gamma * RET_CHUNK)[:, None, None]

    def chunks(t):
        return t.reshape(b, h, n, RET_CHUNK, t.shape[-1]).transpose(2, 0, 1, 3, 4)

    def step(s, inp):
        qi, ki, vi = inp
        att = jnp.einsum('bhqd,bhkd->bhqk', qi, ki) * intra
        y = jnp.einsum('bhqk,bhkv->bhqv', att, vi) + jnp.einsum('bhqd,bhdv->bhqv', qi * q_dec, s)
        s = s * c_dec + jnp.einsum('bhkd,bhkv->bhdv', ki * k_dec, vi)
        return s, y

    state, ys = lax.scan(step, state, (chunks(q), chunks(k), chunks(v)))
    return ys.transpose(1, 2, 0, 3, 4).reshape(b, h, l, -1), state


def mixer_retention(hx, hc, w_in, w_o, decay_exp, pos, need_ctx):
    lg = jnp.log1p(-jnp.exp2(-decay_exp.astype(jnp.float32)))

    def project(h, p):
        q, k, v, g_f, g_b = jnp.split(h @ w_in, 5, axis=-1)
        q, k, v = (to_heads(t, RET_HEADS).astype(jnp.float32) for t in (q, k, v))
        if p is not None:
            q, k = rope_2d(q, *p), rope_2d(k, *p)
        return q, k * (RET_HEAD_DIM ** -0.5), v, g_f, g_b

    def both_dirs(q, k, v, st_f, st_b):
        y_f, st_f = retention_chunked(q, k, v, lg[0], st_f)
        y_b, st_b = retention_chunked(jnp.flip(q, 2), jnp.flip(k, 2), jnp.flip(v, 2), lg[1], st_b)
        return y_f, jnp.flip(y_b, 2), st_f, st_b

    def merge(y_f, y_b, g_f, g_b):
        y_f = from_heads(head_rms(y_f)).astype(g_f.dtype)
        y_b = from_heads(head_rms(y_b)).astype(g_b.dtype)
        return (jax.nn.silu(g_f) * y_f + jax.nn.silu(g_b) * y_b) @ w_o

    b = hx.shape[0]
    zero = jnp.zeros((b, RET_HEADS, RET_HEAD_DIM, RET_HEAD_DIM), jnp.float32)
    qc, kc, vc, gcf, gcb = project(hc, None)
    yc_f, yc_b, st_f, st_b = both_dirs(qc, kc, vc, zero, zero)
    qx, kx, vx, gxf, gxb = project(hx, pos)
    yx_f, yx_b, _, _ = both_dirs(qx, kx, vx, st_f, st_b)
    ox = merge(yx_f, yx_b, gxf, gxb)
    oc = merge(yc_f, yc_b, gcf, gcb) if need_ctx else None
    return ox, oc


def mixer_global_gqa(hx, hc, w_qkv, w_o, q_gain, k_gain, pos, need_ctx):
    b, l, _ = hx.shape
    grp = GQA_HEADS // GQA_KV_HEADS
    args = (w_qkv, GQA_HEADS, GQA_KV_HEADS, GQA_HEAD_DIM)
    qx, kx, vx = gqa_project(hx, *args, pos, q_gain, k_gain)
    qc, kc, vc = gqa_project(hc, *args, None, q_gain, k_gain)
    k_all = jnp.concatenate([kc, kx], axis=2)
    v_all = jnp.concatenate([vc, vx], axis=2)
    nb = l // Q_BLOCK
    q_blocks = qx.reshape(b, GQA_KV_HEADS, grp, nb, Q_BLOCK, GQA_HEAD_DIM).transpose(3, 0, 1, 2, 4, 5)
    o = lax.map(lambda q_blk: gqa_attend(q_blk, k_all, v_all), q_blocks)
    o = o.transpose(1, 2, 3, 0, 4, 5).reshape(b, GQA_HEADS, l, GQA_HEAD_DIM)
    ox = from_heads(o) @ w_o
    oc = None
    if need_ctx:
        oc = from_heads(gqa_attend(qc, kc, vc).reshape(b, GQA_HEADS, -1, GQA_HEAD_DIM)) @ w_o
    return ox, oc


def mixer_window_gqa(hx, hc, w_qkv, w_o, sink, pos, need_ctx):
    b, l, _ = hx.shape
    n_ctx = hc.shape[1]
    grp = SWA_HEADS // SWA_KV_HEADS
    args = (w_qkv, SWA_HEADS, SWA_KV_HEADS, SWA_HEAD_DIM)
    qx, kx, vx = gqa_project(hx, *args, pos)
    qc, kc, vc = gqa_project(hc, *args, None)
    sink_g = sink.reshape(SWA_KV_HEADS, grp)
    band = SWA_BLOCK + 2 * SWA_WINDOW
    pad = ((0, 0), (0, 0), (SWA_WINDOW, SWA_WINDOW), (0, 0))
    kp, vp = jnp.pad(kx, pad), jnp.pad(vx, pad)
    in_win = np.abs(np.arange(band)[None, :] - SWA_WINDOW - np.arange(SWA_BLOCK)[:, None]) <= SWA_WINDOW
    nb = l // SWA_BLOCK
    q_blocks = qx.reshape(b, SWA_KV_HEADS, grp, nb, SWA_BLOCK, SWA_HEAD_DIM).transpose(3, 0, 1, 2, 4, 5)

    def block_step(inp):
        i, q_blk = inp
        start = i * SWA_BLOCK
        kb = lax.dynamic_slice_in_dim(kp, start, band, axis=2)
        vb = lax.dynamic_slice_in_dim(vp, start, band, axis=2)
        kpos = start - SWA_WINDOW + jnp.arange(band)
        ok = in_win & ((kpos >= 0) & (kpos < l))[None, :]
        bias = jnp.concatenate([jnp.zeros((SWA_BLOCK, n_ctx), jnp.float32),
                                jnp.where(ok, 0.0, NEG_INF).astype(jnp.float32)], axis=-1)
        return gqa_attend(q_blk, jnp.concatenate([kc, kb], axis=2), jnp.concatenate([vc, vb], axis=2), bias, sink_g)

    o = lax.map(block_step, (jnp.arange(nb), q_blocks))
    o = o.transpose(1, 2, 3, 0, 4, 5).reshape(b, SWA_HEADS, l, SWA_HEAD_DIM)
    ox = from_heads(o) @ w_o
    oc = None
    if need_ctx:
        oc = from_heads(gqa_attend(qc, kc, vc, None, sink_g).reshape(b, SWA_HEADS, -1, SWA_HEAD_DIM)) @ w_o
    return ox, oc


def clamped_swiglu(u):
    glu = jnp.minimum(u[..., ::2], SWIGLU_LIMIT)
    lin = jnp.clip(u[..., 1::2], -SWIGLU_LIMIT, SWIGLU_LIMIT)
    return glu * jax.nn.sigmoid(SWIGLU_ALPHA * glu) * (lin + 1.0)


def moe_ffn(h, w_router, b_router, w1, b1, w2, b2):
    t_tok, d = h.shape
    logits = (h @ w_router).astype(jnp.float32) + b_router.astype(jnp.float32)
    top_val, top_idx = lax.top_k(logits, TOP_K)
    gate = jax.nn.softmax(top_val, axis=-1)
    n_assign = t_tok * TOP_K
    e_flat = top_idx.reshape(n_assign)
    t_flat = jnp.repeat(jnp.arange(t_tok, dtype=jnp.int32), TOP_K)
    order = jnp.argsort(e_flat)
    e_s, t_s, g_s = e_flat[order], t_flat[order], gate.reshape(n_assign)[order]
    counts = jnp.bincount(e_flat, length=N_EXPERTS)
    padded = (counts + EXPERT_BLOCK - 1) // EXPERT_BLOCK * EXPERT_BLOCK
    pad_end = jnp.cumsum(padded)
    pad_start = pad_end - padded
    start = jnp.cumsum(counts) - counts
    slot = pad_start[e_s] + jnp.arange(n_assign) - start[e_s]
    n_blocks = -(-n_assign // EXPERT_BLOCK) + N_EXPERTS
    n_slots = n_blocks * EXPERT_BLOCK
    slot_tok = jnp.full((n_slots,), t_tok, jnp.int32).at[slot].set(t_s)
    slot_gate = jnp.zeros((n_slots,), jnp.float32).at[slot].set(g_s)
    blk_start = jnp.arange(n_blocks) * EXPERT_BLOCK
    blk_expert = jnp.minimum(jnp.sum(pad_end[None, :] <= blk_start[:, None], axis=1), N_EXPERTS - 1)
    h_pad = jnp.concatenate([h, jnp.zeros((1, d), h.dtype)], axis=0)

    def run_block(inp):
        tok, g, e = inp
        a = clamped_swiglu(h_pad[tok] @ w1[e] + b1[e])
        y = a @ w2[e] + b2[e]
        return y * g[:, None].astype(y.dtype)

    ys = lax.map(run_block, (slot_tok.reshape(n_blocks, EXPERT_BLOCK),
                             slot_gate.reshape(n_blocks, EXPERT_BLOCK), blk_expert))
    out = jnp.zeros((t_tok + 1, d), ys.dtype).at[slot_tok].add(ys.reshape(n_slots, d))
    return out[:t_tok]


def setup_inputs(seed: int = 0) -> dict:
    key = jax.random.key(seed)
    keys = iter(jax.random.split(key, 32))

    def nrm(shape, scale):
        return jax.random.normal(next(keys), shape, jnp.float32) * scale

    d = D_MODEL
    n_na, n_ret, n_gqa, n_swa = (n_layers_of(m) for m in range(N_MIXERS))
    gqa_cols = (GQA_HEADS + 2 * GQA_KV_HEADS) * GQA_HEAD_DIM
    swa_cols = (SWA_HEADS + 2 * SWA_KV_HEADS) * SWA_HEAD_DIM
    ret_exp = RET_DECAY_EXP0 + jnp.arange(RET_HEADS, dtype=jnp.float32)
    return {
        'x': nrm((BATCH, SEQ, d), 1.0),
        'c': nrm((BATCH, d), 1.0),
        'ctx': nrm((BATCH, CTX_LEN, d), 1.0),
        'c_ctx': nrm((d,), 1.0),
        'w_mod': nrm((DEPTH, d, N_MOD * d), 0.5 * d ** -0.5),
        'b_mod': nrm((DEPTH, N_MOD * d), 0.02),
        'g_mix': 1.0 + nrm((DEPTH, d), 0.02),
        'g_ffn': 1.0 + nrm((DEPTH, d), 0.02),
        'g_final': 1.0 + nrm((d,), 0.02),
        'na_w_qkv': nrm((n_na, d, 3 * d), d ** -0.5),
        'na_w_o': nrm((n_na, d, d), d ** -0.5),
        'na_rpb': nrm((n_na, NA_HEADS, 2 * NA_KH - 1, 2 * NA_KW - 1), 0.1),
        'ret_w_in': nrm((n_ret, d, 5 * d), d ** -0.5),
        'ret_w_o': nrm((n_ret, d, d), d ** -0.5),
        'ret_decay_exp': ret_exp + nrm((n_ret, 2, RET_HEADS), 0.1),
        'gqa_w_qkv': nrm((n_gqa, d, gqa_cols), d ** -0.5),
        'gqa_w_o': nrm((n_gqa, d, d), d ** -0.5),
        'gqa_q_gain': 1.0 + nrm((n_gqa, GQA_HEAD_DIM), 0.02),
        'gqa_k_gain': 1.0 + nrm((n_gqa, GQA_HEAD_DIM), 0.02),
        'swa_w_qkv': nrm((n_swa, d, swa_cols), d ** -0.5),
        'swa_w_o': nrm((n_swa, d, d), d ** -0.5),
        'swa_sink': nrm((n_swa, SWA_HEADS), 0.5),
        'moe_w_router': nrm((DEPTH, d, N_EXPERTS), d ** -0.5),
        'moe_b_router': nrm((DEPTH, N_EXPERTS), 0.01),
        'moe_w1': nrm((DEPTH, N_EXPERTS, d, 2 * D_EXPERT), d ** -0.5),
        'moe_b1': nrm((DEPTH, N_EXPERTS, 2 * D_EXPERT), 0.02),
        'moe_w2': nrm((DEPTH, N_EXPERTS, D_EXPERT, d), D_EXPERT ** -0.5),
        'moe_b2': nrm((DEPTH, N_EXPERTS, d), 0.02),
    }


def reference(x, c, ctx, c_ctx, w_mod, b_mod, g_mix, g_ffn, g_final,
              na_w_qkv, na_w_o, na_rpb, ret_w_in, ret_w_o, ret_decay_exp,
              gqa_w_qkv, gqa_w_o, gqa_q_gain, gqa_k_gain, swa_w_qkv, swa_w_o, swa_sink,
              moe_w_router, moe_b_router, moe_w1, moe_b1, moe_w2, moe_b2):
    b, l, d = x.shape
    n_ctx = ctx.shape[1]
    t = jnp.arange(l)
    pos = (t // GRID_W, t % GRID_W)
    silu_c = jax.nn.silu(c)
    silu_cc = jax.nn.silu(c_ctx)
    for li in range(DEPTH):
        m, j = li % N_MIXERS, li // N_MIXERS
        need_ctx = li < DEPTH - 1
        mod_x = jnp.split((silu_c @ w_mod[li] + b_mod[li])[:, None, :], N_MOD, axis=-1)
        mod_c = jnp.split(silu_cc @ w_mod[li] + b_mod[li], N_MOD, axis=-1)
        hx = modulate(rms_norm(x, g_mix[li]), mod_x[0], mod_x[1])
        hc = modulate(rms_norm(ctx, g_mix[li]), mod_c[0], mod_c[1])
        if m == 0:
            ox, oc = mixer_neighbourhood(hx, hc, na_w_qkv[j], na_w_o[j], na_rpb[j], need_ctx)
        elif m == 1:
            ox, oc = mixer_retention(hx, hc, ret_w_in[j], ret_w_o[j], ret_decay_exp[j], pos, need_ctx)
        elif m == 2:
            ox, oc = mixer_global_gqa(hx, hc, gqa_w_qkv[j], gqa_w_o[j], gqa_q_gain[j], gqa_k_gain[j], pos, need_ctx)
        else:
            ox, oc = mixer_window_gqa(hx, hc, swa_w_qkv[j], swa_w_o[j], swa_sink[j], pos, need_ctx)
        x = x + mod_x[2] * ox
        hx = modulate(rms_norm(x, g_ffn[li]), mod_x[3], mod_x[4])
        if need_ctx:
            ctx = ctx + mod_c[2] * oc
            hc = modulate(rms_norm(ctx, g_ffn[li]), mod_c[3], mod_c[4])
            tokens = jnp.concatenate([hx.reshape(b * l, d), hc.reshape(b * n_ctx, d)], axis=0)
        else:
            tokens = hx.reshape(b * l, d)
        y = moe_ffn(tokens, moe_w_router[li], moe_b_router[li], moe_w1[li], moe_b1[li], moe_w2[li], moe_b2[li])
        x = x + mod_x[5] * y[:b * l].reshape(b, l, d)
        if need_ctx:
            ctx = ctx + mod_c[5] * y[b * l:].reshape(b, n_ctx, d)
    return rms_norm(x, g_final)
```

```python
import functools

import jax
import jax.numpy as jnp
import numpy as np
from jax import lax
from jax.experimental import pallas as pl
from jax.experimental.pallas import tpu as pltpu

DEPTH = 4
GRID_W = 64
CTX_LEN = 256
N_MIXERS = 4
N_MOD = 6
NORM_EPS = 1e-6
NEG_INF = -1e30
ROPE_THETA = 10000.0

NA_HEADS = 32
NA_KH = 8
NA_KW = 16
NA_ROW_BLOCK = 4
NA_KEY_ROWS = 12

RET_HEADS = 16
RET_CHUNK = 128

GQA_HEADS = 32
GQA_KV_HEADS = 8

SWA_HEADS = 64
SWA_KV_HEADS = 8
SWA_WINDOW = 128
SWA_BLOCK = 128

N_EXPERTS = 32
TOP_K = 4
SWIGLU_ALPHA = 1.702
SWIGLU_LIMIT = 7.0
MOE_BLOCK = 256

LANES = 128
VMEM_LIMIT_BYTES = 56 * 1024 * 1024

F32 = jnp.float32
BF16 = jnp.bfloat16


def _params(semantics, vmem=VMEM_LIMIT_BYTES):
    return pltpu.CompilerParams(dimension_semantics=semantics, vmem_limit_bytes=vmem)


def _pick(n, candidates):
    for c in candidates:
        if n % c == 0:
            return c
    return n


def _mod_kernel(s_ref, w_ref, b_ref, o_ref):
    acc = jnp.dot(s_ref[...], w_ref[...].astype(BF16), preferred_element_type=F32)
    o_ref[...] = acc + b_ref[...]


def modulation_all_layers(s_rows, w_mod, b_mod):
    depth, d, n = w_mod.shape
    rows = s_rows.shape[0]
    bn = _pick(n, (512, 256, 128))
    return pl.pallas_call(
        _mod_kernel,
        out_shape=jax.ShapeDtypeStruct((depth, rows, n), F32),
        grid=(depth, n // bn),
        in_specs=[
            pl.BlockSpec((rows, d), lambda l, j: (0, 0)),
            pl.BlockSpec((None, d, bn), lambda l, j: (l, 0, j)),
            pl.BlockSpec((None, 1, bn), lambda l, j: (l, 0, j)),
        ],
        out_specs=pl.BlockSpec((None, rows, bn), lambda l, j: (l, 0, j)),
        compiler_params=_params(("parallel", "parallel")),
        name="modulation",
    )(s_rows.astype(BF16), w_mod, b_mod.reshape(depth, 1, n))


def _rms_mod(x, g, shift, scale):
    y = x * lax.rsqrt(jnp.mean(x * x, axis=-1, keepdims=True) + NORM_EPS) * g
    return y * (1.0 + scale) + shift


def _norm_mod_kernel(x_ref, g_ref, m_ref, o_ref, *, shift_idx):
    h = _rms_mod(x_ref[...], g_ref[...], m_ref[shift_idx:shift_idx + 1, :], m_ref[shift_idx + 1:shift_idx + 2, :])
    o_ref[...] = h.astype(o_ref.dtype)


def _norm_route_kernel(x_ref, g_ref, m_ref, wr_ref, br_ref, o_ref, idx_ref, gate_ref, *, shift_idx):
    h = _rms_mod(x_ref[...], g_ref[...], m_ref[shift_idx:shift_idx + 1, :], m_ref[shift_idx + 1:shift_idx + 2, :])
    o_ref[...] = h.astype(o_ref.dtype)
    logits = jnp.dot(h, wr_ref[...], precision=lax.Precision.HIGHEST, preferred_element_type=F32) + br_ref[...]
    rows, n_e = logits.shape
    e_iota = lax.broadcasted_iota(jnp.int32, (rows, n_e), 1).astype(F32)
    lane = lax.broadcasted_iota(jnp.int32, (rows, LANES), 1)
    idx_out = jnp.zeros((rows, LANES), F32)
    val_out = jnp.zeros((rows, LANES), F32)
    top0 = None
    denom = jnp.zeros((rows, 1), F32)
    work = logits
    for k in range(TOP_K):
        mx = jnp.max(work, axis=-1, keepdims=True)
        ix = jnp.min(jnp.where(work == mx, e_iota, float(n_e)), axis=-1, keepdims=True)
        if k == 0:
            top0 = mx
        ex = jnp.exp(mx - top0)
        denom = denom + ex
        idx_out = jnp.where(lane == k, ix, idx_out)
        val_out = jnp.where(lane == k, ex, val_out)
        work = jnp.where(e_iota == ix, -jnp.inf, work)
    idx_ref[...] = idx_out.astype(jnp.int32)
    gate_ref[...] = val_out / denom


def norm_modulate(xs, g, modv, shift_idx):
    b, s, d = xs.shape
    tr = CTX_LEN
    nblk = s // tr
    return pl.pallas_call(
        functools.partial(_norm_mod_kernel, shift_idx=shift_idx),
        out_shape=jax.ShapeDtypeStruct((b, s, d), BF16),
        grid=(b, nblk),
        in_specs=[
            pl.BlockSpec((None, tr, d), lambda bi, i: (bi, i, 0)),
            pl.BlockSpec((1, d), lambda bi, i: (0, 0)),
            pl.BlockSpec((None, None, N_MOD, d), lambda bi, i: (bi, jnp.minimum(i, 1), 0, 0)),
        ],
        out_specs=pl.BlockSpec((None, tr, d), lambda bi, i: (bi, i, 0)),
        compiler_params=_params(("parallel", "parallel")),
        name="norm_modulate",
    )(xs, g.reshape(1, d), modv)


def norm_modulate_route(xs, g, modv, shift_idx, w_router, b_router):
    b, s, d = xs.shape
    tr = CTX_LEN
    nblk = s // tr
    n_e = w_router.shape[1]
    return pl.pallas_call(
        functools.partial(_norm_route_kernel, shift_idx=shift_idx),
        out_shape=(jax.ShapeDtypeStruct((b, s, d), BF16),
                   jax.ShapeDtypeStruct((b, s, LANES), jnp.int32),
                   jax.ShapeDtypeStruct((b, s, LANES), F32)),
        grid=(b, nblk),
        in_specs=[
            pl.BlockSpec((None, tr, d), lambda bi, i: (bi, i, 0)),
            pl.BlockSpec((1, d), lambda bi, i: (0, 0)),
            pl.BlockSpec((None, None, N_MOD, d), lambda bi, i: (bi, jnp.minimum(i, 1), 0, 0)),
            pl.BlockSpec((d, n_e), lambda bi, i: (0, 0)),
            pl.BlockSpec((1, n_e), lambda bi, i: (0, 0)),
        ],
        out_specs=(pl.BlockSpec((None, tr, d), lambda bi, i: (bi, i, 0)),
                   pl.BlockSpec((None, tr, LANES), lambda bi, i: (bi, i, 0)),
                   pl.BlockSpec((None, tr, LANES), lambda bi, i: (bi, i, 0))),
        compiler_params=_params(("parallel", "parallel")),
        name="norm_modulate_route",
    )(xs, g.reshape(1, d), modv, w_router, b_router.reshape(1, n_e))


def _final_norm_kernel(x_ref, g_ref, o_ref):
    x = x_ref[...]
    o_ref[...] = x * lax.rsqrt(jnp.mean(x * x, axis=-1, keepdims=True) + NORM_EPS) * g_ref[...]


def final_norm(xs, g):
    b, s, d = xs.shape
    tr = CTX_LEN
    nblk = s // tr - 1
    return pl.pallas_call(
        _final_norm_kernel,
        out_shape=jax.ShapeDtypeStruct((b, s - CTX_LEN, d), F32),
        grid=(b, nblk),
        in_specs=[pl.BlockSpec((None, tr, d), lambda bi, i: (bi, i + 1, 0)),
                  pl.BlockSpec((1, d), lambda bi, i: (0, 0))],
        out_specs=pl.BlockSpec((None, tr, d), lambda bi, i: (bi, i, 0)),
        compiler_params=_params(("parallel", "parallel")),
        name="final_norm",
    )(xs, g.reshape(1, d))


def _mm_kernel(x_ref, w_ref, o_ref):
    o_ref[...] = jnp.dot(x_ref[...], w_ref[...], preferred_element_type=F32).astype(o_ref.dtype)


def _mm_residual_kernel(x_ref, w_ref, r_ref, g_ref, o_ref, *, groups, group_rows):
    acc = jnp.dot(x_ref[...], w_ref[...], preferred_element_type=F32)
    for gi in range(groups):
        rows = slice(gi * group_rows, (gi + 1) * group_rows)
        o_ref[rows, :] = r_ref[rows, :] + g_ref[gi] * acc[rows, :]


def matmul(x, w):
    m, k = x.shape
    n = w.shape[1]
    bm = _pick(m, (768, 512, 256, 128))
    bn = _pick(n, (1024, 512, 256, 128))
    return pl.pallas_call(
        _mm_kernel,
        out_shape=jax.ShapeDtypeStruct((m, n), BF16),
        grid=(n // bn, m // bm),
        in_specs=[pl.BlockSpec((bm, k), lambda j, i: (i, 0)),
                  pl.BlockSpec((k, bn), lambda j, i: (0, j))],
        out_specs=pl.BlockSpec((bm, bn), lambda j, i: (i, j)),
        compiler_params=_params(("parallel", "parallel")),
        name="projection",
    )(x, w)


def matmul_residual(x, w, res, gate_groups):
    m, k = x.shape
    n = w.shape[1]
    bm = _pick(m, (768, 512, 256))
    bn = _pick(n, (512, 256, 128))
    groups = bm // CTX_LEN
    return pl.pallas_call(
        functools.partial(_mm_residual_kernel, groups=groups, group_rows=CTX_LEN),
        out_shape=jax.ShapeDtypeStruct((m, n), F32),
        grid=(n // bn, m // bm),
        in_specs=[pl.BlockSpec((bm, k), lambda j, i: (i, 0)),
                  pl.BlockSpec((k, bn), lambda j, i: (0, j)),
                  pl.BlockSpec((bm, bn), lambda j, i: (i, j)),
                  pl.BlockSpec((groups, 1, bn), lambda j, i: (i, 0, j))],
        out_specs=pl.BlockSpec((bm, bn), lambda j, i: (i, j)),
        input_output_aliases={2: 0},
        compiler_params=_params(("parallel", "parallel")),
        name="out_projection_residual",
    )(x, w, res, gate_groups)


def rope_tables(seq, head_dim):
    half = head_dim // 2
    n = half // 2
    t = jnp.arange(seq)
    rows, cols = (t // GRID_W).astype(F32), (t % GRID_W).astype(F32)
    freqs = ROPE_THETA ** (-jnp.arange(n, dtype=F32) / n)
    ang = jnp.concatenate([rows[:, None] * freqs, cols[:, None] * freqs], axis=-1)
    cos, sin = jnp.cos(ang), jnp.sin(ang)
    cos = jnp.concatenate([jnp.ones((CTX_LEN, half), F32), cos], axis=0)
    sin = jnp.concatenate([jnp.zeros((CTX_LEN, half), F32), sin], axis=0)
    return jnp.concatenate([cos, cos], axis=-1), jnp.concatenate([-sin, sin], axis=-1)


def _swap_halves(y, head_dim):
    half = head_dim // 2
    if head_dim == LANES:
        return pltpu.roll(y, half, 1)
    lane = lax.broadcasted_iota(jnp.int32, y.shape, 1)
    first = (lane % head_dim) < half
    return jnp.where(first, pltpu.roll(y, LANES - half, 1), pltpu.roll(y, half, 1))


def _qk_prep_kernel(x_ref, gain_ref, cos_ref, sin_ref, o_ref, *, head_dim, normalise, dup):
    x = x_ref[...].astype(F32)
    cols = x.shape[1]
    cos, sin = cos_ref[...], sin_ref[...]
    gain = gain_ref[...]
    for c in range(cols // LANES):
        y = x[:, c * LANES:(c + 1) * LANES]
        if normalise:
            y = y * lax.rsqrt(jnp.mean(y * y, axis=-1, keepdims=True) + NORM_EPS)
        y = y * gain
        y = (y * cos + _swap_halves(y, head_dim) * sin).astype(o_ref.dtype)
        if dup:
            lane = lax.broadcasted_iota(jnp.int32, y.shape, 1)
            other = pltpu.roll(y.astype(F32), head_dim, 1).astype(o_ref.dtype)
            o_ref[:, (2 * c) * LANES:(2 * c + 1) * LANES] = jnp.where(lane < head_dim, y, other)
            o_ref[:, (2 * c + 1) * LANES:(2 * c + 2) * LANES] = jnp.where(lane < head_dim, other, y)
        else:
            o_ref[:, c * LANES:(c + 1) * LANES] = y


def qk_prepare(qkv, col0, ncols, gain, cos, sin, head_dim, normalise, dup=False):
    b, s, _ = qkv.shape
    tr = _pick(s, (1408, 1024, 768, 512, 256))
    bc = _pick(ncols, (512, 256, 128))
    c0 = col0 // bc
    assert col0 % bc == 0
    mult = 2 if dup else 1
    return pl.pallas_call(
        functools.partial(_qk_prep_kernel, head_dim=head_dim, normalise=normalise, dup=dup),
        out_shape=jax.ShapeDtypeStruct((b, s, ncols * mult), BF16),
        grid=(b, s // tr, ncols // bc),
        in_specs=[pl.BlockSpec((None, tr, bc), lambda bi, i, j: (bi, i, c0 + j)),
                  pl.BlockSpec((1, LANES), lambda bi, i, j: (0, 0)),
                  pl.BlockSpec((tr, LANES), lambda bi, i, j: (i, 0)),
                  pl.BlockSpec((tr, LANES), lambda bi, i, j: (i, 0))],
        out_specs=pl.BlockSpec((None, tr, bc * mult), lambda bi, i, j: (bi, i, j)),
        compiler_params=_params(("parallel", "parallel", "parallel")),
        name="qk_prepare",
    )(qkv, gain, cos, sin)


def _dot_nt(a, b):
    return lax.dot_general(a, b, (((1,), (1,)), ((), ())), preferred_element_type=F32)


def _ctx_attn_kernel(q_ref, k_ref, v_ref, buf_ref, o_ref, *, scale):
    del buf_ref
    s = _dot_nt(q_ref[...], k_ref[...]) * scale
    m = jnp.max(s, axis=-1, keepdims=True)
    p = jnp.exp(s - m)
    l = jnp.sum(p, axis=-1, keepdims=True)
    o = jnp.dot(p.astype(BF16), v_ref[...], preferred_element_type=F32) / l
    o_ref[...] = o.astype(o_ref.dtype)


def ctx_self_attention(o_buf, q_arr, q_c0, k_arr, k_c0, v_arr, v_c0, n_heads, group, scale):
    b, s, dm = o_buf.shape
    return pl.pallas_call(
        functools.partial(_ctx_attn_kernel, scale=scale),
        out_shape=jax.ShapeDtypeStruct(o_buf.shape, o_buf.dtype),
        grid=(b, n_heads),
        in_specs=[pl.BlockSpec((None, CTX_LEN, LANES), lambda bi, h: (bi, 0, q_c0 + h)),
                  pl.BlockSpec((None, CTX_LEN, LANES), lambda bi, h: (bi, 0, k_c0 + h // group)),
                  pl.BlockSpec((None, CTX_LEN, LANES), lambda bi, h: (bi, 0, v_c0 + h // group)),
                  pl.BlockSpec(memory_space=pl.ANY)],
        out_specs=pl.BlockSpec((None, CTX_LEN, LANES), lambda bi, h: (bi, 0, h)),
        input_output_aliases={3: 0},
        compiler_params=_params(("parallel", "parallel")),
        name="ctx_self_attention",
    )(q_arr, k_arr, v_arr, o_buf)


def _na_kernel(q_ref, k_ref, v_ref, b_ref, o_ref, *, scale, n_rows):
    blk = pl.program_id(2)
    base = jnp.clip(NA_ROW_BLOCK * blk - NA_KH // 2, 0, n_rows - NA_KEY_ROWS)
    start = pl.multiple_of(CTX_LEN + base * GRID_W, GRID_W)
    nwin = NA_KEY_ROWS * GRID_W
    q = q_ref[...]
    s_w = _dot_nt(q, k_ref[pl.ds(start, nwin), :]) * scale + b_ref[...]
    s_c = _dot_nt(q, k_ref[0:CTX_LEN, :]) * scale
    m = jnp.maximum(jnp.max(s_w, axis=-1, keepdims=True), jnp.max(s_c, axis=-1, keepdims=True))
    p_w = jnp.exp(s_w - m)
    p_c = jnp.exp(s_c - m)
    l = jnp.sum(p_w, axis=-1, keepdims=True) + jnp.sum(p_c, axis=-1, keepdims=True)
    o = (jnp.dot(p_w.astype(BF16), v_ref[pl.ds(start, nwin), :], preferred_element_type=F32)
         + jnp.dot(p_c.astype(BF16), v_ref[0:CTX_LEN, :], preferred_element_type=F32))
    o_ref[...] = (o / l).astype(o_ref.dtype)


def _na_bias_tables(rpb, n_rows):
    rb, kr, w = NA_ROW_BLOCK, NA_KEY_ROWS, GRID_W
    kh = min(NA_KH, n_rows)
    nblk = n_rows // rb
    tables = []
    for blk in range(nblk):
        base = int(np.clip(rb * blk - NA_KH // 2, 0, n_rows - kr))
        r = rb * blk + np.arange(rb)
        r0 = np.clip(r - kh // 2, 0, n_rows - kh)
        krow = base + np.arange(kr)
        row_ok = (krow[None, :] >= r0[:, None]) & (krow[None, :] < r0[:, None] + kh)
        row_idx = np.clip(krow[None, :] - r[:, None] + NA_KH - 1, 0, 2 * NA_KH - 2)
        qc = np.arange(w)
        c0 = np.clip(qc - NA_KW // 2, 0, w - NA_KW)
        kc = np.arange(w)
        col_ok = (kc[None, :] >= c0[:, None]) & (kc[None, :] < c0[:, None] + NA_KW)
        col_idx = np.clip(kc[None, :] - qc[:, None], 1 - NA_KW, NA_KW - 1) + NA_KW - 1
        ok = row_ok[:, None, :, None] & col_ok[None, :, None, :]
        ri = np.broadcast_to(row_idx[:, None, :, None], ok.shape)
        ci = np.broadcast_to(col_idx[None, :, None, :], ok.shape)
        tables.append((base, ok.reshape(rb * w, kr * w), ri.reshape(rb * w, kr * w), ci.reshape(rb * w, kr * w)))
    for blk in range(2, nblk - 1):
        assert tables[blk][0] - rb * blk == tables[1][0] - rb and all(
            np.array_equal(tables[blk][i], tables[1][i]) for i in (1, 2, 3))
    out = []
    for blk in (0, 1, nblk - 1):
        _, ok, ri, ci = tables[blk]
        out.append(jnp.where(ok[None], rpb.astype(F32)[:, ri, ci], NEG_INF))
    return jnp.stack(out, axis=0)


def neighbourhood_attention(qkv, rpb, n_heads):
    b, s, _ = qkv.shape
    n_rows = (s - CTX_LEN) // GRID_W
    nblk = n_rows // NA_ROW_BLOCK
    tq = NA_ROW_BLOCK * GRID_W
    assert tq == CTX_LEN
    nwin = NA_KEY_ROWS * GRID_W
    bias = _na_bias_tables(rpb, n_rows)
    scale = LANES ** -0.5

    def bias_case(blk):
        return jnp.where(blk == 0, 0, jnp.where(blk == nblk - 1, 2, 1))

    return pl.pallas_call(
        functools.partial(_na_kernel, scale=scale, n_rows=n_rows),
        out_shape=jax.ShapeDtypeStruct((b, s, n_heads * LANES), BF16),
        grid=(b, n_heads, nblk),
        in_specs=[pl.BlockSpec((None, tq, LANES), lambda bi, h, i: (bi, i + 1, h)),
                  pl.BlockSpec((None, s, LANES), lambda bi, h, i: (bi, 0, n_heads + h)),
                  pl.BlockSpec((None, s, LANES), lambda bi, h, i: (bi, 0, 2 * n_heads + h)),
                  pl.BlockSpec((None, None, tq, nwin), lambda bi, h, i: (bias_case(i), h, 0, 0))],
        out_specs=pl.BlockSpec((None, tq, LANES), lambda bi, h, i: (bi, i + 1, h)),
        compiler_params=_params(("parallel", "parallel", "arbitrary")),
        name="neighbourhood_attention",
    )(qkv, qkv, qkv, bias)


def _gqa_kernel(q_ref, k_ref, v_ref, o_ref, m_sc, l_sc, acc_sc, *, tk, group):
    tq = q_ref.shape[0]
    q = jnp.concatenate([q_ref[:, g * LANES:(g + 1) * LANES] for g in range(group)], axis=0)
    m_sc[...] = jnp.full(m_sc.shape, -jnp.inf, F32)
    l_sc[...] = jnp.zeros(l_sc.shape, F32)
    acc_sc[...] = jnp.zeros(acc_sc.shape, F32)

    def body(c, carry):
        rows = pl.ds(pl.multiple_of(c * tk, tk), tk)
        s = _dot_nt(q, k_ref[rows, :])
        m_prev = m_sc[...]
        m_new = jnp.maximum(m_prev, jnp.max(s, axis=-1, keepdims=True))
        alpha = jnp.exp(m_prev - m_new)
        p = jnp.exp(s - m_new)
        l_sc[...] = alpha * l_sc[...] + jnp.sum(p, axis=-1, keepdims=True)
        acc_sc[...] = alpha * acc_sc[...] + jnp.dot(p.astype(BF16), v_ref[rows, :], preferred_element_type=F32)
        m_sc[...] = m_new
        return carry

    lax.fori_loop(0, k_ref.shape[0] // tk, body, 0)
    out = acc_sc[...] / l_sc[...]
    for g in range(group):
        o_ref[:, g * LANES:(g + 1) * LANES] = out[g * tq:(g + 1) * tq, :].astype(o_ref.dtype)


def global_gqa_attention(qp, kp, qkv, n_heads, n_kv):
    b, s, _ = qp.shape
    group = n_heads // n_kv
    tq = CTX_LEN
    nq = s // tq - 1
    tk = _pick(s, (768, 512, 256))
    v_c0 = n_heads + n_kv
    return pl.pallas_call(
        functools.partial(_gqa_kernel, tk=tk, group=group),
        out_shape=jax.ShapeDtypeStruct((b, s, n_heads * LANES), BF16),
        grid=(b, n_kv, nq),
        in_specs=[pl.BlockSpec((None, tq, group * LANES), lambda bi, h, i: (bi, i + 1, h)),
                  pl.BlockSpec((None, s, LANES), lambda bi, h, i: (bi, 0, h)),
                  pl.BlockSpec((None, s, LANES), lambda bi, h, i: (bi, 0, v_c0 + h))],
        out_specs=pl.BlockSpec((None, tq, group * LANES), lambda bi, h, i: (bi, i + 1, h)),
        scratch_shapes=[pltpu.VMEM((group * tq, 1), F32), pltpu.VMEM((group * tq, 1), F32),
                        pltpu.VMEM((group * tq, LANES), F32)],
        compiler_params=_params(("parallel", "parallel", "arbitrary")),
        name="global_gqa_attention",
    )(qp, kp, qkv)


def _swa_kernel(q_ref, k_ref, v_ref, sink_ref, o_ref, *, head_dim, n_lat):
    i = pl.program_id(2)
    tq = q_ref.shape[0]
    n_ctx_blocks = CTX_LEN // SWA_BLOCK
    band = SWA_BLOCK + 2 * SWA_WINDOW

    @pl.when(i < n_ctx_blocks)
    def _():
        o_ref[...] = jnp.zeros(o_ref.shape, o_ref.dtype)

    @pl.when(i >= n_ctx_blocks)
    def _():
        qb = i - n_ctx_blocks
        row0 = jnp.clip(CTX_LEN + SWA_BLOCK * qb - SWA_WINDOW, 0, k_ref.shape[0] - band)
        row0 = pl.multiple_of(row0, SWA_BLOCK)
        kb, vb = k_ref[pl.ds(row0, band), :], v_ref[pl.ds(row0, band), :]
        kc, vc = k_ref[0:CTX_LEN, :], v_ref[0:CTX_LEN, :]
        qpos = SWA_BLOCK * qb + lax.broadcasted_iota(jnp.int32, (tq, band), 0)
        kpos = row0 - CTX_LEN + lax.broadcasted_iota(jnp.int32, (tq, band), 1)
        ok = (kpos >= 0) & (kpos < n_lat) & (jnp.abs(kpos - qpos) <= SWA_WINDOW)
        lane = lax.broadcasted_iota(jnp.int32, (tq, LANES), 1)
        heads_per_group = LANES // head_dim
        q_all = q_ref[...]
        for c in range(q_all.shape[1] // LANES):
            qc = q_all[:, c * LANES:(c + 1) * LANES]
            outs = []
            for j in range(heads_per_group):
                sel = (lane >= j * head_dim) & (lane < (j + 1) * head_dim)
                qh = jnp.where(sel, qc, jnp.zeros_like(qc))
                s_b = jnp.where(ok, _dot_nt(qh, kb), NEG_INF)
                s_c = _dot_nt(qh, kc)
                sink = sink_ref[:, c * heads_per_group + j:c * heads_per_group + j + 1]
                m = jnp.maximum(jnp.maximum(jnp.max(s_b, axis=-1, keepdims=True),
                                            jnp.max(s_c, axis=-1, keepdims=True)), sink)
                p_b = jnp.exp(s_b - m)
                p_c = jnp.exp(s_c - m)
                l = jnp.sum(p_b, axis=-1, keepdims=True) + jnp.sum(p_c, axis=-1, keepdims=True) + jnp.exp(sink - m)
                o = (jnp.dot(p_b.astype(BF16), vb, preferred_element_type=F32)
                     + jnp.dot(p_c.astype(BF16), vc, preferred_element_type=F32)) / l
                outs.append((sel, o))
            res = outs[0][1]
            for sel, o in outs[1:]:
                res = jnp.where(sel, o, res)
            o_ref[:, c * LANES:(c + 1) * LANES] = res.astype(o_ref.dtype)


def window_gqa_attention(q, k2, v2, sink, n_heads, n_kv, head_dim):
    b, s, _ = q.shape
    group = n_heads // n_kv
    gcols = group * head_dim
    assert gcols % LANES == 0 and LANES % head_dim == 0
    tq = SWA_BLOCK
    return pl.pallas_call(
        functools.partial(_swa_kernel, head_dim=head_dim, n_lat=s - CTX_LEN),
        out_shape=jax.ShapeDtypeStruct((b, s, n_heads * head_dim), BF16),
        grid=(b, n_kv, s // tq),
        in_specs=[pl.BlockSpec((None, tq, gcols), lambda bi, h, i: (bi, i, h)),
                  pl.BlockSpec((None, s, LANES), lambda bi, h, i: (bi, 0, h)),
                  pl.BlockSpec((None, s, LANES), lambda bi, h, i: (bi, 0, h)),
                  pl.BlockSpec((None, 1, group), lambda bi, h, i: (h, 0, 0))],
        out_specs=pl.BlockSpec((None, tq, gcols), lambda bi, h, i: (bi, i, h)),
        compiler_params=_params(("parallel", "parallel", "arbitrary")),
        name="window_gqa_attention",
    )(q, k2, v2, sink.astype(F32).reshape(n_kv, 1, group))


def _retention_kernel(q_ref, k_ref, v_ref, cos_ref, sin_ref, intra_ref, qdec_ref, kdec_ref, cdec_ref,
                      o_ref, state_sc, *, k_scale):
    @pl.when(pl.program_id(3) == 0)
    def _():
        state_sc[...] = jnp.zeros(state_sc.shape, F32)

    cos, sin = cos_ref[...], sin_ref[...]
    half = cos.shape[1] // 2

    def rope(x):
        swapped = jnp.concatenate([x[:, half:], x[:, :half]], axis=1)
        return x * cos + swapped * sin

    q = rope(q_ref[...].astype(F32))
    k = rope(k_ref[...].astype(F32)) * k_scale
    v = v_ref[...]
    state = state_sc[...]
    att = _dot_nt(q.astype(BF16), k.astype(BF16)) * intra_ref[...]
    y = (jnp.dot(att.astype(BF16), v, preferred_element_type=F32)
         + jnp.dot((q * qdec_ref[...]).astype(BF16), state.astype(BF16), preferred_element_type=F32))
    kd = (k * kdec_ref[...]).astype(BF16)
    state_sc[...] = state * cdec_ref[...] + lax.dot_general(kd, v, (((0,), (0,)), ((), ())),
                                                            preferred_element_type=F32)
    y = y * lax.rsqrt(jnp.mean(y * y, axis=-1, keepdims=True) + NORM_EPS)
    o_ref[...] = y.astype(o_ref.dtype)


def retention_scan(proj, decay_exp, cos, sin, n_heads):
    b, s, five_d = proj.shape
    d = five_d // 5
    dh = d // n_heads
    c = RET_CHUNK
    n_chunks = s // c
    n_ctx_chunks = CTX_LEN // c
    lg = jnp.log1p(-jnp.exp2(-decay_exp.astype(F32)))
    pos = jnp.arange(c, dtype=F32)
    diff = pos[:, None] - pos[None, :]
    lgf, lgb = lg[0][:, None, None], lg[1][:, None, None]
    intra_f = jnp.where(diff >= 0, jnp.exp(lgf * jnp.maximum(diff, 0.0)), 0.0)
    intra_b = jnp.where(diff <= 0, jnp.exp(lgb * jnp.maximum(-diff, 0.0)), 0.0)
    intra = jnp.stack([intra_f, intra_b])
    qdec = jnp.stack([jnp.exp(lg[0][:, None] * (pos + 1.0)), jnp.exp(lg[1][:, None] * (c - pos))])[..., None]
    kdec = jnp.stack([jnp.exp(lg[0][:, None] * (c - 1.0 - pos)), jnp.exp(lg[1][:, None] * pos)])[..., None]
    cdec = jnp.exp(lg * c)[..., None, None]
    hb = dh // LANES

    def chunk_of(dr, t):
        bwd = jnp.where(t < n_ctx_chunks, n_ctx_chunks - 1 - t, n_chunks - 1 - (t - n_ctx_chunks))
        return jnp.where(dr == 0, t, bwd)

    def col(off):
        return lambda dr, bi, h, t: (bi, chunk_of(dr, t), off * n_heads + h)

    tab = lambda dr, bi, h, t: (dr, h, 0, 0)
    rot = lambda dr, bi, h, t: (chunk_of(dr, t), 0)
    return pl.pallas_call(
        functools.partial(_retention_kernel, k_scale=dh ** -0.5),
        out_shape=jax.ShapeDtypeStruct((2, b, s, d), BF16),
        grid=(2, b, n_heads, n_chunks),
        in_specs=[pl.BlockSpec((None, c, dh), col(0)),
                  pl.BlockSpec((None, c, dh), col(1)),
                  pl.BlockSpec((None, c, dh), col(2)),
                  pl.BlockSpec((c, dh), rot),
                  pl.BlockSpec((c, dh), rot),
                  pl.BlockSpec((None, None, c, c), tab),
                  pl.BlockSpec((None, None, c, 1), tab),
                  pl.BlockSpec((None, None, c, 1), tab),
                  pl.BlockSpec((None, None, 1, 1), tab)],
        out_specs=pl.BlockSpec((None, None, c, dh), lambda dr, bi, h, t: (dr, bi, chunk_of(dr, t), h)),
        scratch_shapes=[pltpu.VMEM((dh, dh), F32)],
        compiler_params=_params(("parallel", "parallel", "parallel", "arbitrary")),
        name="retention_scan",
    )(proj, proj, proj, cos, sin, intra, qdec, kdec, cdec)


def _ret_merge_kernel(yf_ref, yb_ref, gf_ref, gb_ref, o_ref):
    gf = gf_ref[...].astype(F32)
    gb = gb_ref[...].astype(F32)
    o = gf * jax.nn.sigmoid(gf) * yf_ref[...].astype(F32) + gb * jax.nn.sigmoid(gb) * yb_ref[...].astype(F32)
    o_ref[...] = o.astype(o_ref.dtype)


def retention_merge(y, proj):
    _, b, s, d = y.shape
    tr = _pick(s, (1408, 1024, 768, 512, 256))
    bc = _pick(d, (1024, 512, 256, 128))
    nc = d // bc
    return pl.pallas_call(
        _ret_merge_kernel,
        out_shape=jax.ShapeDtypeStruct((b, s, d), BF16),
        grid=(b, s // tr, nc),
        in_specs=[pl.BlockSpec((None, None, tr, bc), lambda bi, i, j: (0, bi, i, j)),
                  pl.BlockSpec((None, None, tr, bc), lambda bi, i, j: (1, bi, i, j)),
                  pl.BlockSpec((None, tr, bc), lambda bi, i, j: (bi, i, 3 * nc + j)),
                  pl.BlockSpec((None, tr, bc), lambda bi, i, j: (bi, i, 4 * nc + j))],
        out_specs=pl.BlockSpec((None, tr, bc), lambda bi, i, j: (bi, i, j)),
        compiler_params=_params(("parallel", "parallel", "parallel")),
        name="retention_merge",
    )(y, y, proj, proj)


def _moe_kernel(be_ref, x_ref, w1g_ref, w1l_ref, b1g_ref, b1l_ref, w2_ref, b2_ref, g_ref, o_ref):
    x = x_ref[...]
    glu = jnp.dot(x, w1g_ref[...], preferred_element_type=F32) + b1g_ref[...]
    lin = jnp.dot(x, w1l_ref[...], preferred_element_type=F32) + b1l_ref[...]
    glu = jnp.minimum(glu, SWIGLU_LIMIT)
    lin = jnp.clip(lin, -SWIGLU_LIMIT, SWIGLU_LIMIT)
    a = glu * jax.nn.sigmoid(SWIGLU_ALPHA * glu) * (lin + 1.0)
    y = jnp.dot(a.astype(BF16), w2_ref[...], preferred_element_type=F32) + b2_ref[...]
    o_ref[...] = (y * g_ref[...]).astype(o_ref.dtype)


def moe_experts(xg, blk_expert, slot_gate, w1g, w1l, b1g, b1l, w2, b2):
    n_slots, d = xg.shape
    f = w1g.shape[2]
    n_blocks = n_slots // MOE_BLOCK
    grid_spec = pltpu.PrefetchScalarGridSpec(
        num_scalar_prefetch=1,
        grid=(n_blocks,),
        in_specs=[pl.BlockSpec((MOE_BLOCK, d), lambda i, be: (i, 0)),
                  pl.BlockSpec((None, d, f), lambda i, be: (be[i], 0, 0)),
                  pl.BlockSpec((None, d, f), lambda i, be: (be[i], 0, 0)),
                  pl.BlockSpec((None, 1, f), lambda i, be: (be[i], 0, 0)),
                  pl.BlockSpec((None, 1, f), lambda i, be: (be[i], 0, 0)),
                  pl.BlockSpec((None, f, d), lambda i, be: (be[i], 0, 0)),
                  pl.BlockSpec((None, 1, d), lambda i, be: (be[i], 0, 0)),
                  pl.BlockSpec((MOE_BLOCK, 1), lambda i, be: (i, 0))],
        out_specs=pl.BlockSpec((MOE_BLOCK, d), lambda i, be: (i, 0)),
    )
    return pl.pallas_call(
        _moe_kernel,
        out_shape=jax.ShapeDtypeStruct((n_slots, d), BF16),
        grid_spec=grid_spec,
        compiler_params=_params(("arbitrary",)),
        name="moe_experts",
    )(blk_expert, xg, w1g, w1l, b1g, b1l, w2, b2, slot_gate)


def moe_ffn(h, top_idx, top_gate, w1, b1, w2, b2):
    t_tok, d = h.shape
    n_e = w1.shape[0]
    n_assign = t_tok * TOP_K
    e_flat = top_idx.reshape(n_assign)
    t_flat = jnp.repeat(jnp.arange(t_tok, dtype=jnp.int32), TOP_K)
    order = jnp.argsort(e_flat)
    e_s, t_s, g_s = e_flat[order], t_flat[order], top_gate.reshape(n_assign)[order]
    counts = jnp.bincount(e_flat, length=n_e)
    padded = (counts + MOE_BLOCK - 1) // MOE_BLOCK * MOE_BLOCK
    pad_end = jnp.cumsum(padded)
    pad_start = pad_end - padded
    start = jnp.cumsum(counts) - counts
    slot = (pad_start[e_s] + jnp.arange(n_assign) - start[e_s]).astype(jnp.int32)
    n_blocks = -(-n_assign // MOE_BLOCK) + n_e
    n_slots = n_blocks * MOE_BLOCK
    slot_tok = jnp.zeros((n_slots,), jnp.int32).at[slot].set(t_s)
    slot_gate = jnp.zeros((n_slots,), F32).at[slot].set(g_s)
    slot_of = jnp.zeros((n_assign,), jnp.int32).at[order].set(slot).reshape(t_tok, TOP_K)
    blk_start = jnp.arange(n_blocks) * MOE_BLOCK
    blk_expert = jnp.minimum(jnp.sum(pad_end[None, :] <= blk_start[:, None], axis=1), n_e - 1).astype(jnp.int32)
    xg = jnp.take(h, slot_tok, axis=0)
    f = w1.shape[2] // 2
    ys = moe_experts(xg, blk_expert, slot_gate.reshape(n_slots, 1),
                     w1[:, :, 0::2].astype(BF16), w1[:, :, 1::2].astype(BF16),
                     b1[:, 0::2].astype(F32).reshape(n_e, 1, f), b1[:, 1::2].astype(F32).reshape(n_e, 1, f),
                     w2.astype(BF16), b2.astype(F32).reshape(n_e, 1, d))
    return jnp.sum(jnp.take(ys, slot_of, axis=0).astype(F32), axis=1)


def _group_rows(vec_ctx, vec_lat, s):
    b, d = vec_lat.shape
    ng = s // CTX_LEN
    g = jnp.concatenate([jnp.broadcast_to(vec_ctx[None, None, :], (b, 1, d)),
                         jnp.broadcast_to(vec_lat[:, None, :], (b, ng - 1, d))], axis=1)
    return g.reshape(b * ng, 1, d)


def kernel(x, c, ctx, c_ctx, w_mod, b_mod, g_mix, g_ffn, g_final, na_w_qkv, na_w_o, na_rpb, ret_w_in, ret_w_o, ret_decay_exp, gqa_w_qkv, gqa_w_o, gqa_q_gain, gqa_k_gain, swa_w_qkv, swa_w_o, swa_sink, moe_w_router, moe_b_router, moe_w1, moe_b1, moe_w2, moe_b2):
    b, l, d = x.shape
    assert ctx.shape[1] == CTX_LEN and l % CTX_LEN == 0
    s = CTX_LEN + l
    depth = w_mod.shape[0]
    xs = jnp.concatenate([ctx, x], axis=1)

    n_rows = -(-(b + 1) // 16) * 16
    s_rows = jnp.zeros((n_rows, d), F32).at[:b].set(jax.nn.silu(c)).at[b].set(jax.nn.silu(c_ctx))
    mod = modulation_all_layers(s_rows, w_mod, b_mod).reshape(depth, n_rows, N_MOD, d)

    ones = jnp.ones((1, LANES), F32)
    for li in range(depth):
        mix, j = li % N_MIXERS, li // N_MIXERS
        need_ctx = li < depth - 1
        mod_lat, mod_ctx = mod[li, :b], mod[li, b]
        modv = jnp.stack([jnp.broadcast_to(mod_ctx[None], (b, N_MOD, d)), mod_lat], axis=1)
        h = norm_modulate(xs, g_mix[li], modv, 0)
        h2 = h.reshape(b * s, d)
        if mix == 0:
            qkv = matmul(h2, na_w_qkv[j].astype(BF16)).reshape(b, s, 3 * d)
            o = neighbourhood_attention(qkv, na_rpb[j], NA_HEADS)
            o = ctx_self_attention(o, qkv, 0, qkv, NA_HEADS, qkv, 2 * NA_HEADS, NA_HEADS, 1, LANES ** -0.5)
            w_o = na_w_o[j]
        elif mix == 1:
            proj = matmul(h2, ret_w_in[j].astype(BF16)).reshape(b, s, 5 * d)
            cos, sin = rope_tables(l, d // RET_HEADS)
            y = retention_scan(proj, ret_decay_exp[j], cos, sin, RET_HEADS)
            o = retention_merge(y, proj)
            w_o = ret_w_o[j]
        elif mix == 2:
            dh = d // GQA_HEADS
            assert dh == LANES
            qkv = matmul(h2, gqa_w_qkv[j].astype(BF16)).reshape(b, s, -1)
            cos, sin = rope_tables(l, dh)
            qg = (gqa_q_gain[j].astype(F32) * dh ** -0.5).reshape(1, LANES)
            kg = gqa_k_gain[j].astype(F32).reshape(1, LANES)
            qp = qk_prepare(qkv, 0, GQA_HEADS * dh, qg, cos, sin, dh, True)
            kp = qk_prepare(qkv, GQA_HEADS * dh, GQA_KV_HEADS * dh, kg, cos, sin, dh, True)
            o = global_gqa_attention(qp, kp, qkv, GQA_HEADS, GQA_KV_HEADS)
            o = ctx_self_attention(o, qp, 0, kp, 0, qkv, GQA_HEADS + GQA_KV_HEADS, GQA_HEADS,
                                   GQA_HEADS // GQA_KV_HEADS, 1.0)
            w_o = gqa_w_o[j]
        else:
            dh = d // SWA_HEADS
            qkv = matmul(h2, swa_w_qkv[j].astype(BF16)).reshape(b, s, -1)
            cos, sin = rope_tables(l, dh)
            reps = LANES // dh
            cos, sin = jnp.tile(cos, (1, reps)), jnp.tile(sin, (1, reps))
            unit = jnp.zeros((s, LANES), F32)
            qp = qk_prepare(qkv, 0, SWA_HEADS * dh, ones * dh ** -0.5, cos, sin, dh, False)
            k2 = qk_prepare(qkv, SWA_HEADS * dh, SWA_KV_HEADS * dh, ones, cos, sin, dh, False, dup=True)
            v2 = qk_prepare(qkv, (SWA_HEADS + SWA_KV_HEADS) * dh, SWA_KV_HEADS * dh, ones,
                            jnp.ones((s, LANES), F32), unit, dh, False, dup=True)
            o = window_gqa_attention(qp, k2, v2, swa_sink[j], SWA_HEADS, SWA_KV_HEADS, dh)
            w_o = swa_w_o[j]
        if mix in (0, 2) and not need_ctx:
            o = o.at[:, :CTX_LEN].set(0)
        xs = matmul_residual(o.reshape(b * s, d), w_o.astype(BF16), xs.reshape(b * s, d),
                             _group_rows(mod_ctx[2], mod_lat[:, 2], s)).reshape(b, s, d)

        h, top_idx, top_gate = norm_modulate_route(xs, g_ffn[li], modv, 3, moe_w_router[li], moe_b_router[li])
        y = moe_ffn(h.reshape(b * s, d), top_idx.reshape(b * s, LANES)[:, :TOP_K],
                    top_gate.reshape(b * s, LANES)[:, :TOP_K], moe_w1[li], moe_b1[li], moe_w2[li], moe_b2[li])
        gate5 = _group_rows(mod_ctx[5], mod_lat[:, 5], s)
        xs = xs + (gate5 * y.reshape(-1, CTX_LEN, d)).reshape(b, s, d)
    return final_norm(xs, g_final)
```

```python
import functools

import jax
import jax.numpy as jnp
import numpy as np
from jax import lax
from jax.experimental import pallas as pl
from jax.experimental.pallas import tpu as pltpu

DEPTH = 4
GRID_W = 64
CTX_LEN = 256
N_MIXERS = 4
N_MOD = 6
NORM_EPS = 1e-6
NEG_INF = -1e30
ROPE_THETA = 10000.0

NA_HEADS = 32
NA_KH = 8
NA_KW = 16
NA_ROW_BLOCK = 4
NA_KEY_ROWS = 12
NA_HEADS_PER_STEP = 4

RET_HEADS = 16
RET_CHUNK = 128
RET_HEADS_PER_STEP = 4

GQA_HEADS = 32
GQA_KV_HEADS = 8

SWA_HEADS = 64
SWA_KV_HEADS = 8
SWA_WINDOW = 128
SWA_BLOCK = 128
SWA_GROUPS_PER_CHAIN = 1

N_EXPERTS = 32
TOP_K = 4
SWIGLU_ALPHA = 1.702
SWIGLU_LIMIT = 7.0
MOE_BLOCK = 256

LOG2E = 1.4426950408889634
LANES = 128
VMEM_LIMIT_BYTES = 56 * 1024 * 1024

F32 = jnp.float32
BF16 = jnp.bfloat16


def _params(semantics, vmem=VMEM_LIMIT_BYTES):
    return pltpu.CompilerParams(dimension_semantics=semantics, vmem_limit_bytes=vmem)


def _pick(n, candidates):
    for c in candidates:
        if n % c == 0:
            return c
    return n


def _mod_kernel(s_ref, w_ref, b_ref, o_ref):
    acc = jnp.dot(s_ref[...], w_ref[...].astype(BF16), preferred_element_type=F32)
    o_ref[...] = acc + b_ref[...]


def modulation_all_layers(s_rows, w_mod, b_mod):
    depth, d, n = w_mod.shape
    rows = s_rows.shape[0]
    bn = _pick(n, (512, 256, 128))
    return pl.pallas_call(
        _mod_kernel,
        out_shape=jax.ShapeDtypeStruct((depth, rows, n), F32),
        grid=(depth, n // bn),
        in_specs=[
            pl.BlockSpec((rows, d), lambda l, j: (0, 0)),
            pl.BlockSpec((None, d, bn), lambda l, j: (l, 0, j)),
            pl.BlockSpec((None, 1, bn), lambda l, j: (l, 0, j)),
        ],
        out_specs=pl.BlockSpec((None, rows, bn), lambda l, j: (l, 0, j)),
        compiler_params=_params(("parallel", "parallel")),
        name="modulation",
    )(s_rows.astype(BF16), w_mod, b_mod.reshape(depth, 1, n))


def _rms_mod(x, g, shift, scale):
    y = x * lax.rsqrt(jnp.mean(x * x, axis=-1, keepdims=True) + NORM_EPS) * g
    return y * (1.0 + scale) + shift


def _norm_mod_kernel(x_ref, g_ref, m_ref, o_ref, *, shift_idx):
    h = _rms_mod(x_ref[...], g_ref[...], m_ref[shift_idx:shift_idx + 1, :], m_ref[shift_idx + 1:shift_idx + 2, :])
    o_ref[...] = h.astype(o_ref.dtype)


def _norm_route_kernel(x_ref, g_ref, m_ref, wr_ref, br_ref, o_ref, idx_ref, gate_ref, *, shift_idx):
    h = _rms_mod(x_ref[...], g_ref[...], m_ref[shift_idx:shift_idx + 1, :], m_ref[shift_idx + 1:shift_idx + 2, :])
    o_ref[...] = h.astype(o_ref.dtype)
    logits = jnp.dot(h, wr_ref[...], precision=lax.Precision.HIGHEST, preferred_element_type=F32) + br_ref[...]
    rows, n_e = logits.shape
    e_iota = lax.broadcasted_iota(jnp.int32, (rows, n_e), 1).astype(F32)
    lane = lax.broadcasted_iota(jnp.int32, (rows, LANES), 1)
    idx_out = jnp.zeros((rows, LANES), F32)
    val_out = jnp.zeros((rows, LANES), F32)
    top0 = None
    denom = jnp.zeros((rows, 1), F32)
    work = logits
    for k in range(TOP_K):
        mx = jnp.max(work, axis=-1, keepdims=True)
        ix = jnp.min(jnp.where(work == mx, e_iota, float(n_e)), axis=-1, keepdims=True)
        if k == 0:
            top0 = mx
        ex = jnp.exp(mx - top0)
        denom = denom + ex
        idx_out = jnp.where(lane == k, ix, idx_out)
        val_out = jnp.where(lane == k, ex, val_out)
        work = jnp.where(e_iota == ix, -jnp.inf, work)
    idx_ref[...] = idx_out.astype(jnp.int32)
    gate_ref[...] = val_out / denom


def norm_modulate(xs, g, modv, shift_idx):
    b, s, d = xs.shape
    tr = CTX_LEN
    nblk = s // tr
    return pl.pallas_call(
        functools.partial(_norm_mod_kernel, shift_idx=shift_idx),
        out_shape=jax.ShapeDtypeStruct((b, s, d), BF16),
        grid=(b, nblk),
        in_specs=[
            pl.BlockSpec((None, tr, d), lambda bi, i: (bi, i, 0)),
            pl.BlockSpec((1, d), lambda bi, i: (0, 0)),
            pl.BlockSpec((None, None, N_MOD, d), lambda bi, i: (bi, jnp.minimum(i, 1), 0, 0)),
        ],
        out_specs=pl.BlockSpec((None, tr, d), lambda bi, i: (bi, i, 0)),
        compiler_params=_params(("parallel", "parallel")),
        name="norm_modulate",
    )(xs, g.reshape(1, d), modv)


def norm_modulate_route(xs, g, modv, shift_idx, w_router, b_router):
    b, s, d = xs.shape
    tr = CTX_LEN
    nblk = s // tr
    n_e = w_router.shape[1]
    return pl.pallas_call(
        functools.partial(_norm_route_kernel, shift_idx=shift_idx),
        out_shape=(jax.ShapeDtypeStruct((b, s, d), BF16),
                   jax.ShapeDtypeStruct((b, s, LANES), jnp.int32),
                   jax.ShapeDtypeStruct((b, s, LANES), F32)),
        grid=(b, nblk),
        in_specs=[
            pl.BlockSpec((None, tr, d), lambda bi, i: (bi, i, 0)),
            pl.BlockSpec((1, d), lambda bi, i: (0, 0)),
            pl.BlockSpec((None, None, N_MOD, d), lambda bi, i: (bi, jnp.minimum(i, 1), 0, 0)),
            pl.BlockSpec((d, n_e), lambda bi, i: (0, 0)),
            pl.BlockSpec((1, n_e), lambda bi, i: (0, 0)),
        ],
        out_specs=(pl.BlockSpec((None, tr, d), lambda bi, i: (bi, i, 0)),
                   pl.BlockSpec((None, tr, LANES), lambda bi, i: (bi, i, 0)),
                   pl.BlockSpec((None, tr, LANES), lambda bi, i: (bi, i, 0))),
        compiler_params=_params(("parallel", "parallel")),
        name="norm_modulate_route",
    )(xs, g.reshape(1, d), modv, w_router, b_router.reshape(1, n_e))


def _final_norm_kernel(x_ref, g_ref, o_ref):
    x = x_ref[...]
    o_ref[...] = x * lax.rsqrt(jnp.mean(x * x, axis=-1, keepdims=True) + NORM_EPS) * g_ref[...]


def final_norm(xs, g):
    b, s, d = xs.shape
    tr = CTX_LEN
    nblk = s // tr - 1
    return pl.pallas_call(
        _final_norm_kernel,
        out_shape=jax.ShapeDtypeStruct((b, s - CTX_LEN, d), F32),
        grid=(b, nblk),
        in_specs=[pl.BlockSpec((None, tr, d), lambda bi, i: (bi, i + 1, 0)),
                  pl.BlockSpec((1, d), lambda bi, i: (0, 0))],
        out_specs=pl.BlockSpec((None, tr, d), lambda bi, i: (bi, i, 0)),
        compiler_params=_params(("parallel", "parallel")),
        name="final_norm",
    )(xs, g.reshape(1, d))


def _mm_kernel(x_ref, w_ref, o_ref, wb_sc):
    @pl.when(pl.program_id(1) == 0)
    def _():
        wb_sc[...] = w_ref[...].astype(BF16)

    o_ref[...] = jnp.dot(x_ref[...], wb_sc[...], preferred_element_type=F32).astype(o_ref.dtype)


def _mm_residual_kernel(x_ref, w_ref, r_ref, g_ref, o_ref, wb_sc, *, groups, group_rows):
    @pl.when(pl.program_id(1) == 0)
    def _():
        wb_sc[...] = w_ref[...].astype(BF16)

    acc = jnp.dot(x_ref[...], wb_sc[...], preferred_element_type=F32)
    for gi in range(groups):
        rows = slice(gi * group_rows, (gi + 1) * group_rows)
        o_ref[rows, :] = r_ref[rows, :] + g_ref[gi] * acc[rows, :]


def matmul(x, w):
    m, k = x.shape
    n = w.shape[1]
    bm = _pick(m, (1536, 768, 512, 256, 128))
    bn = _pick(n, (512, 256, 128))
    return pl.pallas_call(
        _mm_kernel,
        out_shape=jax.ShapeDtypeStruct((m, n), BF16),
        grid=(n // bn, m // bm),
        in_specs=[pl.BlockSpec((bm, k), lambda j, i: (i, 0)),
                  pl.BlockSpec((k, bn), lambda j, i: (0, j))],
        out_specs=pl.BlockSpec((bm, bn), lambda j, i: (i, j)),
        scratch_shapes=[pltpu.VMEM((k, bn), BF16)],
        compiler_params=_params(("parallel", "arbitrary")),
        name="projection",
    )(x, w)


def matmul_residual(x, w, res, gate_groups):
    m, k = x.shape
    n = w.shape[1]
    bm = _pick(m, (768, 512, 256))
    bn = _pick(n, (512, 256, 128))
    groups = bm // CTX_LEN
    return pl.pallas_call(
        functools.partial(_mm_residual_kernel, groups=groups, group_rows=CTX_LEN),
        out_shape=jax.ShapeDtypeStruct((m, n), F32),
        grid=(n // bn, m // bm),
        in_specs=[pl.BlockSpec((bm, k), lambda j, i: (i, 0)),
                  pl.BlockSpec((k, bn), lambda j, i: (0, j)),
                  pl.BlockSpec((bm, bn), lambda j, i: (i, j)),
                  pl.BlockSpec((groups, 1, bn), lambda j, i: (i, 0, j))],
        out_specs=pl.BlockSpec((bm, bn), lambda j, i: (i, j)),
        scratch_shapes=[pltpu.VMEM((k, bn), BF16)],
        input_output_aliases={2: 0},
        compiler_params=_params(("parallel", "arbitrary")),
        name="out_projection_residual",
    )(x, w, res, gate_groups)


def rope_tables(seq, head_dim):
    half = head_dim // 2
    n = half // 2
    t = jnp.arange(seq)
    rows, cols = (t // GRID_W).astype(F32), (t % GRID_W).astype(F32)
    freqs = ROPE_THETA ** (-jnp.arange(n, dtype=F32) / n)
    ang = jnp.concatenate([rows[:, None] * freqs, cols[:, None] * freqs], axis=-1)
    cos, sin = jnp.cos(ang), jnp.sin(ang)
    cos = jnp.concatenate([jnp.ones((CTX_LEN, half), F32), cos], axis=0)
    sin = jnp.concatenate([jnp.zeros((CTX_LEN, half), F32), sin], axis=0)
    return jnp.concatenate([cos, cos], axis=-1), jnp.concatenate([-sin, sin], axis=-1)


def _swap_halves(y, head_dim):
    half = head_dim // 2
    if head_dim == LANES:
        return pltpu.roll(y, half, 1)
    lane = lax.broadcasted_iota(jnp.int32, y.shape, 1)
    first = (lane % head_dim) < half
    return jnp.where(first, pltpu.roll(y, LANES - half, 1), pltpu.roll(y, half, 1))


def _qk_prep_kernel(x_ref, gain_ref, cos_ref, sin_ref, o_ref, *, head_dim, normalise, dup):
    x = x_ref[...].astype(F32)
    cols = x.shape[1]
    cos, sin = cos_ref[...], sin_ref[...]
    gain = gain_ref[...]
    for c in range(cols // LANES):
        y = x[:, c * LANES:(c + 1) * LANES]
        if normalise:
            y = y * lax.rsqrt(jnp.mean(y * y, axis=-1, keepdims=True) + NORM_EPS)
        y = y * gain
        y = (y * cos + _swap_halves(y, head_dim) * sin).astype(o_ref.dtype)
        if dup:
            lane = lax.broadcasted_iota(jnp.int32, y.shape, 1)
            other = pltpu.roll(y.astype(F32), head_dim, 1).astype(o_ref.dtype)
            o_ref[:, (2 * c) * LANES:(2 * c + 1) * LANES] = jnp.where(lane < head_dim, y, other)
            o_ref[:, (2 * c + 1) * LANES:(2 * c + 2) * LANES] = jnp.where(lane < head_dim, other, y)
        else:
            o_ref[:, c * LANES:(c + 1) * LANES] = y


def qk_prepare(qkv, col0, ncols, gain, cos, sin, head_dim, normalise, dup=False):
    b, s, _ = qkv.shape
    tr = _pick(s, (1408, 1024, 768, 512, 256))
    bc = _pick(ncols, (512, 256, 128))
    c0 = col0 // bc
    assert col0 % bc == 0
    mult = 2 if dup else 1
    return pl.pallas_call(
        functools.partial(_qk_prep_kernel, head_dim=head_dim, normalise=normalise, dup=dup),
        out_shape=jax.ShapeDtypeStruct((b, s, ncols * mult), BF16),
        grid=(b, s // tr, ncols // bc),
        in_specs=[pl.BlockSpec((None, tr, bc), lambda bi, i, j: (bi, i, c0 + j)),
                  pl.BlockSpec((1, LANES), lambda bi, i, j: (0, 0)),
                  pl.BlockSpec((tr, LANES), lambda bi, i, j: (i, 0)),
                  pl.BlockSpec((tr, LANES), lambda bi, i, j: (i, 0))],
        out_specs=pl.BlockSpec((None, tr, bc * mult), lambda bi, i, j: (bi, i, j)),
        compiler_params=_params(("parallel", "parallel", "parallel")),
        name="qk_prepare",
    )(qkv, gain, cos, sin)


def _dot_nt(a, b):
    return lax.dot_general(a, b, (((1,), (1,)), ((), ())), preferred_element_type=F32)


def _ctx_attn_kernel(q_ref, k_ref, v_ref, buf_ref, o_ref, *, scale):
    del buf_ref
    s = _dot_nt(q_ref[...], k_ref[...]) * scale
    m = jnp.max(s, axis=-1, keepdims=True)
    p = jnp.exp2(s - m)
    l = jnp.sum(p, axis=-1, keepdims=True)
    o = jnp.dot(p.astype(BF16), v_ref[...], preferred_element_type=F32) / l
    o_ref[...] = o.astype(o_ref.dtype)


def ctx_self_attention(o_buf, q_arr, q_c0, k_arr, k_c0, v_arr, v_c0, n_heads, group, scale):
    b, s, dm = o_buf.shape
    return pl.pallas_call(
        functools.partial(_ctx_attn_kernel, scale=scale),
        out_shape=jax.ShapeDtypeStruct(o_buf.shape, o_buf.dtype),
        grid=(b, n_heads),
        in_specs=[pl.BlockSpec((None, CTX_LEN, LANES), lambda bi, h: (bi, 0, q_c0 + h)),
                  pl.BlockSpec((None, CTX_LEN, LANES), lambda bi, h: (bi, 0, k_c0 + h // group)),
                  pl.BlockSpec((None, CTX_LEN, LANES), lambda bi, h: (bi, 0, v_c0 + h // group)),
                  pl.BlockSpec(memory_space=pl.ANY)],
        out_specs=pl.BlockSpec((None, CTX_LEN, LANES), lambda bi, h: (bi, 0, h)),
        input_output_aliases={3: 0},
        compiler_params=_params(("parallel", "parallel")),
        name="ctx_self_attention",
    )(q_arr, k_arr, v_arr, o_buf)


def _na_kernel(q_ref, k_ref, v_ref, b_ref, o_ref, *, scale, n_rows):
    blk = pl.program_id(2)
    base = jnp.clip(NA_ROW_BLOCK * blk - NA_KH // 2, 0, n_rows - NA_KEY_ROWS)
    start = pl.multiple_of(CTX_LEN + base * GRID_W, GRID_W)
    nwin = NA_KEY_ROWS * GRID_W
    ones_w, ones_c = jnp.ones((nwin, LANES), BF16), jnp.ones((CTX_LEN, LANES), BF16)
    for hh in range(q_ref.shape[1] // LANES):
        cols = slice(hh * LANES, (hh + 1) * LANES)
        q = q_ref[:, cols]
        s_w = _dot_nt(q, k_ref[pl.ds(start, nwin), cols]) * scale + b_ref[hh]
        s_c = _dot_nt(q, k_ref[0:CTX_LEN, cols]) * scale
        m = jnp.maximum(jnp.max(s_w, axis=-1, keepdims=True), jnp.max(s_c, axis=-1, keepdims=True))
        v_w = jnp.concatenate([v_ref[pl.ds(start, nwin), cols], ones_w], axis=1)
        v_c = jnp.concatenate([v_ref[0:CTX_LEN, cols], ones_c], axis=1)
        pv = (jnp.dot(jnp.exp2(s_w - m).astype(BF16), v_w, preferred_element_type=F32)
              + jnp.dot(jnp.exp2(s_c - m).astype(BF16), v_c, preferred_element_type=F32))
        o_ref[:, cols] = (pv[:, :LANES] / pv[:, LANES:]).astype(o_ref.dtype)


def _na_bias_tables(rpb, n_rows):
    rb, kr, w = NA_ROW_BLOCK, NA_KEY_ROWS, GRID_W
    kh = min(NA_KH, n_rows)
    nblk = n_rows // rb
    tables = []
    for blk in range(nblk):
        base = int(np.clip(rb * blk - NA_KH // 2, 0, n_rows - kr))
        r = rb * blk + np.arange(rb)
        r0 = np.clip(r - kh // 2, 0, n_rows - kh)
        krow = base + np.arange(kr)
        row_ok = (krow[None, :] >= r0[:, None]) & (krow[None, :] < r0[:, None] + kh)
        row_idx = np.clip(krow[None, :] - r[:, None] + NA_KH - 1, 0, 2 * NA_KH - 2)
        qc = np.arange(w)
        c0 = np.clip(qc - NA_KW // 2, 0, w - NA_KW)
        kc = np.arange(w)
        col_ok = (kc[None, :] >= c0[:, None]) & (kc[None, :] < c0[:, None] + NA_KW)
        col_idx = np.clip(kc[None, :] - qc[:, None], 1 - NA_KW, NA_KW - 1) + NA_KW - 1
        ok = row_ok[:, None, :, None] & col_ok[None, :, None, :]
        ri = np.broadcast_to(row_idx[:, None, :, None], ok.shape)
        ci = np.broadcast_to(col_idx[None, :, None, :], ok.shape)
        tables.append((base, ok.reshape(rb * w, kr * w), ri.reshape(rb * w, kr * w), ci.reshape(rb * w, kr * w)))
    for blk in range(2, nblk - 1):
        assert tables[blk][0] - rb * blk == tables[1][0] - rb and all(
            np.array_equal(tables[blk][i], tables[1][i]) for i in (1, 2, 3))
    out = []
    for blk in (0, 1, nblk - 1):
        _, ok, ri, ci = tables[blk]
        out.append(jnp.where(ok[None], rpb.astype(F32)[:, ri, ci] * LOG2E, NEG_INF))
    return jnp.stack(out, axis=0)


def neighbourhood_attention(qkv, rpb, n_heads):
    b, s, _ = qkv.shape
    n_rows = (s - CTX_LEN) // GRID_W
    nblk = n_rows // NA_ROW_BLOCK
    tq = NA_ROW_BLOCK * GRID_W
    assert tq == CTX_LEN
    nwin = NA_KEY_ROWS * GRID_W
    bias = _na_bias_tables(rpb, n_rows)
    scale = LANES ** -0.5 * LOG2E

    hps = _pick(n_heads, (NA_HEADS_PER_STEP, 1))
    ng = n_heads // hps
    wide = hps * LANES

    def bias_case(blk):
        return jnp.where(blk == 0, 0, jnp.where(blk == nblk - 1, 2, 1))

    return pl.pallas_call(
        functools.partial(_na_kernel, scale=scale, n_rows=n_rows),
        out_shape=jax.ShapeDtypeStruct((b, s, n_heads * LANES), BF16),
        grid=(b, ng, nblk),
        in_specs=[pl.BlockSpec((None, tq, wide), lambda bi, h, i: (bi, i + 1, h)),
                  pl.BlockSpec((None, s, wide), lambda bi, h, i: (bi, 0, ng + h)),
                  pl.BlockSpec((None, s, wide), lambda bi, h, i: (bi, 0, 2 * ng + h)),
                  pl.BlockSpec((None, hps, tq, nwin), lambda bi, h, i: (bias_case(i), h, 0, 0))],
        out_specs=pl.BlockSpec((None, tq, wide), lambda bi, h, i: (bi, i + 1, h)),
        compiler_params=_params(("parallel", "parallel", "arbitrary")),
        name="neighbourhood_attention",
    )(qkv, qkv, qkv, bias)


def _gqa_kernel(q_ref, k_ref, v_ref, o_ref, m_sc, acc_sc, *, tk, group):
    m_sc[...] = jnp.full(m_sc.shape, -jnp.inf, F32)
    acc_sc[...] = jnp.zeros(acc_sc.shape, F32)
    ones = jnp.ones((tk, LANES), BF16)

    def body(c, carry):
        rows = pl.ds(pl.multiple_of(c * tk, tk), tk)
        kc = k_ref[rows, :]
        vc = jnp.concatenate([v_ref[rows, :], ones], axis=1)
        for g in range(group):
            s = _dot_nt(q_ref[:, g * LANES:(g + 1) * LANES], kc)
            m_prev = m_sc[g]
            m_new = jnp.maximum(m_prev, jnp.max(s, axis=-1, keepdims=True))
            alpha = jnp.exp2(m_prev - m_new)
            p = jnp.exp2(s - m_new[:, :1])
            pv = jnp.dot(p.astype(BF16), vc, preferred_element_type=F32)
            acc_sc[g] = jnp.concatenate([alpha, alpha], axis=1) * acc_sc[g] + pv
            m_sc[g] = m_new
        return carry

    lax.fori_loop(0, k_ref.shape[0] // tk, body, 0, unroll=True)
    for g in range(group):
        a = acc_sc[g]
        o_ref[:, g * LANES:(g + 1) * LANES] = (a[:, :LANES] / a[:, LANES:]).astype(o_ref.dtype)


def global_gqa_attention(qp, kp, qkv, n_heads, n_kv):
    b, s, _ = qp.shape
    group = n_heads // n_kv
    tq = CTX_LEN
    nq = s // tq - 1
    tk = _pick(s, (768, 512, 256))
    v_c0 = n_heads + n_kv
    return pl.pallas_call(
        functools.partial(_gqa_kernel, tk=tk, group=group),
        out_shape=jax.ShapeDtypeStruct((b, s, n_heads * LANES), BF16),
        grid=(b, n_kv, nq),
        in_specs=[pl.BlockSpec((None, tq, group * LANES), lambda bi, h, i: (bi, i + 1, h)),
                  pl.BlockSpec((None, s, LANES), lambda bi, h, i: (bi, 0, h)),
                  pl.BlockSpec((None, s, LANES), lambda bi, h, i: (bi, 0, v_c0 + h))],
        out_specs=pl.BlockSpec((None, tq, group * LANES), lambda bi, h, i: (bi, i + 1, h)),
        scratch_shapes=[pltpu.VMEM((group, tq, LANES), F32), pltpu.VMEM((group, tq, 2 * LANES), F32)],
        compiler_params=_params(("parallel", "parallel", "arbitrary")),
        name="global_gqa_attention",
    )(qp, kp, qkv)


def _swa_kernel(q_ref, k_ref, v_ref, sink_ref, o_ref, *, head_dim, n_lat):
    i = pl.program_id(2)
    tq = q_ref.shape[0]
    n_ctx_blocks = CTX_LEN // SWA_BLOCK
    band = SWA_BLOCK + 2 * SWA_WINDOW

    @pl.when(i < n_ctx_blocks)
    def _():
        o_ref[...] = jnp.zeros(o_ref.shape, o_ref.dtype)

    @pl.when(i >= n_ctx_blocks)
    def _():
        qb = i - n_ctx_blocks
        row0 = jnp.clip(CTX_LEN + SWA_BLOCK * qb - SWA_WINDOW, 0, k_ref.shape[0] - band)
        row0 = pl.multiple_of(row0, SWA_BLOCK)
        kb, kc = k_ref[pl.ds(row0, band), :], k_ref[0:CTX_LEN, :]
        vb = jnp.concatenate([v_ref[pl.ds(row0, band), :], jnp.ones((band, LANES), BF16)], axis=1)
        vc = jnp.concatenate([v_ref[0:CTX_LEN, :], jnp.ones((CTX_LEN, LANES), BF16)], axis=1)
        qpos = SWA_BLOCK * qb + lax.broadcasted_iota(jnp.int32, (tq, band), 0)
        kpos = row0 - CTX_LEN + lax.broadcasted_iota(jnp.int32, (tq, band), 1)
        ok = (kpos >= 0) & (kpos < n_lat) & (jnp.abs(kpos - qpos) <= SWA_WINDOW)
        hpg = LANES // head_dim
        lane = lax.broadcasted_iota(jnp.int32, (tq, LANES), 1)
        sels = [(lane >= j * head_dim) & (lane < (j + 1) * head_dim) for j in range(hpg)]
        q_all = q_ref[...]
        n_groups = q_all.shape[1] // LANES
        gpc = min(SWA_GROUPS_PER_CHAIN, n_groups)
        ok = jnp.concatenate([ok] * (hpg * gpc), axis=0)
        for c0 in range(0, n_groups, gpc):
            heads = [(c, j) for c in range(c0, c0 + gpc) for j in range(hpg)]
            qs = jnp.concatenate([jnp.where(sels[j], q_all[:, c * LANES:(c + 1) * LANES], jnp.zeros((tq, LANES), BF16))
                                  for c, j in heads], axis=0)
            sink = jnp.concatenate([jnp.broadcast_to(sink_ref[:, c * hpg + j:c * hpg + j + 1], (tq, 1))
                                    for c, j in heads], axis=0)
            s_b = jnp.where(ok, _dot_nt(qs, kb), NEG_INF)
            s_c = _dot_nt(qs, kc)
            m = jnp.maximum(jnp.maximum(jnp.max(s_b, axis=-1, keepdims=True),
                                        jnp.max(s_c, axis=-1, keepdims=True)), sink)
            pv = (jnp.dot(jnp.exp2(s_b - m).astype(BF16), vb, preferred_element_type=F32)
                  + jnp.dot(jnp.exp2(s_c - m).astype(BF16), vc, preferred_element_type=F32))
            o = pv[:, :LANES] / (pv[:, LANES:] + jnp.exp2(sink - m))
            for ci, c in enumerate(range(c0, c0 + gpc)):
                res = o[ci * hpg * tq:(ci * hpg + 1) * tq]
                for j in range(1, hpg):
                    res = jnp.where(sels[j], o[(ci * hpg + j) * tq:(ci * hpg + j + 1) * tq], res)
                o_ref[:, c * LANES:(c + 1) * LANES] = res.astype(o_ref.dtype)


def window_gqa_attention(q, k2, v2, sink, n_heads, n_kv, head_dim):
    b, s, _ = q.shape
    group = n_heads // n_kv
    gcols = group * head_dim
    assert gcols % LANES == 0 and LANES % head_dim == 0
    tq = SWA_BLOCK
    return pl.pallas_call(
        functools.partial(_swa_kernel, head_dim=head_dim, n_lat=s - CTX_LEN),
        out_shape=jax.ShapeDtypeStruct((b, s, n_heads * head_dim), BF16),
        grid=(b, n_kv, s // tq),
        in_specs=[pl.BlockSpec((None, tq, gcols), lambda bi, h, i: (bi, i, h)),
                  pl.BlockSpec((None, s, LANES), lambda bi, h, i: (bi, 0, h)),
                  pl.BlockSpec((None, s, LANES), lambda bi, h, i: (bi, 0, h)),
                  pl.BlockSpec((None, 1, group), lambda bi, h, i: (h, 0, 0))],
        out_specs=pl.BlockSpec((None, tq, gcols), lambda bi, h, i: (bi, i, h)),
        compiler_params=_params(("parallel", "parallel", "arbitrary")),
        name="window_gqa_attention",
    )(q, k2, v2, sink.astype(F32).reshape(n_kv, 1, group))


def _retention_kernel(qf_ref, kf_ref, vf_ref, cosf_ref, sinf_ref, qb_ref, kb_ref, vb_ref, cosb_ref, sinb_ref,
                      intra_ref, qdec_ref, kdec_ref, cdec_ref, of_ref, ob_ref, state_sc, *, k_scale, heads, dh):
    @pl.when(pl.program_id(2) == 0)
    def _():
        state_sc[...] = jnp.zeros(state_sc.shape, F32)

    half = dh // 2

    def rope(x, cos, sin):
        swapped = jnp.concatenate([x[:, half:], x[:, :half]], axis=1)
        return x * cos + swapped * sin

    dirs = ((qf_ref, kf_ref, vf_ref, cosf_ref, sinf_ref, of_ref), (qb_ref, kb_ref, vb_ref, cosb_ref, sinb_ref, ob_ref))
    for dr, (q_ref, k_ref, v_ref, cos_ref, sin_ref, o_ref) in enumerate(dirs):
        cos, sin = cos_ref[...], sin_ref[...]
        for hh in range(heads):
            cols = slice(hh * dh, (hh + 1) * dh)
            q = rope(q_ref[:, cols].astype(F32), cos, sin)
            k = rope(k_ref[:, cols].astype(F32), cos, sin) * k_scale
            v = v_ref[:, cols]
            state = state_sc[dr, hh]
            att = _dot_nt(q.astype(BF16), k.astype(BF16)) * intra_ref[dr, hh]
            y = (jnp.dot(att.astype(BF16), v, preferred_element_type=F32)
                 + jnp.dot((q * qdec_ref[dr, hh]).astype(BF16), state.astype(BF16), preferred_element_type=F32))
            kd = (k * kdec_ref[dr, hh]).astype(BF16)
            state_sc[dr, hh] = state * cdec_ref[dr, hh] + lax.dot_general(kd, v, (((0,), (0,)), ((), ())),
                                                                         preferred_element_type=F32)
            y = y * lax.rsqrt(jnp.mean(y * y, axis=-1, keepdims=True) + NORM_EPS)
            o_ref[:, cols] = y.astype(o_ref.dtype)


def retention_scan(proj, decay_exp, cos, sin, n_heads):
    b, s, five_d = proj.shape
    d = five_d // 5
    dh = d // n_heads
    c = RET_CHUNK
    n_chunks = s // c
    n_ctx_chunks = CTX_LEN // c
    hps = min(RET_HEADS_PER_STEP, n_heads)
    lg = jnp.log1p(-jnp.exp2(-decay_exp.astype(F32)))
    pos = jnp.arange(c, dtype=F32)
    diff = pos[:, None] - pos[None, :]
    lgf, lgb = lg[0][:, None, None], lg[1][:, None, None]
    intra_f = jnp.where(diff >= 0, jnp.exp(lgf * jnp.maximum(diff, 0.0)), 0.0)
    intra_b = jnp.where(diff <= 0, jnp.exp(lgb * jnp.maximum(-diff, 0.0)), 0.0)
    intra = jnp.stack([intra_f, intra_b])
    qdec = jnp.stack([jnp.exp(lg[0][:, None] * (pos + 1.0)), jnp.exp(lg[1][:, None] * (c - pos))])[..., None]
    kdec = jnp.stack([jnp.exp(lg[0][:, None] * (c - 1.0 - pos)), jnp.exp(lg[1][:, None] * pos)])[..., None]
    cdec = jnp.exp(lg * c)[..., None, None]
    ncb = n_heads // hps

    def bwd_chunk(t):
        return jnp.where(t < n_ctx_chunks, n_ctx_chunks - 1 - t, n_chunks - 1 - (t - n_ctx_chunks))

    fwd = lambda off: (lambda bi, h, t: (bi, t, off * ncb + h))
    bwd = lambda off: (lambda bi, h, t: (bi, bwd_chunk(t), off * ncb + h))
    tab = lambda bi, h, t: (0, h, 0, 0)
    blk = (None, c, hps * dh)
    return pl.pallas_call(
        functools.partial(_retention_kernel, k_scale=dh ** -0.5, heads=hps, dh=dh),
        out_shape=(jax.ShapeDtypeStruct((b, s, d), BF16), jax.ShapeDtypeStruct((b, s, d), BF16)),
        grid=(b, ncb, n_chunks),
        in_specs=[pl.BlockSpec(blk, fwd(0)), pl.BlockSpec(blk, fwd(1)), pl.BlockSpec(blk, fwd(2)),
                  pl.BlockSpec((c, dh), lambda bi, h, t: (t, 0)), pl.BlockSpec((c, dh), lambda bi, h, t: (t, 0)),
                  pl.BlockSpec(blk, bwd(0)), pl.BlockSpec(blk, bwd(1)), pl.BlockSpec(blk, bwd(2)),
                  pl.BlockSpec((c, dh), lambda bi, h, t: (bwd_chunk(t), 0)),
                  pl.BlockSpec((c, dh), lambda bi, h, t: (bwd_chunk(t), 0)),
                  pl.BlockSpec((2, hps, c, c), tab),
                  pl.BlockSpec((2, hps, c, 1), tab),
                  pl.BlockSpec((2, hps, c, 1), tab),
                  pl.BlockSpec((2, hps, 1, 1), tab)],
        out_specs=(pl.BlockSpec(blk, lambda bi, h, t: (bi, t, h)),
                   pl.BlockSpec(blk, lambda bi, h, t: (bi, bwd_chunk(t), h))),
        scratch_shapes=[pltpu.VMEM((2, hps, dh, dh), F32)],
        compiler_params=_params(("parallel", "parallel", "arbitrary")),
        name="retention_scan",
    )(proj, proj, proj, cos, sin, proj, proj, proj, cos, sin, intra, qdec, kdec, cdec)


def _ret_merge_kernel(yf_ref, yb_ref, gf_ref, gb_ref, o_ref):
    gf = gf_ref[...].astype(F32)
    gb = gb_ref[...].astype(F32)
    o = gf * jax.nn.sigmoid(gf) * yf_ref[...].astype(F32) + gb * jax.nn.sigmoid(gb) * yb_ref[...].astype(F32)
    o_ref[...] = o.astype(o_ref.dtype)


def retention_merge(y_f, y_b, proj):
    b, s, d = y_f.shape
    tr = _pick(s, (1408, 1024, 768, 512, 256))
    bc = _pick(d, (1024, 512, 256, 128))
    nc = d // bc
    return pl.pallas_call(
        _ret_merge_kernel,
        out_shape=jax.ShapeDtypeStruct((b, s, d), BF16),
        grid=(b, s // tr, nc),
        in_specs=[pl.BlockSpec((None, tr, bc), lambda bi, i, j: (bi, i, j)),
                  pl.BlockSpec((None, tr, bc), lambda bi, i, j: (bi, i, j)),
                  pl.BlockSpec((None, tr, bc), lambda bi, i, j: (bi, i, 3 * nc + j)),
                  pl.BlockSpec((None, tr, bc), lambda bi, i, j: (bi, i, 4 * nc + j))],
        out_specs=pl.BlockSpec((None, tr, bc), lambda bi, i, j: (bi, i, j)),
        compiler_params=_params(("parallel", "parallel", "parallel")),
        name="retention_merge",
    )(y_f, y_b, proj, proj)


def _moe_kernel(be_ref, first_ref, nused_ref, x_ref, w1_ref, perm_ref, b1_ref, w2_ref, b2_ref, g_ref, o_ref,
                w1p_sc, w2p_sc):
    i = pl.program_id(0)
    f = w2_ref.shape[0]
    fp = w2p_sc.shape[0]

    @pl.when(i == 0)
    def _():
        w2p_sc[...] = jnp.zeros(w2p_sc.shape, w2p_sc.dtype)

    @pl.when(first_ref[i] == 1)
    def _():
        w1p_sc[...] = jnp.dot(w1_ref[...].astype(BF16), perm_ref[...], preferred_element_type=F32).astype(BF16)
        w2p_sc[0:f, :] = w2_ref[...].astype(BF16)

    @pl.when(i < nused_ref[0])
    def _():
        u = jnp.dot(x_ref[...], w1p_sc[...], preferred_element_type=F32) + b1_ref[...]
        glu = jnp.minimum(u[:, :fp], SWIGLU_LIMIT)
        lin = jnp.clip(u[:, fp:], -SWIGLU_LIMIT, SWIGLU_LIMIT)
        a = glu * jax.nn.sigmoid(SWIGLU_ALPHA * glu) * (lin + 1.0)
        y = jnp.dot(a.astype(BF16), w2p_sc[...], preferred_element_type=F32) + b2_ref[...]
        o_ref[...] = (y * g_ref[...]).astype(o_ref.dtype)

    @pl.when(i >= nused_ref[0])
    def _():
        o_ref[...] = jnp.zeros(o_ref.shape, o_ref.dtype)


def moe_experts(xg, blk_expert, blk_first, n_used, slot_gate, w1, b1, w2, b2):
    n_slots, d = xg.shape
    n_e, _, f2 = w1.shape
    f = f2 // 2
    fp = -(-f // LANES) * LANES
    n_blocks = n_slots // MOE_BLOCK
    perm = np.zeros((f2, 2 * fp), np.float32)
    perm[2 * np.arange(f), np.arange(f)] = 1.0
    perm[2 * np.arange(f) + 1, fp + np.arange(f)] = 1.0
    b1p = jnp.zeros((n_e, 1, 2 * fp), F32).at[:, 0, :f].set(b1[:, 0::2]).at[:, 0, fp:fp + f].set(b1[:, 1::2])

    def xrow(i, be, first, nused):
        return (jnp.minimum(i, jnp.maximum(nused[0] - 1, 0)), 0)

    def expert(i, be, first, nused):
        return (be[i], 0, 0)

    grid_spec = pltpu.PrefetchScalarGridSpec(
        num_scalar_prefetch=3,
        grid=(n_blocks,),
        in_specs=[pl.BlockSpec((MOE_BLOCK, d), xrow),
                  pl.BlockSpec((None, d, f2), expert),
                  pl.BlockSpec((f2, 2 * fp), lambda i, be, first, nused: (0, 0)),
                  pl.BlockSpec((None, 1, 2 * fp), expert),
                  pl.BlockSpec((None, f, d), expert),
                  pl.BlockSpec((None, 1, d), expert),
                  pl.BlockSpec((MOE_BLOCK, 1), xrow)],
        out_specs=pl.BlockSpec((MOE_BLOCK, d), lambda i, be, first, nused: (i, 0)),
        scratch_shapes=[pltpu.VMEM((d, 2 * fp), BF16), pltpu.VMEM((fp, d), BF16)],
    )
    return pl.pallas_call(
        _moe_kernel,
        out_shape=jax.ShapeDtypeStruct((n_slots, d), BF16),
        grid_spec=grid_spec,
        compiler_params=_params(("arbitrary",)),
        name="moe_experts",
    )(blk_expert, blk_first, n_used, xg, w1, jnp.asarray(perm, BF16), b1p, w2, b2.astype(F32).reshape(n_e, 1, d),
      slot_gate)


def _moe_combine_kernel(x_ref, g_ref, y0_ref, y1_ref, y2_ref, y3_ref, o_ref):
    y = (y0_ref[...].astype(F32) + y1_ref[...].astype(F32)) + (y2_ref[...].astype(F32) + y3_ref[...].astype(F32))
    o_ref[...] = x_ref[...] + g_ref[0] * y


def moe_combine_residual(xs2, gate_groups, ysg):
    t_tok, d = xs2.shape
    assert TOP_K == 4
    tr = CTX_LEN
    bn = _pick(d, (2048, 1024, 512, 256, 128))
    nc = d // bn
    yspec = [pl.BlockSpec((tr, bn), functools.partial(lambda i, j, k: (i, k * nc + j), k=k)) for k in range(TOP_K)]
    return pl.pallas_call(
        _moe_combine_kernel,
        out_shape=jax.ShapeDtypeStruct((t_tok, d), F32),
        grid=(t_tok // tr, nc),
        in_specs=[pl.BlockSpec((tr, bn), lambda i, j: (i, j)),
                  pl.BlockSpec((1, 1, bn), lambda i, j: (i, 0, j))] + yspec,
        out_specs=pl.BlockSpec((tr, bn), lambda i, j: (i, j)),
        input_output_aliases={0: 0},
        compiler_params=_params(("parallel", "parallel")),
        name="moe_combine_residual",
    )(xs2, gate_groups, ysg, ysg, ysg, ysg)


def moe_layer(xs2, gate_groups, h, top_idx, top_gate, w1, b1, w2, b2):
    t_tok, d = h.shape
    n_e = w1.shape[0]
    n_assign = t_tok * TOP_K
    e_flat = top_idx.reshape(n_assign)
    onehot = (e_flat[:, None] == jnp.arange(n_e, dtype=jnp.int32)[None, :]).astype(jnp.int32)
    csum = jnp.cumsum(onehot, axis=0)
    counts = csum[-1]
    rank = jnp.sum(csum * onehot, axis=1) - 1
    padded = (counts + MOE_BLOCK - 1) // MOE_BLOCK * MOE_BLOCK
    pad_end = jnp.cumsum(padded)
    pad_start = pad_end - padded
    start = jnp.cumsum(counts) - counts
    slot_of = pad_start[e_flat] + rank
    n_blocks = -(-n_assign // MOE_BLOCK) + n_e
    n_slots = n_blocks * MOE_BLOCK
    blk_start = jnp.arange(n_blocks, dtype=jnp.int32) * MOE_BLOCK
    blk_expert = jnp.minimum(jnp.sum(pad_end[None, :] <= blk_start[:, None], axis=1), n_e - 1).astype(jnp.int32)
    n_used = (pad_end[-1] // MOE_BLOCK).astype(jnp.int32)
    prev = jnp.concatenate([jnp.full((1,), -1, jnp.int32), blk_expert[:-1]])
    blk_first = ((blk_expert != prev) & (jnp.arange(n_blocks) < n_used)).astype(jnp.int32)
    _, order = lax.sort_key_val(e_flat, jnp.arange(n_assign, dtype=jnp.int32))
    slot_e = jnp.repeat(blk_expert, MOE_BLOCK)
    r = jnp.arange(n_slots, dtype=jnp.int32) - pad_start[slot_e]
    valid = r < counts[slot_e]
    assign = order[jnp.clip(start[slot_e] + r, 0, n_assign - 1)]
    slot_tok = jnp.where(valid, assign // TOP_K, 0)
    slot_gate = jnp.where(valid, top_gate.reshape(n_assign)[assign], 0.0)
    xg = jnp.take(h, slot_tok, axis=0)
    ys = moe_experts(xg, blk_expert, blk_first, n_used.reshape(1), slot_gate.reshape(n_slots, 1), w1, b1, w2, b2)
    ysg = jnp.take(ys, slot_of, axis=0).reshape(t_tok, TOP_K * d)
    return moe_combine_residual(xs2, gate_groups, ysg)


def _group_rows(vec_ctx, vec_lat, s):
    b, d = vec_lat.shape
    ng = s // CTX_LEN
    g = jnp.concatenate([jnp.broadcast_to(vec_ctx[None, None, :], (b, 1, d)),
                         jnp.broadcast_to(vec_lat[:, None, :], (b, ng - 1, d))], axis=1)
    return g.reshape(b * ng, 1, d)


def kernel(x, c, ctx, c_ctx, w_mod, b_mod, g_mix, g_ffn, g_final, na_w_qkv, na_w_o, na_rpb, ret_w_in, ret_w_o, ret_decay_exp, gqa_w_qkv, gqa_w_o, gqa_q_gain, gqa_k_gain, swa_w_qkv, swa_w_o, swa_sink, moe_w_router, moe_b_router, moe_w1, moe_b1, moe_w2, moe_b2):
    b, l, d = x.shape
    assert ctx.shape[1] == CTX_LEN and l % CTX_LEN == 0
    s = CTX_LEN + l
    depth = w_mod.shape[0]
    xs = jnp.concatenate([ctx, x], axis=1)

    n_rows = -(-(b + 1) // 16) * 16
    s_rows = jnp.zeros((n_rows, d), F32).at[:b].set(jax.nn.silu(c)).at[b].set(jax.nn.silu(c_ctx))
    mod = modulation_all_layers(s_rows, w_mod, b_mod).reshape(depth, n_rows, N_MOD, d)

    ones = jnp.ones((1, LANES), F32)
    for li in range(depth):
        mix, j = li % N_MIXERS, li // N_MIXERS
        need_ctx = li < depth - 1
        mod_lat, mod_ctx = mod[li, :b], mod[li, b]
        modv = jnp.stack([jnp.broadcast_to(mod_ctx[None], (b, N_MOD, d)), mod_lat], axis=1)
        h = norm_modulate(xs, g_mix[li], modv, 0)
        h2 = h.reshape(b * s, d)
        if mix == 0:
            qkv = matmul(h2, na_w_qkv[j]).reshape(b, s, 3 * d)
            o = neighbourhood_attention(qkv, na_rpb[j], NA_HEADS)
            o = ctx_self_attention(o, qkv, 0, qkv, NA_HEADS, qkv, 2 * NA_HEADS, NA_HEADS, 1, LANES ** -0.5 * LOG2E)
            w_o = na_w_o[j]
        elif mix == 1:
            proj = matmul(h2, ret_w_in[j]).reshape(b, s, 5 * d)
            cos, sin = rope_tables(l, d // RET_HEADS)
            y_f, y_b = retention_scan(proj, ret_decay_exp[j], cos, sin, RET_HEADS)
            o = retention_merge(y_f, y_b, proj)
            w_o = ret_w_o[j]
        elif mix == 2:
            dh = d // GQA_HEADS
            assert dh == LANES
            qkv = matmul(h2, gqa_w_qkv[j]).reshape(b, s, -1)
            cos, sin = rope_tables(l, dh)
            qg = (gqa_q_gain[j].astype(F32) * (dh ** -0.5 * LOG2E)).reshape(1, LANES)
            kg = gqa_k_gain[j].astype(F32).reshape(1, LANES)
            qp = qk_prepare(qkv, 0, GQA_HEADS * dh, qg, cos, sin, dh, True)
            kp = qk_prepare(qkv, GQA_HEADS * dh, GQA_KV_HEADS * dh, kg, cos, sin, dh, True)
            o = global_gqa_attention(qp, kp, qkv, GQA_HEADS, GQA_KV_HEADS)
            o = ctx_self_attention(o, qp, 0, kp, 0, qkv, GQA_HEADS + GQA_KV_HEADS, GQA_HEADS,
                                   GQA_HEADS // GQA_KV_HEADS, 1.0)
            w_o = gqa_w_o[j]
        else:
            dh = d // SWA_HEADS
            qkv = matmul(h2, swa_w_qkv[j]).reshape(b, s, -1)
            cos, sin = rope_tables(l, dh)
            reps = LANES // dh
            cos, sin = jnp.tile(cos, (1, reps)), jnp.tile(sin, (1, reps))
            unit = jnp.zeros((s, LANES), F32)
            qp = qk_prepare(qkv, 0, SWA_HEADS * dh, ones * (dh ** -0.5 * LOG2E), cos, sin, dh, False)
            k2 = qk_prepare(qkv, SWA_HEADS * dh, SWA_KV_HEADS * dh, ones, cos, sin, dh, False, dup=True)
            v2 = qk_prepare(qkv, (SWA_HEADS + SWA_KV_HEADS) * dh, SWA_KV_HEADS * dh, ones,
                            jnp.ones((s, LANES), F32), unit, dh, False, dup=True)
            o = window_gqa_attention(qp, k2, v2, swa_sink[j].astype(F32) * LOG2E, SWA_HEADS, SWA_KV_HEADS, dh)
            w_o = swa_w_o[j]
        if mix in (0, 2) and not need_ctx:
            o = o.at[:, :CTX_LEN].set(0)
        xs = matmul_residual(o.reshape(b * s, d), w_o, xs.reshape(b * s, d),
                             _group_rows(mod_ctx[2], mod_lat[:, 2], s)).reshape(b, s, d)

        h, top_idx, top_gate = norm_modulate_route(xs, g_ffn[li], modv, 3, moe_w_router[li], moe_b_router[li])
        xs = moe_layer(xs.reshape(b * s, d), _group_rows(mod_ctx[5], mod_lat[:, 5], s), h.reshape(b * s, d),
                       top_idx.reshape(b * s, LANES)[:, :TOP_K], top_gate.reshape(b * s, LANES)[:, :TOP_K],
                       moe_w1[li], moe_b1[li], moe_w2[li], moe_b2[li]).reshape(b, s, d)
    return final_norm(xs, g_final)
```

```python
import functools

import jax
import jax.numpy as jnp
import numpy as np
from jax import lax
from jax.experimental import pallas as pl
from jax.experimental.pallas import tpu as pltpu

DEPTH = 4
GRID_W = 64
CTX_LEN = 256
N_MIXERS = 4
N_MOD = 6
NORM_EPS = 1e-6
NEG_INF = -1e30
ROPE_THETA = 10000.0

NA_HEADS = 32
NA_KH = 8
NA_KW = 16
NA_ROW_BLOCK = 4
NA_KEY_ROWS = 12
NA_HEADS_PER_STEP = 4

RET_HEADS = 16
RET_CHUNK = 128
RET_HEADS_PER_STEP = 4

GQA_HEADS = 32
GQA_KV_HEADS = 8

SWA_HEADS = 64
SWA_KV_HEADS = 8
SWA_WINDOW = 128
SWA_BLOCK = 128
SWA_GROUPS_PER_CHAIN = 2

N_EXPERTS = 32
TOP_K = 4
SWIGLU_ALPHA = 1.702
SWIGLU_LIMIT = 7.0
MOE_BLOCK = 256
MOE_TOKEN_PARTS = 2

LOG2E = 1.4426950408889634
LANES = 128
SUBLANES = 8
VMEM_LIMIT_BYTES = 56 * 1024 * 1024

F32 = jnp.float32
BF16 = jnp.bfloat16


def _params(semantics, vmem=VMEM_LIMIT_BYTES):
    return pltpu.CompilerParams(dimension_semantics=semantics, vmem_limit_bytes=vmem)


def _pick(n, candidates):
    for c in candidates:
        if n % c == 0:
            return c
    return n


def _mod_kernel(s_ref, w_ref, b_ref, o_ref):
    acc = jnp.dot(s_ref[...], w_ref[...].astype(BF16), preferred_element_type=F32)
    o_ref[...] = acc + b_ref[...]


def modulation_all_layers(s_rows, w_mod, b_mod):
    depth, d, n = w_mod.shape
    rows = s_rows.shape[0]
    bn = _pick(n, (512, 256, 128))
    return pl.pallas_call(
        _mod_kernel,
        out_shape=jax.ShapeDtypeStruct((depth, rows, n), F32),
        grid=(depth, n // bn),
        in_specs=[
            pl.BlockSpec((rows, d), lambda l, j: (0, 0)),
            pl.BlockSpec((None, d, bn), lambda l, j: (l, 0, j)),
            pl.BlockSpec((None, 1, bn), lambda l, j: (l, 0, j)),
        ],
        out_specs=pl.BlockSpec((None, rows, bn), lambda l, j: (l, 0, j)),
        compiler_params=_params(("parallel", "parallel")),
        name="modulation",
    )(s_rows.astype(BF16), w_mod, b_mod.reshape(depth, 1, n))


def _rms_mod(x, g, shift, scale):
    y = x * lax.rsqrt(jnp.mean(x * x, axis=-1, keepdims=True) + NORM_EPS) * g
    return y * (1.0 + scale) + shift


def _norm_mod_kernel(x_ref, g_ref, m_ref, o_ref, *, shift_idx):
    h = _rms_mod(x_ref[...], g_ref[...], m_ref[shift_idx:shift_idx + 1, :], m_ref[shift_idx + 1:shift_idx + 2, :])
    o_ref[...] = h.astype(o_ref.dtype)


def _norm_route_kernel(x_ref, g_ref, m_ref, wh_ref, wl_ref, br_ref, o_ref, idx_ref, gate_ref, *, shift_idx):
    h = _rms_mod(x_ref[...], g_ref[...], m_ref[shift_idx:shift_idx + 1, :], m_ref[shift_idx + 1:shift_idx + 2, :])
    h_hi = h.astype(BF16)
    o_ref[...] = h_hi.astype(o_ref.dtype)
    h_lo = (h - h_hi.astype(F32)).astype(BF16)
    logits = (jnp.dot(h_hi, wh_ref[...], preferred_element_type=F32)
              + (jnp.dot(h_lo, wh_ref[...], preferred_element_type=F32)
                 + jnp.dot(h_hi, wl_ref[...], preferred_element_type=F32))) + br_ref[...]
    rows, n_e = logits.shape
    e_iota = lax.broadcasted_iota(jnp.int32, (rows, n_e), 1).astype(F32)
    lane = lax.broadcasted_iota(jnp.int32, (rows, LANES), 1)
    idx_out = jnp.zeros((rows, LANES), F32)
    val_out = jnp.zeros((rows, LANES), F32)
    top0 = None
    denom = jnp.zeros((rows, 1), F32)
    work = logits
    for k in range(TOP_K):
        mx = jnp.max(work, axis=-1, keepdims=True)
        ix = jnp.min(jnp.where(work == mx, e_iota, float(n_e)), axis=-1, keepdims=True)
        if k == 0:
            top0 = mx
        ex = jnp.exp(mx - top0)
        denom = denom + ex
        idx_out = jnp.where(lane == k, ix, idx_out)
        val_out = jnp.where(lane == k, ex, val_out)
        work = jnp.where(e_iota == ix, -jnp.inf, work)
    idx_ref[...] = idx_out.T[0:SUBLANES, :].astype(jnp.int32)
    gate_ref[...] = (val_out / denom).T[0:SUBLANES, :]


def norm_modulate(xs, g, modv, shift_idx):
    b, s, d = xs.shape
    tr = CTX_LEN
    nblk = s // tr
    return pl.pallas_call(
        functools.partial(_norm_mod_kernel, shift_idx=shift_idx),
        out_shape=jax.ShapeDtypeStruct((b, s, d), BF16),
        grid=(b, nblk),
        in_specs=[
            pl.BlockSpec((None, tr, d), lambda bi, i: (bi, i, 0)),
            pl.BlockSpec((1, d), lambda bi, i: (0, 0)),
            pl.BlockSpec((None, None, N_MOD, d), lambda bi, i: (bi, jnp.minimum(i, 1), 0, 0)),
        ],
        out_specs=pl.BlockSpec((None, tr, d), lambda bi, i: (bi, i, 0)),
        compiler_params=_params(("parallel", "parallel")),
        name="norm_modulate",
    )(xs, g.reshape(1, d), modv)


def norm_modulate_route(xs, g, modv, shift_idx, w_router, b_router):
    b, s, d = xs.shape
    tr = CTX_LEN
    nblk = s // tr
    n_e = w_router.shape[1]
    w_hi = w_router.astype(BF16)
    w_lo = (w_router.astype(F32) - w_hi.astype(F32)).astype(BF16)
    return pl.pallas_call(
        functools.partial(_norm_route_kernel, shift_idx=shift_idx),
        out_shape=(jax.ShapeDtypeStruct((b, s, d), BF16),
                   jax.ShapeDtypeStruct((b, SUBLANES, s), jnp.int32),
                   jax.ShapeDtypeStruct((b, SUBLANES, s), F32)),
        grid=(b, nblk),
        in_specs=[
            pl.BlockSpec((None, tr, d), lambda bi, i: (bi, i, 0)),
            pl.BlockSpec((1, d), lambda bi, i: (0, 0)),
            pl.BlockSpec((None, None, N_MOD, d), lambda bi, i: (bi, jnp.minimum(i, 1), 0, 0)),
            pl.BlockSpec((d, n_e), lambda bi, i: (0, 0)),
            pl.BlockSpec((d, n_e), lambda bi, i: (0, 0)),
            pl.BlockSpec((1, n_e), lambda bi, i: (0, 0)),
        ],
        out_specs=(pl.BlockSpec((None, tr, d), lambda bi, i: (bi, i, 0)),
                   pl.BlockSpec((None, SUBLANES, tr), lambda bi, i: (bi, 0, i)),
                   pl.BlockSpec((None, SUBLANES, tr), lambda bi, i: (bi, 0, i))),
        compiler_params=_params(("parallel", "parallel")),
        name="norm_modulate_route",
    )(xs, g.reshape(1, d), modv, w_hi, w_lo, b_router.astype(F32).reshape(1, n_e))


def _final_norm_kernel(x_ref, g_ref, o_ref):
    x = x_ref[...]
    o_ref[...] = x * lax.rsqrt(jnp.mean(x * x, axis=-1, keepdims=True) + NORM_EPS) * g_ref[...]


def final_norm(xs, g):
    b, s, d = xs.shape
    tr = CTX_LEN
    nblk = s // tr - 1
    return pl.pallas_call(
        _final_norm_kernel,
        out_shape=jax.ShapeDtypeStruct((b, s - CTX_LEN, d), F32),
        grid=(b, nblk),
        in_specs=[pl.BlockSpec((None, tr, d), lambda bi, i: (bi, i + 1, 0)),
                  pl.BlockSpec((1, d), lambda bi, i: (0, 0))],
        out_specs=pl.BlockSpec((None, tr, d), lambda bi, i: (bi, i, 0)),
        compiler_params=_params(("parallel", "parallel")),
        name="final_norm",
    )(xs, g.reshape(1, d))


def _mm_kernel(x_ref, w_ref, o_ref, wb_sc):
    @pl.when(pl.program_id(1) == 0)
    def _():
        wb_sc[...] = w_ref[...].astype(BF16)

    o_ref[...] = jnp.dot(x_ref[...], wb_sc[...], preferred_element_type=F32).astype(o_ref.dtype)


def _mm_residual_kernel(x_ref, w_ref, r_ref, g_ref, o_ref, wb_sc, *, groups, group_rows):
    @pl.when(pl.program_id(1) == 0)
    def _():
        wb_sc[...] = w_ref[...].astype(BF16)

    acc = jnp.dot(x_ref[...], wb_sc[...], preferred_element_type=F32)
    for gi in range(groups):
        rows = slice(gi * group_rows, (gi + 1) * group_rows)
        o_ref[rows, :] = r_ref[rows, :] + g_ref[gi] * acc[rows, :]


def matmul(x, w):
    m, k = x.shape
    n = w.shape[1]
    bm = _pick(m, (1536, 768, 512, 256, 128))
    bn = _pick(n, (512, 256, 128))
    return pl.pallas_call(
        _mm_kernel,
        out_shape=jax.ShapeDtypeStruct((m, n), BF16),
        grid=(n // bn, m // bm),
        in_specs=[pl.BlockSpec((bm, k), lambda j, i: (i, 0)),
                  pl.BlockSpec((k, bn), lambda j, i: (0, j))],
        out_specs=pl.BlockSpec((bm, bn), lambda j, i: (i, j)),
        scratch_shapes=[pltpu.VMEM((k, bn), BF16)],
        compiler_params=_params(("parallel", "arbitrary")),
        name="projection",
    )(x, w)


def matmul_residual(x, w, res, gate_groups):
    m, k = x.shape
    n = w.shape[1]
    bm = _pick(m, (768, 512, 256))
    bn = _pick(n, (512, 256, 128))
    groups = bm // CTX_LEN
    return pl.pallas_call(
        functools.partial(_mm_residual_kernel, groups=groups, group_rows=CTX_LEN),
        out_shape=jax.ShapeDtypeStruct((m, n), F32),
        grid=(n // bn, m // bm),
        in_specs=[pl.BlockSpec((bm, k), lambda j, i: (i, 0)),
                  pl.BlockSpec((k, bn), lambda j, i: (0, j)),
                  pl.BlockSpec((bm, bn), lambda j, i: (i, j)),
                  pl.BlockSpec((groups, 1, bn), lambda j, i: (i, 0, j))],
        out_specs=pl.BlockSpec((bm, bn), lambda j, i: (i, j)),
        scratch_shapes=[pltpu.VMEM((k, bn), BF16)],
        input_output_aliases={2: 0},
        compiler_params=_params(("parallel", "arbitrary")),
        name="out_projection_residual",
    )(x, w, res, gate_groups)


def rope_tables(seq, head_dim):
    half = head_dim // 2
    n = half // 2
    t = jnp.arange(seq)
    rows, cols = (t // GRID_W).astype(F32), (t % GRID_W).astype(F32)
    freqs = ROPE_THETA ** (-jnp.arange(n, dtype=F32) / n)
    ang = jnp.concatenate([rows[:, None] * freqs, cols[:, None] * freqs], axis=-1)
    cos, sin = jnp.cos(ang), jnp.sin(ang)
    cos = jnp.concatenate([jnp.ones((CTX_LEN, half), F32), cos], axis=0)
    sin = jnp.concatenate([jnp.zeros((CTX_LEN, half), F32), sin], axis=0)
    return jnp.concatenate([cos, cos], axis=-1), jnp.concatenate([-sin, sin], axis=-1)


def _swap_halves(y, head_dim):
    half = head_dim // 2
    if head_dim == LANES:
        return pltpu.roll(y, half, 1)
    lane = lax.broadcasted_iota(jnp.int32, y.shape, 1)
    first = (lane % head_dim) < half
    return jnp.where(first, pltpu.roll(y, LANES - half, 1), pltpu.roll(y, half, 1))


def _qk_prep_kernel(x_ref, gain_ref, cos_ref, sin_ref, o_ref, *, head_dim, normalise, dup):
    x = x_ref[...].astype(F32)
    cols = x.shape[1]
    cos, sin = cos_ref[...], sin_ref[...]
    gain = gain_ref[...]
    for c in range(cols // LANES):
        y = x[:, c * LANES:(c + 1) * LANES]
        if normalise:
            y = y * lax.rsqrt(jnp.mean(y * y, axis=-1, keepdims=True) + NORM_EPS)
        y = y * gain
        y = (y * cos + _swap_halves(y, head_dim) * sin).astype(o_ref.dtype)
        if dup:
            lane = lax.broadcasted_iota(jnp.int32, y.shape, 1)
            other = pltpu.roll(y.astype(F32), head_dim, 1).astype(o_ref.dtype)
            o_ref[:, (2 * c) * LANES:(2 * c + 1) * LANES] = jnp.where(lane < head_dim, y, other)
            o_ref[:, (2 * c + 1) * LANES:(2 * c + 2) * LANES] = jnp.where(lane < head_dim, other, y)
        else:
            o_ref[:, c * LANES:(c + 1) * LANES] = y


def qk_prepare(qkv, col0, ncols, gain, cos, sin, head_dim, normalise, dup=False):
    b, s, _ = qkv.shape
    tr = _pick(s, (1408, 1024, 768, 512, 256))
    bc = _pick(ncols, (512, 256, 128))
    c0 = col0 // bc
    assert col0 % bc == 0
    mult = 2 if dup else 1
    return pl.pallas_call(
        functools.partial(_qk_prep_kernel, head_dim=head_dim, normalise=normalise, dup=dup),
        out_shape=jax.ShapeDtypeStruct((b, s, ncols * mult), BF16),
        grid=(b, s // tr, ncols // bc),
        in_specs=[pl.BlockSpec((None, tr, bc), lambda bi, i, j: (bi, i, c0 + j)),
                  pl.BlockSpec((1, LANES), lambda bi, i, j: (0, 0)),
                  pl.BlockSpec((tr, LANES), lambda bi, i, j: (i, 0)),
                  pl.BlockSpec((tr, LANES), lambda bi, i, j: (i, 0))],
        out_specs=pl.BlockSpec((None, tr, bc * mult), lambda bi, i, j: (bi, i, j)),
        compiler_params=_params(("parallel", "parallel", "parallel")),
        name="qk_prepare",
    )(qkv, gain, cos, sin)


def _dot_nt(a, b):
    return lax.dot_general(a, b, (((1,), (1,)), ((), ())), preferred_element_type=F32)


def _ctx_attn_kernel(q_ref, k_ref, v_ref, buf_ref, o_ref, *, scale):
    del buf_ref
    s = _dot_nt(q_ref[...], k_ref[...]) * scale
    m = jnp.max(s, axis=-1, keepdims=True)
    p = jnp.exp2(s - m)
    l = jnp.sum(p, axis=-1, keepdims=True)
    o = jnp.dot(p.astype(BF16), v_ref[...], preferred_element_type=F32) / l
    o_ref[...] = o.astype(o_ref.dtype)


def ctx_self_attention(o_buf, q_arr, q_c0, k_arr, k_c0, v_arr, v_c0, n_heads, group, scale):
    b, s, dm = o_buf.shape
    return pl.pallas_call(
        functools.partial(_ctx_attn_kernel, scale=scale),
        out_shape=jax.ShapeDtypeStruct(o_buf.shape, o_buf.dtype),
        grid=(b, n_heads),
        in_specs=[pl.BlockSpec((None, CTX_LEN, LANES), lambda bi, h: (bi, 0, q_c0 + h)),
                  pl.BlockSpec((None, CTX_LEN, LANES), lambda bi, h: (bi, 0, k_c0 + h // group)),
                  pl.BlockSpec((None, CTX_LEN, LANES), lambda bi, h: (bi, 0, v_c0 + h // group)),
                  pl.BlockSpec(memory_space=pl.ANY)],
        out_specs=pl.BlockSpec((None, CTX_LEN, LANES), lambda bi, h: (bi, 0, h)),
        input_output_aliases={3: 0},
        compiler_params=_params(("parallel", "parallel")),
        name="ctx_self_attention",
    )(q_arr, k_arr, v_arr, o_buf)


def _na_kernel(q_ref, k_ref, v_ref, b_ref, o_ref, *, scale, n_rows):
    blk = pl.program_id(2)
    base = jnp.clip(NA_ROW_BLOCK * blk - NA_KH // 2, 0, n_rows - NA_KEY_ROWS)
    start = pl.multiple_of(CTX_LEN + base * GRID_W, GRID_W)
    nwin = NA_KEY_ROWS * GRID_W
    ones_w, ones_c = jnp.ones((nwin, LANES), BF16), jnp.ones((CTX_LEN, LANES), BF16)
    for hh in range(q_ref.shape[1] // LANES):
        cols = slice(hh * LANES, (hh + 1) * LANES)
        q = q_ref[:, cols]
        s_w = _dot_nt(q, k_ref[pl.ds(start, nwin), cols]) * scale + b_ref[hh]
        s_c = _dot_nt(q, k_ref[0:CTX_LEN, cols]) * scale
        m = jnp.maximum(jnp.max(s_w, axis=-1, keepdims=True), jnp.max(s_c, axis=-1, keepdims=True))
        v_w = jnp.concatenate([v_ref[pl.ds(start, nwin), cols], ones_w], axis=1)
        v_c = jnp.concatenate([v_ref[0:CTX_LEN, cols], ones_c], axis=1)
        pv = (jnp.dot(jnp.exp2(s_w - m).astype(BF16), v_w, preferred_element_type=F32)
              + jnp.dot(jnp.exp2(s_c - m).astype(BF16), v_c, preferred_element_type=F32))
        o_ref[:, cols] = (pv[:, :LANES] / pv[:, LANES:]).astype(o_ref.dtype)


def _na_bias_tables(rpb, n_rows):
    rb, kr, w = NA_ROW_BLOCK, NA_KEY_ROWS, GRID_W
    kh = min(NA_KH, n_rows)
    nblk = n_rows // rb
    n_h, n_ri, n_ci = rpb.shape
    qc = np.arange(w)
    c0 = np.clip(qc - NA_KW // 2, 0, w - NA_KW)
    kc = np.arange(w)
    col_ok = (kc[None, :] >= c0[:, None]) & (kc[None, :] < c0[:, None] + NA_KW)
    col_idx = np.clip(kc[None, :] - qc[:, None], 1 - NA_KW, NA_KW - 1) + NA_KW - 1
    onehot = (col_idx.reshape(1, w * w) == np.arange(n_ci)[:, None]).astype(np.float32)
    by_col = jnp.dot(rpb.astype(F32).reshape(n_h * n_ri, n_ci) * LOG2E, jnp.asarray(onehot),
                     precision=lax.Precision.HIGHEST).reshape(n_h, n_ri, w, w)
    cases = []
    for blk in range(nblk):
        base = int(np.clip(rb * blk - NA_KH // 2, 0, n_rows - kr))
        r = rb * blk + np.arange(rb)
        r0 = np.clip(r - kh // 2, 0, n_rows - kh)
        krow = base + np.arange(kr)
        row_ok = (krow[None, :] >= r0[:, None]) & (krow[None, :] < r0[:, None] + kh)
        row_idx = np.clip(krow[None, :] - r[:, None] + NA_KH - 1, 0, 2 * NA_KH - 2)
        cases.append((base - rb * blk, row_ok, row_idx))
    for blk in range(2, nblk - 1):
        assert cases[blk][0] == cases[1][0] and all(np.array_equal(cases[blk][i], cases[1][i]) for i in (1, 2))
    out = []
    for blk in (0, 1, nblk - 1):
        _, row_ok, row_idx = cases[blk]
        tab = jnp.stack([jnp.stack([by_col[:, int(row_idx[a, b2])] for b2 in range(kr)], axis=2)
                         for a in range(rb)], axis=1)
        ok = row_ok[:, None, :, None] & col_ok[None, :, None, :]
        out.append(jnp.where(ok[None], tab, NEG_INF).reshape(n_h, rb * w, kr * w))
    return jnp.stack(out, axis=0)


def neighbourhood_attention(qkv, rpb, n_heads):
    b, s, _ = qkv.shape
    n_rows = (s - CTX_LEN) // GRID_W
    nblk = n_rows // NA_ROW_BLOCK
    tq = NA_ROW_BLOCK * GRID_W
    assert tq == CTX_LEN
    nwin = NA_KEY_ROWS * GRID_W
    bias = _na_bias_tables(rpb, n_rows)
    scale = LANES ** -0.5 * LOG2E

    hps = _pick(n_heads, (NA_HEADS_PER_STEP, 1))
    ng = n_heads // hps
    wide = hps * LANES

    def bias_case(blk):
        return jnp.where(blk == 0, 0, jnp.where(blk == nblk - 1, 2, 1))

    return pl.pallas_call(
        functools.partial(_na_kernel, scale=scale, n_rows=n_rows),
        out_shape=jax.ShapeDtypeStruct((b, s, n_heads * LANES), BF16),
        grid=(b, ng, nblk),
        in_specs=[pl.BlockSpec((None, tq, wide), lambda bi, h, i: (bi, i + 1, h)),
                  pl.BlockSpec((None, s, wide), lambda bi, h, i: (bi, 0, ng + h)),
                  pl.BlockSpec((None, s, wide), lambda bi, h, i: (bi, 0, 2 * ng + h)),
                  pl.BlockSpec((None, hps, tq, nwin), lambda bi, h, i: (bias_case(i), h, 0, 0))],
        out_specs=pl.BlockSpec((None, tq, wide), lambda bi, h, i: (bi, i + 1, h)),
        compiler_params=_params(("parallel", "parallel", "arbitrary")),
        name="neighbourhood_attention",
    )(qkv, qkv, qkv, bias)


def _gqa_kernel(q_ref, k_ref, v_ref, o_ref, m_sc, acc_sc, *, tk, group):
    m_sc[...] = jnp.full(m_sc.shape, -jnp.inf, F32)
    acc_sc[...] = jnp.zeros(acc_sc.shape, F32)
    ones = jnp.ones((tk, LANES), BF16)

    def body(c, carry):
        rows = pl.ds(pl.multiple_of(c * tk, tk), tk)
        kc = k_ref[rows, :]
        vc = jnp.concatenate([v_ref[rows, :], ones], axis=1)
        for g in range(group):
            s = _dot_nt(q_ref[:, g * LANES:(g + 1) * LANES], kc)
            m_prev = m_sc[g]
            m_new = jnp.maximum(m_prev, jnp.max(s, axis=-1, keepdims=True))
            alpha = jnp.exp2(m_prev - m_new)
            p = jnp.exp2(s - m_new[:, :1])
            pv = jnp.dot(p.astype(BF16), vc, preferred_element_type=F32)
            acc_sc[g] = jnp.concatenate([alpha, alpha], axis=1) * acc_sc[g] + pv
            m_sc[g] = m_new
        return carry

    lax.fori_loop(0, k_ref.shape[0] // tk, body, 0, unroll=True)
    for g in range(group):
        a = acc_sc[g]
        o_ref[:, g * LANES:(g + 1) * LANES] = (a[:, :LANES] / a[:, LANES:]).astype(o_ref.dtype)


def global_gqa_attention(qp, kp, qkv, n_heads, n_kv):
    b, s, _ = qp.shape
    group = n_heads // n_kv
    tq = CTX_LEN
    nq = s // tq - 1
    tk = _pick(s, (768, 512, 256))
    v_c0 = n_heads + n_kv
    return pl.pallas_call(
        functools.partial(_gqa_kernel, tk=tk, group=group),
        out_shape=jax.ShapeDtypeStruct((b, s, n_heads * LANES), BF16),
        grid=(b, n_kv, nq),
        in_specs=[pl.BlockSpec((None, tq, group * LANES), lambda bi, h, i: (bi, i + 1, h)),
                  pl.BlockSpec((None, s, LANES), lambda bi, h, i: (bi, 0, h)),
                  pl.BlockSpec((None, s, LANES), lambda bi, h, i: (bi, 0, v_c0 + h))],
        out_specs=pl.BlockSpec((None, tq, group * LANES), lambda bi, h, i: (bi, i + 1, h)),
        scratch_shapes=[pltpu.VMEM((group, tq, LANES), F32), pltpu.VMEM((group, tq, 2 * LANES), F32)],
        compiler_params=_params(("parallel", "parallel", "arbitrary")),
        name="global_gqa_attention",
    )(qp, kp, qkv)


def _swa_kernel(q_ref, k_ref, v_ref, sink_ref, o_ref, *, head_dim, n_lat):
    i = pl.program_id(2)
    tq = q_ref.shape[0]
    n_ctx_blocks = CTX_LEN // SWA_BLOCK
    band = SWA_BLOCK + 2 * SWA_WINDOW

    @pl.when(i < n_ctx_blocks)
    def _():
        o_ref[...] = jnp.zeros(o_ref.shape, o_ref.dtype)

    @pl.when(i >= n_ctx_blocks)
    def _():
        qb = i - n_ctx_blocks
        row0 = jnp.clip(CTX_LEN + SWA_BLOCK * qb - SWA_WINDOW, 0, k_ref.shape[0] - band)
        row0 = pl.multiple_of(row0, SWA_BLOCK)
        nk = band + CTX_LEN
        k_all = jnp.concatenate([k_ref[pl.ds(row0, band), :], k_ref[0:CTX_LEN, :]], axis=0)
        v_all = jnp.concatenate([jnp.concatenate([v_ref[pl.ds(row0, band), :], v_ref[0:CTX_LEN, :]], axis=0),
                                 jnp.ones((nk, LANES), BF16)], axis=1)
        qpos = SWA_BLOCK * qb + lax.broadcasted_iota(jnp.int32, (tq, nk), 0)
        col = lax.broadcasted_iota(jnp.int32, (tq, nk), 1)
        kpos = row0 - CTX_LEN + col
        ok = (col >= band) | ((kpos >= 0) & (kpos < n_lat) & (jnp.abs(kpos - qpos) <= SWA_WINDOW))
        hpg = LANES // head_dim
        lane = lax.broadcasted_iota(jnp.int32, (tq, LANES), 1)
        sels = [(lane >= j * head_dim) & (lane < (j + 1) * head_dim) for j in range(hpg)]
        q_all = q_ref[...]
        n_groups = q_all.shape[1] // LANES
        gpc = min(SWA_GROUPS_PER_CHAIN, n_groups)
        ok = jnp.concatenate([ok] * (hpg * gpc), axis=0)
        for c0 in range(0, n_groups, gpc):
            heads = [(c, j) for c in range(c0, c0 + gpc) for j in range(hpg)]
            qs = jnp.concatenate([jnp.where(sels[j], q_all[:, c * LANES:(c + 1) * LANES], jnp.zeros((tq, LANES), BF16))
                                  for c, j in heads], axis=0)
            sink = jnp.concatenate([jnp.broadcast_to(sink_ref[:, c * hpg + j:c * hpg + j + 1], (tq, 1))
                                    for c, j in heads], axis=0)
            s = jnp.where(ok, _dot_nt(qs, k_all), NEG_INF)
            m = jnp.maximum(jnp.max(s, axis=-1, keepdims=True), sink)
            pv = jnp.dot(jnp.exp2(s - m).astype(BF16), v_all, preferred_element_type=F32)
            o = pv[:, :LANES] / (pv[:, LANES:] + jnp.exp2(sink - m))
            for ci, c in enumerate(range(c0, c0 + gpc)):
                res = o[ci * hpg * tq:(ci * hpg + 1) * tq]
                for j in range(1, hpg):
                    res = jnp.where(sels[j], o[(ci * hpg + j) * tq:(ci * hpg + j + 1) * tq], res)
                o_ref[:, c * LANES:(c + 1) * LANES] = res.astype(o_ref.dtype)


def window_gqa_attention(q, k2, v2, sink, n_heads, n_kv, head_dim):
    b, s, _ = q.shape
    group = n_heads // n_kv
    gcols = group * head_dim
    assert gcols % LANES == 0 and LANES % head_dim == 0
    tq = SWA_BLOCK
    return pl.pallas_call(
        functools.partial(_swa_kernel, head_dim=head_dim, n_lat=s - CTX_LEN),
        out_shape=jax.ShapeDtypeStruct((b, s, n_heads * head_dim), BF16),
        grid=(b, n_kv, s // tq),
        in_specs=[pl.BlockSpec((None, tq, gcols), lambda bi, h, i: (bi, i, h)),
                  pl.BlockSpec((None, s, LANES), lambda bi, h, i: (bi, 0, h)),
                  pl.BlockSpec((None, s, LANES), lambda bi, h, i: (bi, 0, h)),
                  pl.BlockSpec((None, 1, group), lambda bi, h, i: (h, 0, 0))],
        out_specs=pl.BlockSpec((None, tq, gcols), lambda bi, h, i: (bi, i, h)),
        compiler_params=_params(("parallel", "parallel", "arbitrary")),
        name="window_gqa_attention",
    )(q, k2, v2, sink.astype(F32).reshape(n_kv, 1, group))


def _retention_kernel(qf_ref, kf_ref, vf_ref, cosf_ref, sinf_ref, qb_ref, kb_ref, vb_ref, cosb_ref, sinb_ref,
                      intra_ref, qdec_ref, kdec_ref, cdec_ref, of_ref, ob_ref, state_sc, *, k_scale, heads, dh):
    @pl.when(pl.program_id(2) == 0)
    def _():
        state_sc[...] = jnp.zeros(state_sc.shape, F32)

    half = dh // 2

    def rope(x, cos, sin):
        swapped = jnp.concatenate([x[:, half:], x[:, :half]], axis=1)
        return x * cos + swapped * sin

    dirs = ((qf_ref, kf_ref, vf_ref, cosf_ref, sinf_ref, of_ref), (qb_ref, kb_ref, vb_ref, cosb_ref, sinb_ref, ob_ref))
    for dr, (q_ref, k_ref, v_ref, cos_ref, sin_ref, o_ref) in enumerate(dirs):
        cos, sin = cos_ref[...], sin_ref[...]
        for hh in range(heads):
            cols = slice(hh * dh, (hh + 1) * dh)
            q = rope(q_ref[:, cols].astype(F32), cos, sin)
            k = rope(k_ref[:, cols].astype(F32), cos, sin) * k_scale
            v = v_ref[:, cols]
            state = state_sc[dr, hh]
            att = _dot_nt(q.astype(BF16), k.astype(BF16)) * intra_ref[dr, hh]
            y = (jnp.dot(att.astype(BF16), v, preferred_element_type=F32)
                 + jnp.dot((q * qdec_ref[dr, hh]).astype(BF16), state.astype(BF16), preferred_element_type=F32))
            kd = (k * kdec_ref[dr, hh]).astype(BF16)
            state_sc[dr, hh] = state * cdec_ref[dr, hh] + lax.dot_general(kd, v, (((0,), (0,)), ((), ())),
                                                                         preferred_element_type=F32)
            y = y * lax.rsqrt(jnp.mean(y * y, axis=-1, keepdims=True) + NORM_EPS)
            o_ref[:, cols] = y.astype(o_ref.dtype)


def retention_scan(proj, decay_exp, cos, sin, n_heads):
    b, s, five_d = proj.shape
    d = five_d // 5
    dh = d // n_heads
    c = RET_CHUNK
    n_chunks = s // c
    n_ctx_chunks = CTX_LEN // c
    hps = min(RET_HEADS_PER_STEP, n_heads)
    lg = jnp.log1p(-jnp.exp2(-decay_exp.astype(F32)))
    pos = jnp.arange(c, dtype=F32)
    diff = pos[:, None] - pos[None, :]
    lgf, lgb = lg[0][:, None, None], lg[1][:, None, None]
    intra_f = jnp.where(diff >= 0, jnp.exp(lgf * jnp.maximum(diff, 0.0)), 0.0)
    intra_b = jnp.where(diff <= 0, jnp.exp(lgb * jnp.maximum(-diff, 0.0)), 0.0)
    intra = jnp.stack([intra_f, intra_b])
    qdec = jnp.stack([jnp.exp(lg[0][:, None] * (pos + 1.0)), jnp.exp(lg[1][:, None] * (c - pos))])[..., None]
    kdec = jnp.stack([jnp.exp(lg[0][:, None] * (c - 1.0 - pos)), jnp.exp(lg[1][:, None] * pos)])[..., None]
    cdec = jnp.exp(lg * c)[..., None, None]
    ncb = n_heads // hps

    def bwd_chunk(t):
        return jnp.where(t < n_ctx_chunks, n_ctx_chunks - 1 - t, n_chunks - 1 - (t - n_ctx_chunks))

    fwd = lambda off: (lambda bi, h, t: (bi, t, off * ncb + h))
    bwd = lambda off: (lambda bi, h, t: (bi, bwd_chunk(t), off * ncb + h))
    tab = lambda bi, h, t: (0, h, 0, 0)
    blk = (None, c, hps * dh)
    return pl.pallas_call(
        functools.partial(_retention_kernel, k_scale=dh ** -0.5, heads=hps, dh=dh),
        out_shape=(jax.ShapeDtypeStruct((b, s, d), BF16), jax.ShapeDtypeStruct((b, s, d), BF16)),
        grid=(b, ncb, n_chunks),
        in_specs=[pl.BlockSpec(blk, fwd(0)), pl.BlockSpec(blk, fwd(1)), pl.BlockSpec(blk, fwd(2)),
                  pl.BlockSpec((c, dh), lambda bi, h, t: (t, 0)), pl.BlockSpec((c, dh), lambda bi, h, t: (t, 0)),
                  pl.BlockSpec(blk, bwd(0)), pl.BlockSpec(blk, bwd(1)), pl.BlockSpec(blk, bwd(2)),
                  pl.BlockSpec((c, dh), lambda bi, h, t: (bwd_chunk(t), 0)),
                  pl.BlockSpec((c, dh), lambda bi, h, t: (bwd_chunk(t), 0)),
                  pl.BlockSpec((2, hps, c, c), tab),
                  pl.BlockSpec((2, hps, c, 1), tab),
                  pl.BlockSpec((2, hps, c, 1), tab),
                  pl.BlockSpec((2, hps, 1, 1), tab)],
        out_specs=(pl.BlockSpec(blk, lambda bi, h, t: (bi, t, h)),
                   pl.BlockSpec(blk, lambda bi, h, t: (bi, bwd_chunk(t), h))),
        scratch_shapes=[pltpu.VMEM((2, hps, dh, dh), F32)],
        compiler_params=_params(("parallel", "parallel", "arbitrary")),
        name="retention_scan",
    )(proj, proj, proj, cos, sin, proj, proj, proj, cos, sin, intra, qdec, kdec, cdec)


def _ret_merge_kernel(yf_ref, yb_ref, gf_ref, gb_ref, o_ref):
    gf = gf_ref[...].astype(F32)
    gb = gb_ref[...].astype(F32)
    o = gf * jax.nn.sigmoid(gf) * yf_ref[...].astype(F32) + gb * jax.nn.sigmoid(gb) * yb_ref[...].astype(F32)
    o_ref[...] = o.astype(o_ref.dtype)


def retention_merge(y_f, y_b, proj):
    b, s, d = y_f.shape
    tr = _pick(s, (1408, 1024, 768, 512, 256))
    bc = _pick(d, (1024, 512, 256, 128))
    nc = d // bc
    return pl.pallas_call(
        _ret_merge_kernel,
        out_shape=jax.ShapeDtypeStruct((b, s, d), BF16),
        grid=(b, s // tr, nc),
        in_specs=[pl.BlockSpec((None, tr, bc), lambda bi, i, j: (bi, i, j)),
                  pl.BlockSpec((None, tr, bc), lambda bi, i, j: (bi, i, j)),
                  pl.BlockSpec((None, tr, bc), lambda bi, i, j: (bi, i, 3 * nc + j)),
                  pl.BlockSpec((None, tr, bc), lambda bi, i, j: (bi, i, 4 * nc + j))],
        out_specs=pl.BlockSpec((None, tr, bc), lambda bi, i, j: (bi, i, j)),
        compiler_params=_params(("parallel", "parallel", "parallel")),
        name="retention_merge",
    )(y_f, y_b, proj, proj)


def _moe_kernel(be_ref, first_ref, nused_ref, x_ref, w1_ref, perm_ref, b1_ref, w2_ref, b2_ref, g_ref, o_ref,
                w1p_sc, w2p_sc):
    i = pl.program_id(0)
    f = w2_ref.shape[0]
    fp = w2p_sc.shape[0]

    @pl.when(i == 0)
    def _():
        w2p_sc[...] = jnp.zeros(w2p_sc.shape, w2p_sc.dtype)

    @pl.when(first_ref[i] == 1)
    def _():
        w1p_sc[...] = jnp.dot(w1_ref[...].astype(BF16), perm_ref[...], preferred_element_type=F32).astype(BF16)
        w2p_sc[0:f, :] = w2_ref[...].astype(BF16)

    @pl.when(i < nused_ref[0])
    def _():
        u = jnp.dot(x_ref[...], w1p_sc[...], preferred_element_type=F32) + b1_ref[...]
        glu = jnp.minimum(u[:, :fp], SWIGLU_LIMIT)
        lin = jnp.clip(u[:, fp:], -SWIGLU_LIMIT, SWIGLU_LIMIT)
        a = glu * jax.nn.sigmoid(SWIGLU_ALPHA * glu) * (lin + 1.0)
        y = jnp.dot(a.astype(BF16), w2p_sc[...], preferred_element_type=F32) + b2_ref[...]
        o_ref[...] = (y * g_ref[...]).astype(o_ref.dtype)

    @pl.when(i >= nused_ref[0])
    def _():
        o_ref[...] = jnp.zeros(o_ref.shape, o_ref.dtype)


def moe_experts(xg, blk_expert, blk_first, n_used, slot_gate, layer, w1, b1, w2, b2):
    n_slots, d = xg.shape
    _, n_e, _, f2 = w1.shape
    f = f2 // 2
    fp = -(-f // LANES) * LANES
    n_blocks = n_slots // MOE_BLOCK
    perm = np.zeros((f2, 2 * fp), np.float32)
    perm[2 * np.arange(f), np.arange(f)] = 1.0
    perm[2 * np.arange(f) + 1, fp + np.arange(f)] = 1.0
    b1l = b1[layer]
    b1p = jnp.zeros((n_e, 1, 2 * fp), F32).at[:, 0, :f].set(b1l[:, 0::2]).at[:, 0, fp:fp + f].set(b1l[:, 1::2])

    def xrow(i, be, first, nused):
        return (jnp.minimum(i, jnp.maximum(nused[0] - 1, 0)), 0)

    def expert(i, be, first, nused):
        return (be[i], 0, 0)

    def layer_expert(i, be, first, nused):
        return (layer, be[i], 0, 0)

    grid_spec = pltpu.PrefetchScalarGridSpec(
        num_scalar_prefetch=3,
        grid=(n_blocks,),
        in_specs=[pl.BlockSpec((MOE_BLOCK, d), xrow),
                  pl.BlockSpec((None, None, d, f2), layer_expert),
                  pl.BlockSpec((f2, 2 * fp), lambda i, be, first, nused: (0, 0)),
                  pl.BlockSpec((None, 1, 2 * fp), expert),
                  pl.BlockSpec((None, None, f, d), layer_expert),
                  pl.BlockSpec((None, 1, d), expert),
                  pl.BlockSpec((MOE_BLOCK, 1), xrow)],
        out_specs=pl.BlockSpec((MOE_BLOCK, d), lambda i, be, first, nused: (i, 0)),
        scratch_shapes=[pltpu.VMEM((d, 2 * fp), BF16), pltpu.VMEM((fp, d), BF16)],
    )
    return pl.pallas_call(
        _moe_kernel,
        out_shape=jax.ShapeDtypeStruct((n_slots, d), BF16),
        grid_spec=grid_spec,
        compiler_params=_params(("arbitrary",)),
        name="moe_experts",
    )(blk_expert, blk_first, n_used, xg, w1, jnp.asarray(perm, BF16), b1p, w2,
      b2[layer].astype(F32).reshape(n_e, 1, d), slot_gate)


def _moe_combine_kernel(x_ref, g_ref, y0_ref, y1_ref, y2_ref, y3_ref, o_ref):
    y = (y0_ref[...].astype(F32) + y1_ref[...].astype(F32)) + (y2_ref[...].astype(F32) + y3_ref[...].astype(F32))
    o_ref[...] = x_ref[...] + g_ref[0] * y


def moe_combine_residual(xs2, gate_groups, ysg, tok0):
    d = xs2.shape[1]
    t_part = ysg.shape[0] // TOP_K
    assert TOP_K == 4
    tr = CTX_LEN
    bn = _pick(d, (2048, 1024, 512, 256, 128))
    nc = d // bn
    nr = t_part // tr
    r0 = tok0 // tr
    yspec = [pl.BlockSpec((tr, bn), functools.partial(lambda i, j, k: (k * nr + i, j), k=k)) for k in range(TOP_K)]
    return pl.pallas_call(
        _moe_combine_kernel,
        out_shape=jax.ShapeDtypeStruct(xs2.shape, F32),
        grid=(nr, nc),
        in_specs=[pl.BlockSpec((tr, bn), lambda i, j: (r0 + i, j)),
                  pl.BlockSpec((1, 1, bn), lambda i, j: (r0 + i, 0, j))] + yspec,
        out_specs=pl.BlockSpec((tr, bn), lambda i, j: (r0 + i, j)),
        input_output_aliases={0: 0},
        compiler_params=_params(("parallel", "parallel")),
        name="moe_combine_residual",
    )(xs2, gate_groups, ysg, ysg, ysg, ysg)


def moe_layer(xs2, gate_groups, h, top_idx, top_gate, layer, w1, b1, w2, b2):
    t_all = h.shape[0]
    parts = MOE_TOKEN_PARTS if t_all % (MOE_TOKEN_PARTS * CTX_LEN) == 0 else 1
    t_part = t_all // parts
    for part in range(parts):
        cols = slice(part * t_part, (part + 1) * t_part)
        xs2 = _moe_part(xs2, gate_groups, h, top_idx[:, cols], top_gate[:, cols], part * t_part, layer,
                        w1, b1, w2, b2)
    return xs2


def _moe_part(xs2, gate_groups, h, top_idx, top_gate, tok0, layer, w1, b1, w2, b2):
    t_tok = top_idx.shape[1]
    n_e = w1.shape[1]
    n_assign = t_tok * TOP_K
    n_blocks = -(-n_assign // MOE_BLOCK) + n_e
    n_slots = n_blocks * MOE_BLOCK
    n_extra = n_slots - n_assign
    e_flat = top_idx.reshape(n_assign)
    counts = jnp.sum((e_flat[:, None] == jnp.arange(n_e, dtype=jnp.int32)[None, :]).astype(jnp.int32), axis=0)
    padded = (counts + MOE_BLOCK - 1) // MOE_BLOCK * MOE_BLOCK
    n_used = (jnp.sum(padded) // MOE_BLOCK).astype(jnp.int32)
    pad_cum = jnp.cumsum(padded - counts)
    pad_key = jnp.sum((jnp.arange(n_extra, dtype=jnp.int32)[:, None] >= pad_cum[None, :]).astype(jnp.int32), axis=1)
    keys = jnp.concatenate([e_flat, pad_key])
    ids = jnp.arange(n_slots, dtype=jnp.int32)
    tok = jnp.concatenate([jnp.tile(jnp.arange(t_tok, dtype=jnp.int32), TOP_K), jnp.zeros((n_extra,), jnp.int32)])
    gate = jnp.concatenate([top_gate.reshape(n_assign), jnp.zeros((n_extra,), F32)])
    slot_key, slot_id, slot_tok, slot_gate = lax.sort((keys, ids, tok, gate), dimension=0, is_stable=True, num_keys=1)
    _, slot_of = lax.sort_key_val(slot_id, ids)
    blk_expert = jnp.minimum(slot_key.reshape(n_blocks, MOE_BLOCK)[:, 0], n_e - 1)
    prev = jnp.concatenate([jnp.full((1,), -1, jnp.int32), blk_expert[:-1]])
    blk_first = ((blk_expert != prev) & (jnp.arange(n_blocks) < n_used)).astype(jnp.int32)
    xg = h.at[slot_tok + tok0].get(mode="promise_in_bounds")
    ys = moe_experts(xg, blk_expert, blk_first, n_used.reshape(1), slot_gate.reshape(n_slots, 1), layer,
                     w1, b1, w2, b2)
    ysg = ys.at[slot_of[:n_assign]].get(mode="promise_in_bounds")
    return moe_combine_residual(xs2, gate_groups, ysg, tok0)


def _group_rows(vec_ctx, vec_lat, s):
    b, d = vec_lat.shape
    ng = s // CTX_LEN
    g = jnp.concatenate([jnp.broadcast_to(vec_ctx[None, None, :], (b, 1, d)),
                         jnp.broadcast_to(vec_lat[:, None, :], (b, ng - 1, d))], axis=1)
    return g.reshape(b * ng, 1, d)


def kernel(x, c, ctx, c_ctx, w_mod, b_mod, g_mix, g_ffn, g_final, na_w_qkv, na_w_o, na_rpb, ret_w_in, ret_w_o, ret_decay_exp, gqa_w_qkv, gqa_w_o, gqa_q_gain, gqa_k_gain, swa_w_qkv, swa_w_o, swa_sink, moe_w_router, moe_b_router, moe_w1, moe_b1, moe_w2, moe_b2):
    b, l, d = x.shape
    assert ctx.shape[1] == CTX_LEN and l % CTX_LEN == 0
    s = CTX_LEN + l
    depth = w_mod.shape[0]
    xs = jnp.concatenate([ctx, x], axis=1)

    n_rows = -(-(b + 1) // 16) * 16
    s_rows = jnp.zeros((n_rows, d), F32).at[:b].set(jax.nn.silu(c)).at[b].set(jax.nn.silu(c_ctx))
    mod = modulation_all_layers(s_rows, w_mod, b_mod).reshape(depth, n_rows, N_MOD, d)

    ones = jnp.ones((1, LANES), F32)
    for li in range(depth):
        mix, j = li % N_MIXERS, li // N_MIXERS
        need_ctx = li < depth - 1
        mod_lat, mod_ctx = mod[li, :b], mod[li, b]
        modv = jnp.stack([jnp.broadcast_to(mod_ctx[None], (b, N_MOD, d)), mod_lat], axis=1)
        h = norm_modulate(xs, g_mix[li], modv, 0)
        h2 = h.reshape(b * s, d)
        if mix == 0:
            qkv = matmul(h2, na_w_qkv[j]).reshape(b, s, 3 * d)
            o = neighbourhood_attention(qkv, na_rpb[j], NA_HEADS)
            o = ctx_self_attention(o, qkv, 0, qkv, NA_HEADS, qkv, 2 * NA_HEADS, NA_HEADS, 1, LANES ** -0.5 * LOG2E)
            w_o = na_w_o[j]
        elif mix == 1:
            proj = matmul(h2, ret_w_in[j]).reshape(b, s, 5 * d)
            cos, sin = rope_tables(l, d // RET_HEADS)
            y_f, y_b = retention_scan(proj, ret_decay_exp[j], cos, sin, RET_HEADS)
            o = retention_merge(y_f, y_b, proj)
            w_o = ret_w_o[j]
        elif mix == 2:
            dh = d // GQA_HEADS
            assert dh == LANES
            qkv = matmul(h2, gqa_w_qkv[j]).reshape(b, s, -1)
            cos, sin = rope_tables(l, dh)
            qg = (gqa_q_gain[j].astype(F32) * (dh ** -0.5 * LOG2E)).reshape(1, LANES)
            kg = gqa_k_gain[j].astype(F32).reshape(1, LANES)
            qp = qk_prepare(qkv, 0, GQA_HEADS * dh, qg, cos, sin, dh, True)
            kp = qk_prepare(qkv, GQA_HEADS * dh, GQA_KV_HEADS * dh, kg, cos, sin, dh, True)
            o = global_gqa_attention(qp, kp, qkv, GQA_HEADS, GQA_KV_HEADS)
            o = ctx_self_attention(o, qp, 0, kp, 0, qkv, GQA_HEADS + GQA_KV_HEADS, GQA_HEADS,
                                   GQA_HEADS // GQA_KV_HEADS, 1.0)
            w_o = gqa_w_o[j]
        else:
            dh = d // SWA_HEADS
            qkv = matmul(h2, swa_w_qkv[j]).reshape(b, s, -1)
            cos, sin = rope_tables(l, dh)
            reps = LANES // dh
            cos, sin = jnp.tile(cos, (1, reps)), jnp.tile(sin, (1, reps))
            unit = jnp.zeros((s, LANES), F32)
            qp = qk_prepare(qkv, 0, SWA_HEADS * dh, ones * (dh ** -0.5 * LOG2E), cos, sin, dh, False)
            k2 = qk_prepare(qkv, SWA_HEADS * dh, SWA_KV_HEADS * dh, ones, cos, sin, dh, False, dup=True)
            v2 = qk_prepare(qkv, (SWA_HEADS + SWA_KV_HEADS) * dh, SWA_KV_HEADS * dh, ones,
                            jnp.ones((s, LANES), F32), unit, dh, False, dup=True)
            o = window_gqa_attention(qp, k2, v2, swa_sink[j].astype(F32) * LOG2E, SWA_HEADS, SWA_KV_HEADS, dh)
            w_o = swa_w_o[j]
        if mix in (0, 2) and not need_ctx:
            o = o.at[:, :CTX_LEN].set(0)
        xs = matmul_residual(o.reshape(b * s, d), w_o, xs.reshape(b * s, d),
                             _group_rows(mod_ctx[2], mod_lat[:, 2], s)).reshape(b, s, d)

        h, top_idx, top_gate = norm_modulate_route(xs, g_ffn[li], modv, 3, moe_w_router[li], moe_b_router[li])
        top_idx = top_idx[:, :TOP_K].transpose(1, 0, 2).reshape(TOP_K, b * s)
        top_gate = top_gate[:, :TOP_K].transpose(1, 0, 2).reshape(TOP_K, b * s)
        xs = moe_layer(xs.reshape(b * s, d), _group_rows(mod_ctx[5], mod_lat[:, 5], s), h.reshape(b * s, d),
                       top_idx, top_gate, li, moe_w1, moe_b1, moe_w2, moe_b2).reshape(b, s, d)
    return final_norm(xs, g_final)
```

```python
import functools

import jax
import jax.numpy as jnp
import numpy as np
from jax import lax
from jax.experimental import pallas as pl
from jax.experimental.pallas import tpu as pltpu

DEPTH = 4
GRID_W = 64
CTX_LEN = 256
N_MIXERS = 4
N_MOD = 6
NORM_EPS = 1e-6
NEG_INF = -1e30
ROPE_THETA = 10000.0

NA_HEADS = 32
NA_KH = 8
NA_KW = 16
NA_ROW_BLOCK = 4
NA_KEY_ROWS = 12
NA_HEADS_PER_STEP = 4

RET_HEADS = 16
RET_CHUNK = 128
RET_HEADS_PER_STEP = 4

GQA_HEADS = 32
GQA_KV_HEADS = 8

SWA_HEADS = 64
SWA_KV_HEADS = 8
SWA_WINDOW = 128
SWA_BLOCK = 128
SWA_GROUPS_PER_CHAIN = 2

N_EXPERTS = 32
TOP_K = 4
SWIGLU_ALPHA = 1.702
SWIGLU_LIMIT = 7.0
MOE_BLOCK = 256
MOE_TOKEN_PARTS = 1

LOG2E = 1.4426950408889634
LANES = 128
SUBLANES = 8
VMEM_LIMIT_BYTES = 56 * 1024 * 1024

F32 = jnp.float32
BF16 = jnp.bfloat16


def _params(semantics, vmem=VMEM_LIMIT_BYTES):
    return pltpu.CompilerParams(dimension_semantics=semantics, vmem_limit_bytes=vmem)


def _pick(n, candidates):
    for c in candidates:
        if n % c == 0:
            return c
    return n


def _mod_kernel(s_ref, w_ref, b_ref, o_ref):
    acc = jnp.dot(s_ref[...], w_ref[...].astype(BF16), preferred_element_type=F32)
    o_ref[...] = acc + b_ref[...]


def modulation_all_layers(s_rows, w_mod, b_mod):
    depth, d, n = w_mod.shape
    rows = s_rows.shape[0]
    bn = _pick(n, (512, 256, 128))
    return pl.pallas_call(
        _mod_kernel,
        out_shape=jax.ShapeDtypeStruct((depth, rows, n), F32),
        grid=(depth, n // bn),
        in_specs=[
            pl.BlockSpec((rows, d), lambda l, j: (0, 0)),
            pl.BlockSpec((None, d, bn), lambda l, j: (l, 0, j)),
            pl.BlockSpec((None, 1, bn), lambda l, j: (l, 0, j)),
        ],
        out_specs=pl.BlockSpec((None, rows, bn), lambda l, j: (l, 0, j)),
        compiler_params=_params(("parallel", "parallel")),
        name="modulation",
    )(s_rows.astype(BF16), w_mod, b_mod.reshape(depth, 1, n))


def _rms_mod(x, g, shift, scale):
    y = x * lax.rsqrt(jnp.mean(x * x, axis=-1, keepdims=True) + NORM_EPS) * g
    return y * (1.0 + scale) + shift


def _norm_mod_kernel(x_ref, g_ref, m_ref, o_ref, *, shift_idx):
    h = _rms_mod(x_ref[...], g_ref[...], m_ref[shift_idx:shift_idx + 1, :], m_ref[shift_idx + 1:shift_idx + 2, :])
    o_ref[...] = h.astype(o_ref.dtype)


def _norm_route_kernel(x_ref, g_ref, m_ref, wh_ref, wl_ref, br_ref, o_ref, idx_ref, gate_ref, *, shift_idx):
    h = _rms_mod(x_ref[...], g_ref[...], m_ref[shift_idx:shift_idx + 1, :], m_ref[shift_idx + 1:shift_idx + 2, :])
    h_hi = h.astype(BF16)
    o_ref[...] = h_hi.astype(o_ref.dtype)
    h_lo = (h - h_hi.astype(F32)).astype(BF16)
    logits = (jnp.dot(h_hi, wh_ref[...], preferred_element_type=F32)
              + (jnp.dot(h_lo, wh_ref[...], preferred_element_type=F32)
                 + jnp.dot(h_hi, wl_ref[...], preferred_element_type=F32))) + br_ref[...]
    rows, n_e = logits.shape
    e_iota = lax.broadcasted_iota(jnp.int32, (rows, n_e), 1).astype(F32)
    lane = lax.broadcasted_iota(jnp.int32, (rows, LANES), 1)
    idx_out = jnp.zeros((rows, LANES), F32)
    val_out = jnp.zeros((rows, LANES), F32)
    top0 = None
    denom = jnp.zeros((rows, 1), F32)
    work = logits
    for k in range(TOP_K):
        mx = jnp.max(work, axis=-1, keepdims=True)
        ix = jnp.min(jnp.where(work == mx, e_iota, float(n_e)), axis=-1, keepdims=True)
        if k == 0:
            top0 = mx
        ex = jnp.exp(mx - top0)
        denom = denom + ex
        idx_out = jnp.where(lane == k, ix, idx_out)
        val_out = jnp.where(lane == k, ex, val_out)
        work = jnp.where(e_iota == ix, -jnp.inf, work)
    idx_ref[...] = idx_out.T[0:SUBLANES, :].astype(jnp.int32)
    gate_ref[...] = (val_out / denom).T[0:SUBLANES, :]


def norm_modulate(xs, g, modv, shift_idx):
    b, s, d = xs.shape
    tr = CTX_LEN
    nblk = s // tr
    return pl.pallas_call(
        functools.partial(_norm_mod_kernel, shift_idx=shift_idx),
        out_shape=jax.ShapeDtypeStruct((b, s, d), BF16),
        grid=(b, nblk),
        in_specs=[
            pl.BlockSpec((None, tr, d), lambda bi, i: (bi, i, 0)),
            pl.BlockSpec((1, d), lambda bi, i: (0, 0)),
            pl.BlockSpec((None, None, N_MOD, d), lambda bi, i: (bi, jnp.minimum(i, 1), 0, 0)),
        ],
        out_specs=pl.BlockSpec((None, tr, d), lambda bi, i: (bi, i, 0)),
        compiler_params=_params(("parallel", "parallel")),
        name="norm_modulate",
    )(xs, g.reshape(1, d), modv)


def norm_modulate_route(xs, g, modv, shift_idx, w_router, b_router):
    b, s, d = xs.shape
    tr = CTX_LEN
    nblk = s // tr
    n_e = w_router.shape[1]
    w_hi = w_router.astype(BF16)
    w_lo = (w_router.astype(F32) - w_hi.astype(F32)).astype(BF16)
    return pl.pallas_call(
        functools.partial(_norm_route_kernel, shift_idx=shift_idx),
        out_shape=(jax.ShapeDtypeStruct((b, s, d), BF16),
                   jax.ShapeDtypeStruct((b, SUBLANES, s), jnp.int32),
                   jax.ShapeDtypeStruct((b, SUBLANES, s), F32)),
        grid=(b, nblk),
        in_specs=[
            pl.BlockSpec((None, tr, d), lambda bi, i: (bi, i, 0)),
            pl.BlockSpec((1, d), lambda bi, i: (0, 0)),
            pl.BlockSpec((None, None, N_MOD, d), lambda bi, i: (bi, jnp.minimum(i, 1), 0, 0)),
            pl.BlockSpec((d, n_e), lambda bi, i: (0, 0)),
            pl.BlockSpec((d, n_e), lambda bi, i: (0, 0)),
            pl.BlockSpec((1, n_e), lambda bi, i: (0, 0)),
        ],
        out_specs=(pl.BlockSpec((None, tr, d), lambda bi, i: (bi, i, 0)),
                   pl.BlockSpec((None, SUBLANES, tr), lambda bi, i: (bi, 0, i)),
                   pl.BlockSpec((None, SUBLANES, tr), lambda bi, i: (bi, 0, i))),
        compiler_params=_params(("parallel", "parallel")),
        name="norm_modulate_route",
    )(xs, g.reshape(1, d), modv, w_hi, w_lo, b_router.astype(F32).reshape(1, n_e))


def _final_norm_kernel(x_ref, g_ref, o_ref):
    x = x_ref[...]
    o_ref[...] = x * lax.rsqrt(jnp.mean(x * x, axis=-1, keepdims=True) + NORM_EPS) * g_ref[...]


def final_norm(xs, g):
    b, s, d = xs.shape
    tr = CTX_LEN
    nblk = s // tr - 1
    return pl.pallas_call(
        _final_norm_kernel,
        out_shape=jax.ShapeDtypeStruct((b, s - CTX_LEN, d), F32),
        grid=(b, nblk),
        in_specs=[pl.BlockSpec((None, tr, d), lambda bi, i: (bi, i + 1, 0)),
                  pl.BlockSpec((1, d), lambda bi, i: (0, 0))],
        out_specs=pl.BlockSpec((None, tr, d), lambda bi, i: (bi, i, 0)),
        compiler_params=_params(("parallel", "parallel")),
        name="final_norm",
    )(xs, g.reshape(1, d))


def _mm_kernel(x_ref, w_ref, o_ref, wb_sc):
    @pl.when(pl.program_id(1) == 0)
    def _():
        wb_sc[...] = w_ref[...].astype(BF16)

    o_ref[...] = jnp.dot(x_ref[...], wb_sc[...], preferred_element_type=F32).astype(o_ref.dtype)


def _mm_residual_kernel(x_ref, w_ref, r_ref, g_ref, o_ref, wb_sc, *, groups, group_rows):
    @pl.when(pl.program_id(1) == 0)
    def _():
        wb_sc[...] = w_ref[...].astype(BF16)

    acc = jnp.dot(x_ref[...], wb_sc[...], preferred_element_type=F32)
    for gi in range(groups):
        rows = slice(gi * group_rows, (gi + 1) * group_rows)
        o_ref[rows, :] = r_ref[rows, :] + g_ref[gi] * acc[rows, :]


def matmul(x, w):
    m, k = x.shape
    n = w.shape[1]
    bm = _pick(m, (1536, 768, 512, 256, 128))
    bn = _pick(n, (512, 256, 128))
    return pl.pallas_call(
        _mm_kernel,
        out_shape=jax.ShapeDtypeStruct((m, n), BF16),
        grid=(n // bn, m // bm),
        in_specs=[pl.BlockSpec((bm, k), lambda j, i: (i, 0)),
                  pl.BlockSpec((k, bn), lambda j, i: (0, j))],
        out_specs=pl.BlockSpec((bm, bn), lambda j, i: (i, j)),
        scratch_shapes=[pltpu.VMEM((k, bn), BF16)],
        compiler_params=_params(("parallel", "arbitrary")),
        name="projection",
    )(x, w)


def matmul_residual(x, w, res, gate_groups):
    m, k = x.shape
    n = w.shape[1]
    bm = _pick(m, (768, 512, 256))
    bn = _pick(n, (512, 256, 128))
    groups = bm // CTX_LEN
    return pl.pallas_call(
        functools.partial(_mm_residual_kernel, groups=groups, group_rows=CTX_LEN),
        out_shape=jax.ShapeDtypeStruct((m, n), F32),
        grid=(n // bn, m // bm),
        in_specs=[pl.BlockSpec((bm, k), lambda j, i: (i, 0)),
                  pl.BlockSpec((k, bn), lambda j, i: (0, j)),
                  pl.BlockSpec((bm, bn), lambda j, i: (i, j)),
                  pl.BlockSpec((groups, 1, bn), lambda j, i: (i, 0, j))],
        out_specs=pl.BlockSpec((bm, bn), lambda j, i: (i, j)),
        scratch_shapes=[pltpu.VMEM((k, bn), BF16)],
        input_output_aliases={2: 0},
        compiler_params=_params(("parallel", "arbitrary")),
        name="out_projection_residual",
    )(x, w, res, gate_groups)


def rope_tables(seq, head_dim):
    half = head_dim // 2
    n = half // 2
    t = jnp.arange(seq)
    rows, cols = (t // GRID_W).astype(F32), (t % GRID_W).astype(F32)
    freqs = ROPE_THETA ** (-jnp.arange(n, dtype=F32) / n)
    ang = jnp.concatenate([rows[:, None] * freqs, cols[:, None] * freqs], axis=-1)
    cos, sin = jnp.cos(ang), jnp.sin(ang)
    cos = jnp.concatenate([jnp.ones((CTX_LEN, half), F32), cos], axis=0)
    sin = jnp.concatenate([jnp.zeros((CTX_LEN, half), F32), sin], axis=0)
    return jnp.concatenate([cos, cos], axis=-1), jnp.concatenate([-sin, sin], axis=-1)


def _swap_halves(y, head_dim):
    half = head_dim // 2
    if head_dim == LANES:
        return pltpu.roll(y, half, 1)
    lane = lax.broadcasted_iota(jnp.int32, y.shape, 1)
    first = (lane % head_dim) < half
    return jnp.where(first, pltpu.roll(y, LANES - half, 1), pltpu.roll(y, half, 1))


def _qk_prep_kernel(x_ref, gain_ref, cos_ref, sin_ref, o_ref, *, head_dim, normalise, dup):
    x = x_ref[...].astype(F32)
    cols = x.shape[1]
    cos, sin = cos_ref[...], sin_ref[...]
    gain = gain_ref[...]
    for c in range(cols // LANES):
        y = x[:, c * LANES:(c + 1) * LANES]
        if normalise:
            y = y * lax.rsqrt(jnp.mean(y * y, axis=-1, keepdims=True) + NORM_EPS)
        y = y * gain
        y = (y * cos + _swap_halves(y, head_dim) * sin).astype(o_ref.dtype)
        if dup:
            lane = lax.broadcasted_iota(jnp.int32, y.shape, 1)
            other = pltpu.roll(y.astype(F32), head_dim, 1).astype(o_ref.dtype)
            o_ref[:, (2 * c) * LANES:(2 * c + 1) * LANES] = jnp.where(lane < head_dim, y, other)
            o_ref[:, (2 * c + 1) * LANES:(2 * c + 2) * LANES] = jnp.where(lane < head_dim, other, y)
        else:
            o_ref[:, c * LANES:(c + 1) * LANES] = y


def qk_prepare(qkv, col0, ncols, gain, cos, sin, head_dim, normalise, dup=False):
    b, s, _ = qkv.shape
    tr = _pick(s, (1408, 1024, 768, 512, 256))
    bc = _pick(ncols, (512, 256, 128))
    c0 = col0 // bc
    assert col0 % bc == 0
    mult = 2 if dup else 1
    return pl.pallas_call(
        functools.partial(_qk_prep_kernel, head_dim=head_dim, normalise=normalise, dup=dup),
        out_shape=jax.ShapeDtypeStruct((b, s, ncols * mult), BF16),
        grid=(b, s // tr, ncols // bc),
        in_specs=[pl.BlockSpec((None, tr, bc), lambda bi, i, j: (bi, i, c0 + j)),
                  pl.BlockSpec((1, LANES), lambda bi, i, j: (0, 0)),
                  pl.BlockSpec((tr, LANES), lambda bi, i, j: (i, 0)),
                  pl.BlockSpec((tr, LANES), lambda bi, i, j: (i, 0))],
        out_specs=pl.BlockSpec((None, tr, bc * mult), lambda bi, i, j: (bi, i, j)),
        compiler_params=_params(("parallel", "parallel", "parallel")),
        name="qk_prepare",
    )(qkv, gain, cos, sin)


def _dot_nt(a, b):
    return lax.dot_general(a, b, (((1,), (1,)), ((), ())), preferred_element_type=F32)


def _ctx_attn_kernel(q_ref, k_ref, v_ref, buf_ref, o_ref, *, scale):
    del buf_ref
    s = _dot_nt(q_ref[...], k_ref[...]) * scale
    m = jnp.max(s, axis=-1, keepdims=True)
    p = jnp.exp2(s - m)
    l = jnp.sum(p, axis=-1, keepdims=True)
    o = jnp.dot(p.astype(BF16), v_ref[...], preferred_element_type=F32) / l
    o_ref[...] = o.astype(o_ref.dtype)


def ctx_self_attention(o_buf, q_arr, q_c0, k_arr, k_c0, v_arr, v_c0, n_heads, group, scale):
    b, s, dm = o_buf.shape
    return pl.pallas_call(
        functools.partial(_ctx_attn_kernel, scale=scale),
        out_shape=jax.ShapeDtypeStruct(o_buf.shape, o_buf.dtype),
        grid=(b, n_heads),
        in_specs=[pl.BlockSpec((None, CTX_LEN, LANES), lambda bi, h: (bi, 0, q_c0 + h)),
                  pl.BlockSpec((None, CTX_LEN, LANES), lambda bi, h: (bi, 0, k_c0 + h // group)),
                  pl.BlockSpec((None, CTX_LEN, LANES), lambda bi, h: (bi, 0, v_c0 + h // group)),
                  pl.BlockSpec(memory_space=pl.ANY)],
        out_specs=pl.BlockSpec((None, CTX_LEN, LANES), lambda bi, h: (bi, 0, h)),
        input_output_aliases={3: 0},
        compiler_params=_params(("parallel", "parallel")),
        name="ctx_self_attention",
    )(q_arr, k_arr, v_arr, o_buf)


def _na_kernel(q_ref, k_ref, v_ref, b_ref, o_ref, *, scale, n_rows):
    blk = pl.program_id(2)
    base = jnp.clip(NA_ROW_BLOCK * blk - NA_KH // 2, 0, n_rows - NA_KEY_ROWS)
    start = pl.multiple_of(CTX_LEN + base * GRID_W, GRID_W)
    nwin = NA_KEY_ROWS * GRID_W
    ones_w, ones_c = jnp.ones((nwin, LANES), BF16), jnp.ones((CTX_LEN, LANES), BF16)
    for hh in range(q_ref.shape[1] // LANES):
        cols = slice(hh * LANES, (hh + 1) * LANES)
        q = q_ref[:, cols]
        s_w = _dot_nt(q, k_ref[pl.ds(start, nwin), cols]) * scale + b_ref[hh]
        s_c = _dot_nt(q, k_ref[0:CTX_LEN, cols]) * scale
        m = jnp.maximum(jnp.max(s_w, axis=-1, keepdims=True), jnp.max(s_c, axis=-1, keepdims=True))
        v_w = jnp.concatenate([v_ref[pl.ds(start, nwin), cols], ones_w], axis=1)
        v_c = jnp.concatenate([v_ref[0:CTX_LEN, cols], ones_c], axis=1)
        pv = (jnp.dot(jnp.exp2(s_w - m).astype(BF16), v_w, preferred_element_type=F32)
              + jnp.dot(jnp.exp2(s_c - m).astype(BF16), v_c, preferred_element_type=F32))
        o_ref[:, cols] = (pv[:, :LANES] / pv[:, LANES:]).astype(o_ref.dtype)


def _na_bias_tables(rpb, n_rows):
    rb, kr, w = NA_ROW_BLOCK, NA_KEY_ROWS, GRID_W
    kh = min(NA_KH, n_rows)
    nblk = n_rows // rb
    n_h, n_ri, n_ci = rpb.shape
    qc = np.arange(w)
    c0 = np.clip(qc - NA_KW // 2, 0, w - NA_KW)
    kc = np.arange(w)
    col_ok = (kc[None, :] >= c0[:, None]) & (kc[None, :] < c0[:, None] + NA_KW)
    col_idx = np.clip(kc[None, :] - qc[:, None], 1 - NA_KW, NA_KW - 1) + NA_KW - 1
    onehot = (col_idx.reshape(1, w * w) == np.arange(n_ci)[:, None]).astype(np.float32)
    by_col = jnp.dot(rpb.astype(F32).reshape(n_h * n_ri, n_ci) * LOG2E, jnp.asarray(onehot),
                     precision=lax.Precision.HIGHEST).reshape(n_h, n_ri, w, w)
    cases = []
    for blk in range(nblk):
        base = int(np.clip(rb * blk - NA_KH // 2, 0, n_rows - kr))
        r = rb * blk + np.arange(rb)
        r0 = np.clip(r - kh // 2, 0, n_rows - kh)
        krow = base + np.arange(kr)
        row_ok = (krow[None, :] >= r0[:, None]) & (krow[None, :] < r0[:, None] + kh)
        row_idx = np.clip(krow[None, :] - r[:, None] + NA_KH - 1, 0, 2 * NA_KH - 2)
        cases.append((base - rb * blk, row_ok, row_idx))
    for blk in range(2, nblk - 1):
        assert cases[blk][0] == cases[1][0] and all(np.array_equal(cases[blk][i], cases[1][i]) for i in (1, 2))
    out = []
    for blk in (0, 1, nblk - 1):
        _, row_ok, row_idx = cases[blk]
        tab = jnp.stack([jnp.stack([by_col[:, int(row_idx[a, b2])] for b2 in range(kr)], axis=2)
                         for a in range(rb)], axis=1)
        ok = row_ok[:, None, :, None] & col_ok[None, :, None, :]
        out.append(jnp.where(ok[None], tab, NEG_INF).reshape(n_h, rb * w, kr * w))
    return jnp.stack(out, axis=0)


def neighbourhood_attention(qkv, rpb, n_heads):
    b, s, _ = qkv.shape
    n_rows = (s - CTX_LEN) // GRID_W
    nblk = n_rows // NA_ROW_BLOCK
    tq = NA_ROW_BLOCK * GRID_W
    assert tq == CTX_LEN
    nwin = NA_KEY_ROWS * GRID_W
    bias = _na_bias_tables(rpb, n_rows)
    scale = LANES ** -0.5 * LOG2E

    hps = _pick(n_heads, (NA_HEADS_PER_STEP, 1))
    ng = n_heads // hps
    wide = hps * LANES

    def bias_case(blk):
        return jnp.where(blk == 0, 0, jnp.where(blk == nblk - 1, 2, 1))

    return pl.pallas_call(
        functools.partial(_na_kernel, scale=scale, n_rows=n_rows),
        out_shape=jax.ShapeDtypeStruct((b, s, n_heads * LANES), BF16),
        grid=(b, ng, nblk),
        in_specs=[pl.BlockSpec((None, tq, wide), lambda bi, h, i: (bi, i + 1, h)),
                  pl.BlockSpec((None, s, wide), lambda bi, h, i: (bi, 0, ng + h)),
                  pl.BlockSpec((None, s, wide), lambda bi, h, i: (bi, 0, 2 * ng + h)),
                  pl.BlockSpec((None, hps, tq, nwin), lambda bi, h, i: (bias_case(i), h, 0, 0))],
        out_specs=pl.BlockSpec((None, tq, wide), lambda bi, h, i: (bi, i + 1, h)),
        compiler_params=_params(("parallel", "parallel", "arbitrary")),
        name="neighbourhood_attention",
    )(qkv, qkv, qkv, bias)


def _gqa_kernel(q_ref, k_ref, v_ref, o_ref, m_sc, acc_sc, *, tk, group):
    m_sc[...] = jnp.full(m_sc.shape, -jnp.inf, F32)
    acc_sc[...] = jnp.zeros(acc_sc.shape, F32)
    ones = jnp.ones((tk, LANES), BF16)

    def body(c, carry):
        rows = pl.ds(pl.multiple_of(c * tk, tk), tk)
        kc = k_ref[rows, :]
        vc = jnp.concatenate([v_ref[rows, :], ones], axis=1)
        for g in range(group):
            s = _dot_nt(q_ref[:, g * LANES:(g + 1) * LANES], kc)
            m_prev = m_sc[g]
            m_new = jnp.maximum(m_prev, jnp.max(s, axis=-1, keepdims=True))
            alpha = jnp.exp2(m_prev - m_new)
            p = jnp.exp2(s - m_new[:, :1])
            pv = jnp.dot(p.astype(BF16), vc, preferred_element_type=F32)
            acc_sc[g] = jnp.concatenate([alpha, alpha], axis=1) * acc_sc[g] + pv
            m_sc[g] = m_new
        return carry

    lax.fori_loop(0, k_ref.shape[0] // tk, body, 0, unroll=True)
    for g in range(group):
        a = acc_sc[g]
        o_ref[:, g * LANES:(g + 1) * LANES] = (a[:, :LANES] / a[:, LANES:]).astype(o_ref.dtype)


def global_gqa_attention(qp, kp, qkv, n_heads, n_kv):
    b, s, _ = qp.shape
    group = n_heads // n_kv
    tq = CTX_LEN
    nq = s // tq - 1
    tk = _pick(s, (768, 512, 256))
    v_c0 = n_heads + n_kv
    return pl.pallas_call(
        functools.partial(_gqa_kernel, tk=tk, group=group),
        out_shape=jax.ShapeDtypeStruct((b, s, n_heads * LANES), BF16),
        grid=(b, n_kv, nq),
        in_specs=[pl.BlockSpec((None, tq, group * LANES), lambda bi, h, i: (bi, i + 1, h)),
                  pl.BlockSpec((None, s, LANES), lambda bi, h, i: (bi, 0, h)),
                  pl.BlockSpec((None, s, LANES), lambda bi, h, i: (bi, 0, v_c0 + h))],
        out_specs=pl.BlockSpec((None, tq, group * LANES), lambda bi, h, i: (bi, i + 1, h)),
        scratch_shapes=[pltpu.VMEM((group, tq, LANES), F32), pltpu.VMEM((group, tq, 2 * LANES), F32)],
        compiler_params=_params(("parallel", "parallel", "arbitrary")),
        name="global_gqa_attention",
    )(qp, kp, qkv)


def _swa_kernel(q_ref, k_ref, v_ref, sink_ref, o_ref, *, head_dim, n_lat):
    i = pl.program_id(2)
    tq = q_ref.shape[0]
    n_ctx_blocks = CTX_LEN // SWA_BLOCK
    band = SWA_BLOCK + 2 * SWA_WINDOW

    @pl.when(i < n_ctx_blocks)
    def _():
        o_ref[...] = jnp.zeros(o_ref.shape, o_ref.dtype)

    @pl.when(i >= n_ctx_blocks)
    def _():
        qb = i - n_ctx_blocks
        row0 = jnp.clip(CTX_LEN + SWA_BLOCK * qb - SWA_WINDOW, 0, k_ref.shape[0] - band)
        row0 = pl.multiple_of(row0, SWA_BLOCK)
        nk = band + CTX_LEN
        k_all = jnp.concatenate([k_ref[pl.ds(row0, band), :], k_ref[0:CTX_LEN, :]], axis=0)
        v_all = jnp.concatenate([jnp.concatenate([v_ref[pl.ds(row0, band), :], v_ref[0:CTX_LEN, :]], axis=0),
                                 jnp.ones((nk, LANES), BF16)], axis=1)
        qpos = SWA_BLOCK * qb + lax.broadcasted_iota(jnp.int32, (tq, nk), 0)
        col = lax.broadcasted_iota(jnp.int32, (tq, nk), 1)
        kpos = row0 - CTX_LEN + col
        ok = (col >= band) | ((kpos >= 0) & (kpos < n_lat) & (jnp.abs(kpos - qpos) <= SWA_WINDOW))
        hpg = LANES // head_dim
        lane = lax.broadcasted_iota(jnp.int32, (tq, LANES), 1)
        sels = [(lane >= j * head_dim) & (lane < (j + 1) * head_dim) for j in range(hpg)]
        q_all = q_ref[...]
        n_groups = q_all.shape[1] // LANES
        gpc = min(SWA_GROUPS_PER_CHAIN, n_groups)
        ok = jnp.concatenate([ok] * (hpg * gpc), axis=0)
        for c0 in range(0, n_groups, gpc):
            heads = [(c, j) for c in range(c0, c0 + gpc) for j in range(hpg)]
            qs = jnp.concatenate([jnp.where(sels[j], q_all[:, c * LANES:(c + 1) * LANES], jnp.zeros((tq, LANES), BF16))
                                  for c, j in heads], axis=0)
            sink = jnp.concatenate([jnp.broadcast_to(sink_ref[:, c * hpg + j:c * hpg + j + 1], (tq, 1))
                                    for c, j in heads], axis=0)
            s = jnp.where(ok, _dot_nt(qs, k_all), NEG_INF)
            m = jnp.maximum(jnp.max(s, axis=-1, keepdims=True), sink)
            pv = jnp.dot(jnp.exp2(s - m).astype(BF16), v_all, preferred_element_type=F32)
            o = pv[:, :LANES] / (pv[:, LANES:] + jnp.exp2(sink - m))
            for ci, c in enumerate(range(c0, c0 + gpc)):
                res = o[ci * hpg * tq:(ci * hpg + 1) * tq]
                for j in range(1, hpg):
                    res = jnp.where(sels[j], o[(ci * hpg + j) * tq:(ci * hpg + j + 1) * tq], res)
                o_ref[:, c * LANES:(c + 1) * LANES] = res.astype(o_ref.dtype)


def window_gqa_attention(q, k2, v2, sink, n_heads, n_kv, head_dim):
    b, s, _ = q.shape
    group = n_heads // n_kv
    gcols = group * head_dim
    assert gcols % LANES == 0 and LANES % head_dim == 0
    tq = SWA_BLOCK
    return pl.pallas_call(
        functools.partial(_swa_kernel, head_dim=head_dim, n_lat=s - CTX_LEN),
        out_shape=jax.ShapeDtypeStruct((b, s, n_heads * head_dim), BF16),
        grid=(b, n_kv, s // tq),
        in_specs=[pl.BlockSpec((None, tq, gcols), lambda bi, h, i: (bi, i, h)),
                  pl.BlockSpec((None, s, LANES), lambda bi, h, i: (bi, 0, h)),
                  pl.BlockSpec((None, s, LANES), lambda bi, h, i: (bi, 0, h)),
                  pl.BlockSpec((None, 1, group), lambda bi, h, i: (h, 0, 0))],
        out_specs=pl.BlockSpec((None, tq, gcols), lambda bi, h, i: (bi, i, h)),
        compiler_params=_params(("parallel", "parallel", "arbitrary")),
        name="window_gqa_attention",
    )(q, k2, v2, sink.astype(F32).reshape(n_kv, 1, group))


def _retention_kernel(qf_ref, kf_ref, vf_ref, cosf_ref, sinf_ref, qb_ref, kb_ref, vb_ref, cosb_ref, sinb_ref,
                      intra_ref, qdec_ref, kdec_ref, cdec_ref, of_ref, ob_ref, state_sc, *, k_scale, heads, dh):
    @pl.when(pl.program_id(2) == 0)
    def _():
        state_sc[...] = jnp.zeros(state_sc.shape, F32)

    half = dh // 2

    def rope(x, cos, sin):
        swapped = jnp.concatenate([x[:, half:], x[:, :half]], axis=1)
        return x * cos + swapped * sin

    dirs = ((qf_ref, kf_ref, vf_ref, cosf_ref, sinf_ref, of_ref), (qb_ref, kb_ref, vb_ref, cosb_ref, sinb_ref, ob_ref))
    for dr, (q_ref, k_ref, v_ref, cos_ref, sin_ref, o_ref) in enumerate(dirs):
        cos, sin = cos_ref[...], sin_ref[...]
        for hh in range(heads):
            cols = slice(hh * dh, (hh + 1) * dh)
            q = rope(q_ref[:, cols].astype(F32), cos, sin)
            k = rope(k_ref[:, cols].astype(F32), cos, sin) * k_scale
            v = v_ref[:, cols]
            state = state_sc[dr, hh]
            att = _dot_nt(q.astype(BF16), k.astype(BF16)) * intra_ref[dr, hh]
            y = (jnp.dot(att.astype(BF16), v, preferred_element_type=F32)
                 + jnp.dot((q * qdec_ref[dr, hh]).astype(BF16), state.astype(BF16), preferred_element_type=F32))
            kd = (k * kdec_ref[dr, hh]).astype(BF16)
            state_sc[dr, hh] = state * cdec_ref[dr, hh] + lax.dot_general(kd, v, (((0,), (0,)), ((), ())),
                                                                         preferred_element_type=F32)
            y = y * lax.rsqrt(jnp.mean(y * y, axis=-1, keepdims=True) + NORM_EPS)
            o_ref[:, cols] = y.astype(o_ref.dtype)


def retention_scan(proj, decay_exp, cos, sin, n_heads):
    b, s, five_d = proj.shape
    d = five_d // 5
    dh = d // n_heads
    c = RET_CHUNK
    n_chunks = s // c
    n_ctx_chunks = CTX_LEN // c
    hps = min(RET_HEADS_PER_STEP, n_heads)
    lg = jnp.log1p(-jnp.exp2(-decay_exp.astype(F32)))
    pos = jnp.arange(c, dtype=F32)
    diff = pos[:, None] - pos[None, :]
    lgf, lgb = lg[0][:, None, None], lg[1][:, None, None]
    intra_f = jnp.where(diff >= 0, jnp.exp(lgf * jnp.maximum(diff, 0.0)), 0.0)
    intra_b = jnp.where(diff <= 0, jnp.exp(lgb * jnp.maximum(-diff, 0.0)), 0.0)
    intra = jnp.stack([intra_f, intra_b])
    qdec = jnp.stack([jnp.exp(lg[0][:, None] * (pos + 1.0)), jnp.exp(lg[1][:, None] * (c - pos))])[..., None]
    kdec = jnp.stack([jnp.exp(lg[0][:, None] * (c - 1.0 - pos)), jnp.exp(lg[1][:, None] * pos)])[..., None]
    cdec = jnp.exp(lg * c)[..., None, None]
    ncb = n_heads // hps

    def bwd_chunk(t):
        return jnp.where(t < n_ctx_chunks, n_ctx_chunks - 1 - t, n_chunks - 1 - (t - n_ctx_chunks))

    fwd = lambda off: (lambda bi, h, t: (bi, t, off * ncb + h))
    bwd = lambda off: (lambda bi, h, t: (bi, bwd_chunk(t), off * ncb + h))
    tab = lambda bi, h, t: (0, h, 0, 0)
    blk = (None, c, hps * dh)
    return pl.pallas_call(
        functools.partial(_retention_kernel, k_scale=dh ** -0.5, heads=hps, dh=dh),
        out_shape=(jax.ShapeDtypeStruct((b, s, d), BF16), jax.ShapeDtypeStruct((b, s, d), BF16)),
        grid=(b, ncb, n_chunks),
        in_specs=[pl.BlockSpec(blk, fwd(0)), pl.BlockSpec(blk, fwd(1)), pl.BlockSpec(blk, fwd(2)),
                  pl.BlockSpec((c, dh), lambda bi, h, t: (t, 0)), pl.BlockSpec((c, dh), lambda bi, h, t: (t, 0)),
                  pl.BlockSpec(blk, bwd(0)), pl.BlockSpec(blk, bwd(1)), pl.BlockSpec(blk, bwd(2)),
                  pl.BlockSpec((c, dh), lambda bi, h, t: (bwd_chunk(t), 0)),
                  pl.BlockSpec((c, dh), lambda bi, h, t: (bwd_chunk(t), 0)),
                  pl.BlockSpec((2, hps, c, c), tab),
                  pl.BlockSpec((2, hps, c, 1), tab),
                  pl.BlockSpec((2, hps, c, 1), tab),
                  pl.BlockSpec((2, hps, 1, 1), tab)],
        out_specs=(pl.BlockSpec(blk, lambda bi, h, t: (bi, t, h)),
                   pl.BlockSpec(blk, lambda bi, h, t: (bi, bwd_chunk(t), h))),
        scratch_shapes=[pltpu.VMEM((2, hps, dh, dh), F32)],
        compiler_params=_params(("parallel", "parallel", "arbitrary")),
        name="retention_scan",
    )(proj, proj, proj, cos, sin, proj, proj, proj, cos, sin, intra, qdec, kdec, cdec)


def _ret_merge_kernel(yf_ref, yb_ref, gf_ref, gb_ref, o_ref):
    gf = gf_ref[...].astype(F32)
    gb = gb_ref[...].astype(F32)
    o = gf * jax.nn.sigmoid(gf) * yf_ref[...].astype(F32) + gb * jax.nn.sigmoid(gb) * yb_ref[...].astype(F32)
    o_ref[...] = o.astype(o_ref.dtype)


def retention_merge(y_f, y_b, proj):
    b, s, d = y_f.shape
    tr = _pick(s, (1408, 1024, 768, 512, 256))
    bc = _pick(d, (1024, 512, 256, 128))
    nc = d // bc
    return pl.pallas_call(
        _ret_merge_kernel,
        out_shape=jax.ShapeDtypeStruct((b, s, d), BF16),
        grid=(b, s // tr, nc),
        in_specs=[pl.BlockSpec((None, tr, bc), lambda bi, i, j: (bi, i, j)),
                  pl.BlockSpec((None, tr, bc), lambda bi, i, j: (bi, i, j)),
                  pl.BlockSpec((None, tr, bc), lambda bi, i, j: (bi, i, 3 * nc + j)),
                  pl.BlockSpec((None, tr, bc), lambda bi, i, j: (bi, i, 4 * nc + j))],
        out_specs=pl.BlockSpec((None, tr, bc), lambda bi, i, j: (bi, i, j)),
        compiler_params=_params(("parallel", "parallel", "parallel")),
        name="retention_merge",
    )(y_f, y_b, proj, proj)


def _moe_kernel(be_ref, first_ref, nused_ref, x_ref, w1_ref, perm_ref, b1_ref, w2_ref, b2_ref, g_ref, o_ref,
                w1p_sc, w2p_sc):
    i = pl.program_id(0)
    f = w2_ref.shape[0]
    fp = w2p_sc.shape[0]

    @pl.when(i == 0)
    def _():
        w2p_sc[...] = jnp.zeros(w2p_sc.shape, w2p_sc.dtype)

    @pl.when(first_ref[i] == 1)
    def _():
        w1p_sc[...] = jnp.dot(perm_ref[...], w1_ref[...].astype(BF16), preferred_element_type=F32).astype(BF16)
        w2p_sc[0:f, :] = w2_ref[...].astype(BF16)

    @pl.when(i < nused_ref[0])
    def _():
        u = _dot_nt(x_ref[...], w1p_sc[...]) + b1_ref[...]
        glu = jnp.minimum(u[:, :fp], SWIGLU_LIMIT)
        lin = jnp.clip(u[:, fp:], -SWIGLU_LIMIT, SWIGLU_LIMIT)
        a = glu * jax.nn.sigmoid(SWIGLU_ALPHA * glu) * (lin + 1.0)
        y = jnp.dot(a.astype(BF16), w2p_sc[...], preferred_element_type=F32) + b2_ref[...]
        o_ref[...] = (y * g_ref[...]).astype(o_ref.dtype)

    @pl.when(i >= nused_ref[0])
    def _():
        o_ref[...] = jnp.zeros(o_ref.shape, o_ref.dtype)


def moe_experts(xg, blk_expert, blk_first, n_used, slot_gate, layer, w1, b1, w2, b2):
    n_slots, d = xg.shape
    _, n_e, _, f2 = w1.shape
    f = f2 // 2
    fp = -(-f // LANES) * LANES
    n_blocks = n_slots // MOE_BLOCK
    w1t = jnp.swapaxes(w1, 2, 3)
    perm = np.zeros((2 * fp, f2), np.float32)
    perm[np.arange(f), 2 * np.arange(f)] = 1.0
    perm[fp + np.arange(f), 2 * np.arange(f) + 1] = 1.0
    b1l = b1[layer]
    b1p = jnp.zeros((n_e, 1, 2 * fp), F32).at[:, 0, :f].set(b1l[:, 0::2]).at[:, 0, fp:fp + f].set(b1l[:, 1::2])

    def xrow(i, be, first, nused):
        return (jnp.minimum(i, jnp.maximum(nused[0] - 1, 0)), 0)

    def expert(i, be, first, nused):
        return (be[i], 0, 0)

    def layer_expert(i, be, first, nused):
        return (layer, be[i], 0, 0)

    grid_spec = pltpu.PrefetchScalarGridSpec(
        num_scalar_prefetch=3,
        grid=(n_blocks,),
        in_specs=[pl.BlockSpec((MOE_BLOCK, d), xrow),
                  pl.BlockSpec((None, None, f2, d), layer_expert),
                  pl.BlockSpec((2 * fp, f2), lambda i, be, first, nused: (0, 0)),
                  pl.BlockSpec((None, 1, 2 * fp), expert),
                  pl.BlockSpec((None, None, f, d), layer_expert),
                  pl.BlockSpec((None, 1, d), expert),
                  pl.BlockSpec((MOE_BLOCK, 1), xrow)],
        out_specs=pl.BlockSpec((MOE_BLOCK, d), lambda i, be, first, nused: (i, 0)),
        scratch_shapes=[pltpu.VMEM((2 * fp, d), BF16), pltpu.VMEM((fp, d), BF16)],
    )
    return pl.pallas_call(
        _moe_kernel,
        out_shape=jax.ShapeDtypeStruct((n_slots, d), BF16),
        grid_spec=grid_spec,
        compiler_params=_params(("arbitrary",)),
        name="moe_experts",
    )(blk_expert, blk_first, n_used, xg, w1t, jnp.asarray(perm, BF16), b1p, w2,
      b2[layer].astype(F32).reshape(n_e, 1, d), slot_gate)


def _moe_combine_kernel(x_ref, g_ref, y0_ref, y1_ref, y2_ref, y3_ref, o_ref):
    y = (y0_ref[...].astype(F32) + y1_ref[...].astype(F32)) + (y2_ref[...].astype(F32) + y3_ref[...].astype(F32))
    o_ref[...] = x_ref[...] + g_ref[0] * y


def moe_combine_residual(xs2, gate_groups, ysg, tok0):
    d = xs2.shape[1]
    t_part = ysg.shape[0] // TOP_K
    assert TOP_K == 4
    tr = CTX_LEN
    bn = _pick(d, (2048, 1024, 512, 256, 128))
    nc = d // bn
    nr = t_part // tr
    r0 = tok0 // tr
    yspec = [pl.BlockSpec((tr, bn), functools.partial(lambda i, j, k: (k * nr + i, j), k=k)) for k in range(TOP_K)]
    return pl.pallas_call(
        _moe_combine_kernel,
        out_shape=jax.ShapeDtypeStruct(xs2.shape, F32),
        grid=(nr, nc),
        in_specs=[pl.BlockSpec((tr, bn), lambda i, j: (r0 + i, j)),
                  pl.BlockSpec((1, 1, bn), lambda i, j: (r0 + i, 0, j))] + yspec,
        out_specs=pl.BlockSpec((tr, bn), lambda i, j: (r0 + i, j)),
        input_output_aliases={0: 0},
        compiler_params=_params(("parallel", "parallel")),
        name="moe_combine_residual",
    )(xs2, gate_groups, ysg, ysg, ysg, ysg)


def moe_layer(xs2, gate_groups, h, top_idx, top_gate, layer, w1, b1, w2, b2):
    t_all = h.shape[0]
    parts = MOE_TOKEN_PARTS if t_all % (MOE_TOKEN_PARTS * CTX_LEN) == 0 else 1
    t_part = t_all // parts
    for part in range(parts):
        cols = slice(part * t_part, (part + 1) * t_part)
        xs2 = _moe_part(xs2, gate_groups, h, top_idx[:, cols], top_gate[:, cols], part * t_part, layer,
                        w1, b1, w2, b2)
    return xs2


def _moe_part(xs2, gate_groups, h, top_idx, top_gate, tok0, layer, w1, b1, w2, b2):
    t_tok = top_idx.shape[1]
    n_e = w1.shape[1]
    n_assign = t_tok * TOP_K
    n_blocks = -(-n_assign // MOE_BLOCK) + n_e
    n_slots = n_blocks * MOE_BLOCK
    n_extra = n_slots - n_assign
    e_flat = top_idx.reshape(n_assign)
    counts = jnp.sum((e_flat[:, None] == jnp.arange(n_e, dtype=jnp.int32)[None, :]).astype(jnp.int32), axis=0)
    padded = (counts + MOE_BLOCK - 1) // MOE_BLOCK * MOE_BLOCK
    n_used = (jnp.sum(padded) // MOE_BLOCK).astype(jnp.int32)
    pad_cum = jnp.cumsum(padded - counts)
    pad_key = jnp.sum((jnp.arange(n_extra, dtype=jnp.int32)[:, None] >= pad_cum[None, :]).astype(jnp.int32), axis=1)
    keys = jnp.concatenate([e_flat, pad_key])
    ids = jnp.arange(n_slots, dtype=jnp.int32)
    tok = jnp.concatenate([jnp.tile(jnp.arange(t_tok, dtype=jnp.int32), TOP_K),
                           jnp.arange(n_extra, dtype=jnp.int32) % t_tok])
    gate = jnp.concatenate([top_gate.reshape(n_assign), jnp.zeros((n_extra,), F32)])
    slot_key, slot_id, slot_tok, slot_gate = lax.sort((keys, ids, tok, gate), dimension=0, is_stable=True, num_keys=1)
    _, slot_of = lax.sort_key_val(slot_id, ids)
    blk_expert = jnp.minimum(slot_key.reshape(n_blocks, MOE_BLOCK)[:, 0], n_e - 1)
    prev = jnp.concatenate([jnp.full((1,), -1, jnp.int32), blk_expert[:-1]])
    blk_first = ((blk_expert != prev) & (jnp.arange(n_blocks) < n_used)).astype(jnp.int32)
    xg = h.at[slot_tok + tok0].get(mode="promise_in_bounds")
    ys = moe_experts(xg, blk_expert, blk_first, n_used.reshape(1), slot_gate.reshape(n_slots, 1), layer,
                     w1, b1, w2, b2)
    ysg = ys.at[slot_of[:n_assign]].get(mode="promise_in_bounds")
    return moe_combine_residual(xs2, gate_groups, ysg, tok0)


def _group_rows(vec_ctx, vec_lat, s):
    b, d = vec_lat.shape
    ng = s // CTX_LEN
    g = jnp.concatenate([jnp.broadcast_to(vec_ctx[None, None, :], (b, 1, d)),
                         jnp.broadcast_to(vec_lat[:, None, :], (b, ng - 1, d))], axis=1)
    return g.reshape(b * ng, 1, d)


def kernel(x, c, ctx, c_ctx, w_mod, b_mod, g_mix, g_ffn, g_final, na_w_qkv, na_w_o, na_rpb, ret_w_in, ret_w_o, ret_decay_exp, gqa_w_qkv, gqa_w_o, gqa_q_gain, gqa_k_gain, swa_w_qkv, swa_w_o, swa_sink, moe_w_router, moe_b_router, moe_w1, moe_b1, moe_w2, moe_b2):
    b, l, d = x.shape
    assert ctx.shape[1] == CTX_LEN and l % CTX_LEN == 0
    s = CTX_LEN + l
    depth = w_mod.shape[0]
    xs = jnp.concatenate([ctx, x], axis=1)

    n_rows = -(-(b + 1) // 16) * 16
    s_rows = jnp.zeros((n_rows, d), F32).at[:b].set(jax.nn.silu(c)).at[b].set(jax.nn.silu(c_ctx))
    mod = modulation_all_layers(s_rows, w_mod, b_mod).reshape(depth, n_rows, N_MOD, d)

    ones = jnp.ones((1, LANES), F32)
    for li in range(depth):
        mix, j = li % N_MIXERS, li // N_MIXERS
        need_ctx = li < depth - 1
        mod_lat, mod_ctx = mod[li, :b], mod[li, b]
        modv = jnp.stack([jnp.broadcast_to(mod_ctx[None], (b, N_MOD, d)), mod_lat], axis=1)
        h = norm_modulate(xs, g_mix[li], modv, 0)
        h2 = h.reshape(b * s, d)
        if mix == 0:
            qkv = matmul(h2, na_w_qkv[j]).reshape(b, s, 3 * d)
            o = neighbourhood_attention(qkv, na_rpb[j], NA_HEADS)
            o = ctx_self_attention(o, qkv, 0, qkv, NA_HEADS, qkv, 2 * NA_HEADS, NA_HEADS, 1, LANES ** -0.5 * LOG2E)
            w_o = na_w_o[j]
        elif mix == 1:
            proj = matmul(h2, ret_w_in[j]).reshape(b, s, 5 * d)
            cos, sin = rope_tables(l, d // RET_HEADS)
            y_f, y_b = retention_scan(proj, ret_decay_exp[j], cos, sin, RET_HEADS)
            o = retention_merge(y_f, y_b, proj)
            w_o = ret_w_o[j]
        elif mix == 2:
            dh = d // GQA_HEADS
            assert dh == LANES
            qkv = matmul(h2, gqa_w_qkv[j]).reshape(b, s, -1)
            cos, sin = rope_tables(l, dh)
            qg = (gqa_q_gain[j].astype(F32) * (dh ** -0.5 * LOG2E)).reshape(1, LANES)
            kg = gqa_k_gain[j].astype(F32).reshape(1, LANES)
            qp = qk_prepare(qkv, 0, GQA_HEADS * dh, qg, cos, sin, dh, True)
            kp = qk_prepare(qkv, GQA_HEADS * dh, GQA_KV_HEADS * dh, kg, cos, sin, dh, True)
            o = global_gqa_attention(qp, kp, qkv, GQA_HEADS, GQA_KV_HEADS)
            o = ctx_self_attention(o, qp, 0, kp, 0, qkv, GQA_HEADS + GQA_KV_HEADS, GQA_HEADS,
                                   GQA_HEADS // GQA_KV_HEADS, 1.0)
            w_o = gqa_w_o[j]
        else:
            dh = d // SWA_HEADS
            qkv = matmul(h2, swa_w_qkv[j]).reshape(b, s, -1)
            cos, sin = rope_tables(l, dh)
            reps = LANES // dh
            cos, sin = jnp.tile(cos, (1, reps)), jnp.tile(sin, (1, reps))
            unit = jnp.zeros((s, LANES), F32)
            qp = qk_prepare(qkv, 0, SWA_HEADS * dh, ones * (dh ** -0.5 * LOG2E), cos, sin, dh, False)
            k2 = qk_prepare(qkv, SWA_HEADS * dh, SWA_KV_HEADS * dh, ones, cos, sin, dh, False, dup=True)
            v2 = qk_prepare(qkv, (SWA_HEADS + SWA_KV_HEADS) * dh, SWA_KV_HEADS * dh, ones,
                            jnp.ones((s, LANES), F32), unit, dh, False, dup=True)
            o = window_gqa_attention(qp, k2, v2, swa_sink[j].astype(F32) * LOG2E, SWA_HEADS, SWA_KV_HEADS, dh)
            w_o = swa_w_o[j]
        if mix in (0, 2) and not need_ctx:
            o = o.at[:, :CTX_LEN].set(0)
        xs = matmul_residual(o.reshape(b * s, d), w_o, xs.reshape(b * s, d),
                             _group_rows(mod_ctx[2], mod_lat[:, 2], s)).reshape(b, s, d)

        h, top_idx, top_gate = norm_modulate_route(xs, g_ffn[li], modv, 3, moe_w_router[li], moe_b_router[li])
        top_idx = top_idx[:, :TOP_K].transpose(1, 0, 2).reshape(TOP_K, b * s)
        top_gate = top_gate[:, :TOP_K].transpose(1, 0, 2).reshape(TOP_K, b * s)
        xs = moe_layer(xs.reshape(b * s, d), _group_rows(mod_ctx[5], mod_lat[:, 5], s), h.reshape(b * s, d),
                       top_idx, top_gate, li, moe_w1, moe_b1, moe_w2, moe_b2).reshape(b, s, d)
    return final_norm(xs, g_final)
```

```python
import functools

import jax
import jax.numpy as jnp
import numpy as np
from jax import lax
from jax.experimental import pallas as pl
from jax.experimental.pallas import tpu as pltpu

DEPTH = 4
GRID_W = 64
CTX_LEN = 256
N_MIXERS = 4
N_MOD = 6
NORM_EPS = 1e-6
NEG_INF = -1e30
ROPE_THETA = 10000.0

NA_HEADS = 32
NA_KH = 8
NA_KW = 16
NA_ROW_BLOCK = 4
NA_KEY_ROWS = 12
NA_HEADS_PER_STEP = 4

RET_HEADS = 16
RET_CHUNK = 128
RET_HEADS_PER_STEP = 4

GQA_HEADS = 32
GQA_KV_HEADS = 8

SWA_HEADS = 64
SWA_KV_HEADS = 8
SWA_WINDOW = 128
SWA_BLOCK = 128
SWA_BLOCKS_PER_STEP = 2
SWA_GROUPS_PER_CHAIN = 2

N_EXPERTS = 32
TOP_K = 4
SWIGLU_ALPHA = 1.702
SWIGLU_LIMIT = 7.0
MOE_BLOCK = 256

LOG2E = 1.4426950408889634
LANES = 128
SUBLANES = 8
VMEM_LIMIT_BYTES = 56 * 1024 * 1024

F32 = jnp.float32
BF16 = jnp.bfloat16


def _params(semantics, vmem=VMEM_LIMIT_BYTES):
    return pltpu.CompilerParams(dimension_semantics=semantics, vmem_limit_bytes=vmem)


def _pick(n, candidates):
    for c in candidates:
        if n % c == 0:
            return c
    return n


def _mod_kernel(s_ref, w_ref, b_ref, o_ref):
    acc = jnp.dot(s_ref[...], w_ref[...].astype(BF16), preferred_element_type=F32)
    o_ref[...] = acc + b_ref[...]


def modulation_all_layers(s_rows, w_mod, b_mod):
    depth, d, n = w_mod.shape
    rows = s_rows.shape[0]
    bn = _pick(n, (512, 256, 128))
    return pl.pallas_call(
        _mod_kernel,
        out_shape=jax.ShapeDtypeStruct((depth, rows, n), F32),
        grid=(depth, n // bn),
        in_specs=[
            pl.BlockSpec((rows, d), lambda l, j: (0, 0)),
            pl.BlockSpec((None, d, bn), lambda l, j: (l, 0, j)),
            pl.BlockSpec((None, 1, bn), lambda l, j: (l, 0, j)),
        ],
        out_specs=pl.BlockSpec((None, rows, bn), lambda l, j: (l, 0, j)),
        compiler_params=_params(("parallel", "parallel")),
        name="modulation",
    )(s_rows.astype(BF16), w_mod, b_mod.reshape(depth, 1, n))


def _rms_mod(x, g, shift, scale):
    y = x * lax.rsqrt(jnp.mean(x * x, axis=-1, keepdims=True) + NORM_EPS) * g
    return y * (1.0 + scale) + shift


def _norm_mod_kernel(x_ref, g_ref, m_ref, o_ref, *, shift_idx):
    h = _rms_mod(x_ref[...], g_ref[...], m_ref[shift_idx:shift_idx + 1, :], m_ref[shift_idx + 1:shift_idx + 2, :])
    o_ref[...] = h.astype(o_ref.dtype)


def _norm_route_kernel(x_ref, g_ref, m_ref, wh_ref, wl_ref, br_ref, o_ref, idx_ref, gate_ref, *, shift_idx):
    h = _rms_mod(x_ref[...], g_ref[...], m_ref[shift_idx:shift_idx + 1, :], m_ref[shift_idx + 1:shift_idx + 2, :])
    h_hi = h.astype(BF16)
    o_ref[...] = h_hi.astype(o_ref.dtype)
    h_lo = (h - h_hi.astype(F32)).astype(BF16)
    logits = (jnp.dot(h_hi, wh_ref[...], preferred_element_type=F32)
              + (jnp.dot(h_lo, wh_ref[...], preferred_element_type=F32)
                 + jnp.dot(h_hi, wl_ref[...], preferred_element_type=F32))) + br_ref[...]
    rows, n_e = logits.shape
    e_iota = lax.broadcasted_iota(jnp.int32, (rows, n_e), 1).astype(F32)
    lane = lax.broadcasted_iota(jnp.int32, (rows, LANES), 1)
    idx_out = jnp.zeros((rows, LANES), F32)
    val_out = jnp.zeros((rows, LANES), F32)
    top0 = None
    denom = jnp.zeros((rows, 1), F32)
    work = logits
    for k in range(TOP_K):
        mx = jnp.max(work, axis=-1, keepdims=True)
        ix = jnp.min(jnp.where(work == mx, e_iota, float(n_e)), axis=-1, keepdims=True)
        if k == 0:
            top0 = mx
        ex = jnp.exp(mx - top0)
        denom = denom + ex
        idx_out = jnp.where(lane == k, ix, idx_out)
        val_out = jnp.where(lane == k, ex, val_out)
        work = jnp.where(e_iota == ix, -jnp.inf, work)
    idx_ref[...] = idx_out.T[0:SUBLANES, :].astype(jnp.int32)
    gate_ref[...] = (val_out / denom).T[0:SUBLANES, :]


def norm_modulate(xs, g, modv, shift_idx):
    b, s, d = xs.shape
    tr = CTX_LEN
    nblk = s // tr
    return pl.pallas_call(
        functools.partial(_norm_mod_kernel, shift_idx=shift_idx),
        out_shape=jax.ShapeDtypeStruct((b, s, d), BF16),
        grid=(b, nblk),
        in_specs=[
            pl.BlockSpec((None, tr, d), lambda bi, i: (bi, i, 0)),
            pl.BlockSpec((1, d), lambda bi, i: (0, 0)),
            pl.BlockSpec((None, None, N_MOD, d), lambda bi, i: (bi, jnp.minimum(i, 1), 0, 0)),
        ],
        out_specs=pl.BlockSpec((None, tr, d), lambda bi, i: (bi, i, 0)),
        compiler_params=_params(("parallel", "parallel")),
        name="norm_modulate",
    )(xs, g.reshape(1, d), modv)


def norm_modulate_route(xs, g, modv, shift_idx, w_router, b_router):
    b, s, d = xs.shape
    tr = CTX_LEN
    nblk = s // tr
    n_e = w_router.shape[1]
    w_hi = w_router.astype(BF16)
    w_lo = (w_router.astype(F32) - w_hi.astype(F32)).astype(BF16)
    return pl.pallas_call(
        functools.partial(_norm_route_kernel, shift_idx=shift_idx),
        out_shape=(jax.ShapeDtypeStruct((b, s, d), BF16),
                   jax.ShapeDtypeStruct((b, SUBLANES, s), jnp.int32),
                   jax.ShapeDtypeStruct((b, SUBLANES, s), F32)),
        grid=(b, nblk),
        in_specs=[
            pl.BlockSpec((None, tr, d), lambda bi, i: (bi, i, 0)),
            pl.BlockSpec((1, d), lambda bi, i: (0, 0)),
            pl.BlockSpec((None, None, N_MOD, d), lambda bi, i: (bi, jnp.minimum(i, 1), 0, 0)),
            pl.BlockSpec((d, n_e), lambda bi, i: (0, 0)),
            pl.BlockSpec((d, n_e), lambda bi, i: (0, 0)),
            pl.BlockSpec((1, n_e), lambda bi, i: (0, 0)),
        ],
        out_specs=(pl.BlockSpec((None, tr, d), lambda bi, i: (bi, i, 0)),
                   pl.BlockSpec((None, SUBLANES, tr), lambda bi, i: (bi, 0, i)),
                   pl.BlockSpec((None, SUBLANES, tr), lambda bi, i: (bi, 0, i))),
        compiler_params=_params(("parallel", "parallel")),
        name="norm_modulate_route",
    )(xs, g.reshape(1, d), modv, w_hi, w_lo, b_router.astype(F32).reshape(1, n_e))


def _mm_kernel(x_ref, w_ref, o_ref, wb_sc):
    @pl.when(pl.program_id(1) == 0)
    def _():
        wb_sc[...] = w_ref[...].astype(BF16)

    o_ref[...] = jnp.dot(x_ref[...], wb_sc[...], preferred_element_type=F32).astype(o_ref.dtype)


def _mm_residual_kernel(x_ref, w_ref, r_ref, g_ref, o_ref, wb_sc, *, groups, group_rows):
    @pl.when(pl.program_id(1) == 0)
    def _():
        wb_sc[...] = w_ref[...].astype(BF16)

    acc = jnp.dot(x_ref[...], wb_sc[...], preferred_element_type=F32)
    for gi in range(groups):
        rows = slice(gi * group_rows, (gi + 1) * group_rows)
        o_ref[rows, :] = r_ref[rows, :] + g_ref[gi] * acc[rows, :]


def matmul(x, w):
    m, k = x.shape
    n = w.shape[1]
    bm = _pick(m, (1536, 768, 512, 256, 128))
    bn = _pick(n, (512, 256, 128))
    return pl.pallas_call(
        _mm_kernel,
        out_shape=jax.ShapeDtypeStruct((m, n), BF16),
        grid=(n // bn, m // bm),
        in_specs=[pl.BlockSpec((bm, k), lambda j, i: (i, 0)),
                  pl.BlockSpec((k, bn), lambda j, i: (0, j))],
        out_specs=pl.BlockSpec((bm, bn), lambda j, i: (i, j)),
        scratch_shapes=[pltpu.VMEM((k, bn), BF16)],
        compiler_params=_params(("parallel", "arbitrary")),
        name="projection",
    )(x, w)


def matmul_residual(x, w, res, gate_groups):
    m, k = x.shape
    n = w.shape[1]
    bm = _pick(m, (768, 512, 256))
    bn = _pick(n, (512, 256, 128))
    groups = bm // CTX_LEN
    return pl.pallas_call(
        functools.partial(_mm_residual_kernel, groups=groups, group_rows=CTX_LEN),
        out_shape=jax.ShapeDtypeStruct((m, n), F32),
        grid=(n // bn, m // bm),
        in_specs=[pl.BlockSpec((bm, k), lambda j, i: (i, 0)),
                  pl.BlockSpec((k, bn), lambda j, i: (0, j)),
                  pl.BlockSpec((bm, bn), lambda j, i: (i, j)),
                  pl.BlockSpec((groups, 1, bn), lambda j, i: (i, 0, j))],
        out_specs=pl.BlockSpec((bm, bn), lambda j, i: (i, j)),
        scratch_shapes=[pltpu.VMEM((k, bn), BF16)],
        input_output_aliases={2: 0},
        compiler_params=_params(("parallel", "arbitrary")),
        name="out_projection_residual",
    )(x, w, res, gate_groups)


def rope_tables(seq, head_dim):
    half = head_dim // 2
    n = half // 2
    t = jnp.arange(seq)
    rows, cols = (t // GRID_W).astype(F32), (t % GRID_W).astype(F32)
    freqs = ROPE_THETA ** (-jnp.arange(n, dtype=F32) / n)
    ang = jnp.concatenate([rows[:, None] * freqs, cols[:, None] * freqs], axis=-1)
    cos, sin = jnp.cos(ang), jnp.sin(ang)
    cos = jnp.concatenate([jnp.ones((CTX_LEN, half), F32), cos], axis=0)
    sin = jnp.concatenate([jnp.zeros((CTX_LEN, half), F32), sin], axis=0)
    return jnp.concatenate([cos, cos], axis=-1), jnp.concatenate([-sin, sin], axis=-1)


def _swap_halves(y, head_dim):
    half = head_dim // 2
    if head_dim == LANES:
        return pltpu.roll(y, half, 1)
    lane = lax.broadcasted_iota(jnp.int32, y.shape, 1)
    first = (lane % head_dim) < half
    return jnp.where(first, pltpu.roll(y, LANES - half, 1), pltpu.roll(y, half, 1))


def _qk_prep_kernel(x_ref, gain_ref, cos_ref, sin_ref, o_ref, *, head_dim, normalise, dup):
    x = x_ref[...].astype(F32)
    cols = x.shape[1]
    cos, sin = cos_ref[...], sin_ref[...]
    gain = gain_ref[...]
    for c in range(cols // LANES):
        y = x[:, c * LANES:(c + 1) * LANES]
        if normalise:
            y = y * lax.rsqrt(jnp.mean(y * y, axis=-1, keepdims=True) + NORM_EPS)
        y = y * gain
        y = (y * cos + _swap_halves(y, head_dim) * sin).astype(o_ref.dtype)
        if dup:
            lane = lax.broadcasted_iota(jnp.int32, y.shape, 1)
            other = pltpu.roll(y.astype(F32), head_dim, 1).astype(o_ref.dtype)
            o_ref[:, (2 * c) * LANES:(2 * c + 1) * LANES] = jnp.where(lane < head_dim, y, other)
            o_ref[:, (2 * c + 1) * LANES:(2 * c + 2) * LANES] = jnp.where(lane < head_dim, other, y)
        else:
            o_ref[:, c * LANES:(c + 1) * LANES] = y


def qk_prepare(qkv, col0, ncols, gain, cos, sin, head_dim, normalise, dup=False):
    b, s, _ = qkv.shape
    tr = _pick(s, (1408, 1024, 768, 512, 256))
    bc = _pick(ncols, (512, 256, 128))
    c0 = col0 // bc
    assert col0 % bc == 0
    mult = 2 if dup else 1
    return pl.pallas_call(
        functools.partial(_qk_prep_kernel, head_dim=head_dim, normalise=normalise, dup=dup),
        out_shape=jax.ShapeDtypeStruct((b, s, ncols * mult), BF16),
        grid=(b, s // tr, ncols // bc),
        in_specs=[pl.BlockSpec((None, tr, bc), lambda bi, i, j: (bi, i, c0 + j)),
                  pl.BlockSpec((1, LANES), lambda bi, i, j: (0, 0)),
                  pl.BlockSpec((tr, LANES), lambda bi, i, j: (i, 0)),
                  pl.BlockSpec((tr, LANES), lambda bi, i, j: (i, 0))],
        out_specs=pl.BlockSpec((None, tr, bc * mult), lambda bi, i, j: (bi, i, j)),
        compiler_params=_params(("parallel", "parallel", "parallel")),
        name="qk_prepare",
    )(qkv, gain, cos, sin)


def _dot_nt(a, b):
    return lax.dot_general(a, b, (((1,), (1,)), ((), ())), preferred_element_type=F32)


def _ctx_attn_kernel(q_ref, k_ref, v_ref, buf_ref, o_ref, *, scale):
    del buf_ref
    s = _dot_nt(q_ref[...], k_ref[...]) * scale
    m = jnp.max(s, axis=-1, keepdims=True)
    p = jnp.exp2(s - m)
    l = jnp.sum(p, axis=-1, keepdims=True)
    o = jnp.dot(p.astype(BF16), v_ref[...], preferred_element_type=F32) / l
    o_ref[...] = o.astype(o_ref.dtype)


def ctx_self_attention(o_buf, q_arr, q_c0, k_arr, k_c0, v_arr, v_c0, n_heads, group, scale):
    b, s, dm = o_buf.shape
    return pl.pallas_call(
        functools.partial(_ctx_attn_kernel, scale=scale),
        out_shape=jax.ShapeDtypeStruct(o_buf.shape, o_buf.dtype),
        grid=(b, n_heads),
        in_specs=[pl.BlockSpec((None, CTX_LEN, LANES), lambda bi, h: (bi, 0, q_c0 + h)),
                  pl.BlockSpec((None, CTX_LEN, LANES), lambda bi, h: (bi, 0, k_c0 + h // group)),
                  pl.BlockSpec((None, CTX_LEN, LANES), lambda bi, h: (bi, 0, v_c0 + h // group)),
                  pl.BlockSpec(memory_space=pl.ANY)],
        out_specs=pl.BlockSpec((None, CTX_LEN, LANES), lambda bi, h: (bi, 0, h)),
        input_output_aliases={3: 0},
        compiler_params=_params(("parallel", "parallel")),
        name="ctx_self_attention",
    )(q_arr, k_arr, v_arr, o_buf)


def _na_kernel(q_ref, k_ref, v_ref, b_ref, o_ref, *, scale, n_rows):
    blk = pl.program_id(2)
    base = jnp.clip(NA_ROW_BLOCK * blk - NA_KH // 2, 0, n_rows - NA_KEY_ROWS)
    start = pl.multiple_of(CTX_LEN + base * GRID_W, GRID_W)
    nwin = NA_KEY_ROWS * GRID_W
    ones_w, ones_c = jnp.ones((nwin, LANES), BF16), jnp.ones((CTX_LEN, LANES), BF16)
    for hh in range(q_ref.shape[1] // LANES):
        cols = slice(hh * LANES, (hh + 1) * LANES)
        q = q_ref[:, cols]
        s_w = _dot_nt(q, k_ref[pl.ds(start, nwin), cols]) * scale + b_ref[hh]
        s_c = _dot_nt(q, k_ref[0:CTX_LEN, cols]) * scale
        m = jnp.maximum(jnp.max(s_w, axis=-1, keepdims=True), jnp.max(s_c, axis=-1, keepdims=True))
        v_w = jnp.concatenate([v_ref[pl.ds(start, nwin), cols], ones_w], axis=1)
        v_c = jnp.concatenate([v_ref[0:CTX_LEN, cols], ones_c], axis=1)
        pv = (jnp.dot(jnp.exp2(s_w - m).astype(BF16), v_w, preferred_element_type=F32)
              + jnp.dot(jnp.exp2(s_c - m).astype(BF16), v_c, preferred_element_type=F32))
        o_ref[:, cols] = (pv[:, :LANES] / pv[:, LANES:]).astype(o_ref.dtype)


def _na_bias_tables(rpb, n_rows):
    rb, kr, w = NA_ROW_BLOCK, NA_KEY_ROWS, GRID_W
    kh = min(NA_KH, n_rows)
    nblk = n_rows // rb
    n_h, n_ri, n_ci = rpb.shape
    qc = np.arange(w)
    c0 = np.clip(qc - NA_KW // 2, 0, w - NA_KW)
    kc = np.arange(w)
    col_ok = (kc[None, :] >= c0[:, None]) & (kc[None, :] < c0[:, None] + NA_KW)
    col_idx = np.clip(kc[None, :] - qc[:, None], 1 - NA_KW, NA_KW - 1) + NA_KW - 1
    onehot = (col_idx.reshape(1, w * w) == np.arange(n_ci)[:, None]).astype(np.float32)
    by_col = jnp.dot(rpb.astype(F32).reshape(n_h * n_ri, n_ci) * LOG2E, jnp.asarray(onehot),
                     precision=lax.Precision.HIGHEST).reshape(n_h, n_ri, w, w)
    cases = []
    for blk in range(nblk):
        base = int(np.clip(rb * blk - NA_KH // 2, 0, n_rows - kr))
        r = rb * blk + np.arange(rb)
        r0 = np.clip(r - kh // 2, 0, n_rows - kh)
        krow = base + np.arange(kr)
        row_ok = (krow[None, :] >= r0[:, None]) & (krow[None, :] < r0[:, None] + kh)
        row_idx = np.clip(krow[None, :] - r[:, None] + NA_KH - 1, 0, 2 * NA_KH - 2)
        cases.append((base - rb * blk, row_ok, row_idx))
    for blk in range(2, nblk - 1):
        assert cases[blk][0] == cases[1][0] and all(np.array_equal(cases[blk][i], cases[1][i]) for i in (1, 2))
    out = []
    for blk in (0, 1, nblk - 1):
        _, row_ok, row_idx = cases[blk]
        tab = jnp.stack([jnp.stack([by_col[:, int(row_idx[a, b2])] for b2 in range(kr)], axis=2)
                         for a in range(rb)], axis=1)
        ok = row_ok[:, None, :, None] & col_ok[None, :, None, :]
        out.append(jnp.where(ok[None], tab, NEG_INF).reshape(n_h, rb * w, kr * w))
    return jnp.stack(out, axis=0)


def neighbourhood_attention(qkv, rpb, n_heads):
    b, s, _ = qkv.shape
    n_rows = (s - CTX_LEN) // GRID_W
    nblk = n_rows // NA_ROW_BLOCK
    tq = NA_ROW_BLOCK * GRID_W
    assert tq == CTX_LEN
    nwin = NA_KEY_ROWS * GRID_W
    bias = _na_bias_tables(rpb, n_rows)
    scale = LANES ** -0.5 * LOG2E

    hps = _pick(n_heads, (NA_HEADS_PER_STEP, 1))
    ng = n_heads // hps
    wide = hps * LANES

    def bias_case(blk):
        return jnp.where(blk == 0, 0, jnp.where(blk == nblk - 1, 2, 1))

    return pl.pallas_call(
        functools.partial(_na_kernel, scale=scale, n_rows=n_rows),
        out_shape=jax.ShapeDtypeStruct((b, s, n_heads * LANES), BF16),
        grid=(b, ng, nblk),
        in_specs=[pl.BlockSpec((None, tq, wide), lambda bi, h, i: (bi, i + 1, h)),
                  pl.BlockSpec((None, s, wide), lambda bi, h, i: (bi, 0, ng + h)),
                  pl.BlockSpec((None, s, wide), lambda bi, h, i: (bi, 0, 2 * ng + h)),
                  pl.BlockSpec((None, hps, tq, nwin), lambda bi, h, i: (bias_case(i), h, 0, 0))],
        out_specs=pl.BlockSpec((None, tq, wide), lambda bi, h, i: (bi, i + 1, h)),
        compiler_params=_params(("parallel", "parallel", "arbitrary")),
        name="neighbourhood_attention",
    )(qkv, qkv, qkv, bias)


def _gqa_kernel(q_ref, k_ref, v_ref, o_ref, m_sc, acc_sc, *, tk, group):
    m_sc[...] = jnp.full(m_sc.shape, -jnp.inf, F32)
    acc_sc[...] = jnp.zeros(acc_sc.shape, F32)
    ones = jnp.ones((tk, LANES), BF16)

    def body(c, carry):
        rows = pl.ds(pl.multiple_of(c * tk, tk), tk)
        kc = k_ref[rows, :]
        vc = jnp.concatenate([v_ref[rows, :], ones], axis=1)
        for g in range(group):
            s = _dot_nt(q_ref[:, g * LANES:(g + 1) * LANES], kc)
            m_prev = m_sc[g]
            m_new = jnp.maximum(m_prev, jnp.max(s, axis=-1, keepdims=True))
            alpha = jnp.exp2(m_prev - m_new)
            p = jnp.exp2(s - m_new[:, :1])
            pv = jnp.dot(p.astype(BF16), vc, preferred_element_type=F32)
            acc_sc[g] = jnp.concatenate([alpha, alpha], axis=1) * acc_sc[g] + pv
            m_sc[g] = m_new
        return carry

    lax.fori_loop(0, k_ref.shape[0] // tk, body, 0, unroll=True)
    for g in range(group):
        a = acc_sc[g]
        o_ref[:, g * LANES:(g + 1) * LANES] = (a[:, :LANES] / a[:, LANES:]).astype(o_ref.dtype)


def global_gqa_attention(qp, kp, qkv, n_heads, n_kv):
    b, s, _ = qp.shape
    group = n_heads // n_kv
    tq = CTX_LEN
    nq = s // tq - 1
    tk = _pick(s, (768, 512, 256))
    v_c0 = n_heads + n_kv
    return pl.pallas_call(
        functools.partial(_gqa_kernel, tk=tk, group=group),
        out_shape=jax.ShapeDtypeStruct((b, s, n_heads * LANES), BF16),
        grid=(b, n_kv, nq),
        in_specs=[pl.BlockSpec((None, tq, group * LANES), lambda bi, h, i: (bi, i + 1, h)),
                  pl.BlockSpec((None, s, LANES), lambda bi, h, i: (bi, 0, h)),
                  pl.BlockSpec((None, s, LANES), lambda bi, h, i: (bi, 0, v_c0 + h))],
        out_specs=pl.BlockSpec((None, tq, group * LANES), lambda bi, h, i: (bi, i + 1, h)),
        scratch_shapes=[pltpu.VMEM((group, tq, LANES), F32), pltpu.VMEM((group, tq, 2 * LANES), F32)],
        compiler_params=_params(("parallel", "parallel", "arbitrary")),
        name="global_gqa_attention",
    )(qp, kp, qkv)


def _swa_kernel(q_ref, k_ref, v_ref, sink_ref, o_ref, *, head_dim, n_lat):
    i = pl.program_id(2)
    tq = SWA_BLOCK
    sub_blocks = q_ref.shape[0] // tq
    n_ctx_steps = CTX_LEN // q_ref.shape[0]
    band = SWA_BLOCK + 2 * SWA_WINDOW

    @pl.when(i < n_ctx_steps)
    def _():
        o_ref[...] = jnp.zeros(o_ref.shape, o_ref.dtype)

    def query_block(sb):
        rows = slice(sb * tq, (sb + 1) * tq)
        qb = (i - n_ctx_steps) * sub_blocks + sb
        row0 = jnp.clip(CTX_LEN + SWA_BLOCK * qb - SWA_WINDOW, 0, k_ref.shape[0] - band)
        row0 = pl.multiple_of(row0, SWA_BLOCK)
        nk = band + CTX_LEN
        k_all = jnp.concatenate([k_ref[pl.ds(row0, band), :], k_ref[0:CTX_LEN, :]], axis=0)
        v_all = jnp.concatenate([jnp.concatenate([v_ref[pl.ds(row0, band), :], v_ref[0:CTX_LEN, :]], axis=0),
                                 jnp.ones((nk, LANES), BF16)], axis=1)
        qpos = SWA_BLOCK * qb + lax.broadcasted_iota(jnp.int32, (tq, nk), 0)
        col = lax.broadcasted_iota(jnp.int32, (tq, nk), 1)
        kpos = row0 - CTX_LEN + col
        ok = (col >= band) | ((kpos >= 0) & (kpos < n_lat) & (jnp.abs(kpos - qpos) <= SWA_WINDOW))
        hpg = LANES // head_dim
        lane = lax.broadcasted_iota(jnp.int32, (tq, LANES), 1)
        sels = [(lane >= j * head_dim) & (lane < (j + 1) * head_dim) for j in range(hpg)]
        q_all = q_ref[rows, :]
        n_groups = q_all.shape[1] // LANES
        gpc = min(SWA_GROUPS_PER_CHAIN, n_groups)
        ok = jnp.concatenate([ok] * (hpg * gpc), axis=0)
        for c0 in range(0, n_groups, gpc):
            heads = [(c, j) for c in range(c0, c0 + gpc) for j in range(hpg)]
            qs = jnp.concatenate([jnp.where(sels[j], q_all[:, c * LANES:(c + 1) * LANES], jnp.zeros((tq, LANES), BF16))
                                  for c, j in heads], axis=0)
            sink = jnp.concatenate([jnp.broadcast_to(sink_ref[:, c * hpg + j:c * hpg + j + 1], (tq, 1))
                                    for c, j in heads], axis=0)
            s = jnp.where(ok, _dot_nt(qs, k_all), NEG_INF)
            m = jnp.maximum(jnp.max(s, axis=-1, keepdims=True), sink)
            pv = jnp.dot(jnp.exp2(s - m).astype(BF16), v_all, preferred_element_type=F32)
            o = pv[:, :LANES] / (pv[:, LANES:] + jnp.exp2(sink - m))
            for ci, c in enumerate(range(c0, c0 + gpc)):
                res = o[ci * hpg * tq:(ci * hpg + 1) * tq]
                for j in range(1, hpg):
                    res = jnp.where(sels[j], o[(ci * hpg + j) * tq:(ci * hpg + j + 1) * tq], res)
                o_ref[rows, c * LANES:(c + 1) * LANES] = res.astype(o_ref.dtype)

    @pl.when(i >= n_ctx_steps)
    def _():
        for sb in range(sub_blocks):
            query_block(sb)


def window_gqa_attention(q, k2, v2, sink, n_heads, n_kv, head_dim):
    b, s, _ = q.shape
    group = n_heads // n_kv
    gcols = group * head_dim
    assert gcols % LANES == 0 and LANES % head_dim == 0
    tq = SWA_BLOCK * SWA_BLOCKS_PER_STEP
    assert CTX_LEN % tq == 0 and s % tq == 0
    return pl.pallas_call(
        functools.partial(_swa_kernel, head_dim=head_dim, n_lat=s - CTX_LEN),
        out_shape=jax.ShapeDtypeStruct((b, s, n_heads * head_dim), BF16),
        grid=(b, n_kv, s // tq),
        in_specs=[pl.BlockSpec((None, tq, gcols), lambda bi, h, i: (bi, i, h)),
                  pl.BlockSpec((None, s, LANES), lambda bi, h, i: (bi, 0, h)),
                  pl.BlockSpec((None, s, LANES), lambda bi, h, i: (bi, 0, h)),
                  pl.BlockSpec((None, 1, group), lambda bi, h, i: (h, 0, 0))],
        out_specs=pl.BlockSpec((None, tq, gcols), lambda bi, h, i: (bi, i, h)),
        compiler_params=_params(("parallel", "parallel", "arbitrary")),
        name="window_gqa_attention",
    )(q, k2, v2, sink.astype(F32).reshape(n_kv, 1, group))


def _retention_kernel(qf_ref, kf_ref, vf_ref, cosf_ref, sinf_ref, qb_ref, kb_ref, vb_ref, cosb_ref, sinb_ref,
                      intra_ref, qdec_ref, kdec_ref, cdec_ref, of_ref, ob_ref, state_sc, *, k_scale, heads, dh):
    @pl.when(pl.program_id(2) == 0)
    def _():
        state_sc[...] = jnp.zeros(state_sc.shape, F32)

    half = dh // 2

    def rope(x, cos, sin):
        swapped = jnp.concatenate([x[:, half:], x[:, :half]], axis=1)
        return x * cos + swapped * sin

    dirs = ((qf_ref, kf_ref, vf_ref, cosf_ref, sinf_ref, of_ref), (qb_ref, kb_ref, vb_ref, cosb_ref, sinb_ref, ob_ref))
    for dr, (q_ref, k_ref, v_ref, cos_ref, sin_ref, o_ref) in enumerate(dirs):
        cos, sin = cos_ref[...], sin_ref[...]
        for hh in range(heads):
            cols = slice(hh * dh, (hh + 1) * dh)
            q = rope(q_ref[:, cols].astype(F32), cos, sin)
            k = rope(k_ref[:, cols].astype(F32), cos, sin) * k_scale
            v = v_ref[:, cols]
            state = state_sc[dr, hh]
            att = _dot_nt(q.astype(BF16), k.astype(BF16)) * intra_ref[dr, hh]
            y = (jnp.dot(att.astype(BF16), v, preferred_element_type=F32)
                 + jnp.dot((q * qdec_ref[dr, hh]).astype(BF16), state.astype(BF16), preferred_element_type=F32))
            kd = (k * kdec_ref[dr, hh]).astype(BF16)
            state_sc[dr, hh] = state * cdec_ref[dr, hh] + lax.dot_general(kd, v, (((0,), (0,)), ((), ())),
                                                                         preferred_element_type=F32)
            y = y * lax.rsqrt(jnp.mean(y * y, axis=-1, keepdims=True) + NORM_EPS)
            o_ref[:, cols] = y.astype(o_ref.dtype)


def retention_scan(proj, decay_exp, cos, sin, n_heads):
    b, s, five_d = proj.shape
    d = five_d // 5
    dh = d // n_heads
    c = RET_CHUNK
    n_chunks = s // c
    n_ctx_chunks = CTX_LEN // c
    hps = min(RET_HEADS_PER_STEP, n_heads)
    lg = jnp.log1p(-jnp.exp2(-decay_exp.astype(F32)))
    pos = jnp.arange(c, dtype=F32)
    diff = pos[:, None] - pos[None, :]
    lgf, lgb = lg[0][:, None, None], lg[1][:, None, None]
    intra_f = jnp.where(diff >= 0, jnp.exp(lgf * jnp.maximum(diff, 0.0)), 0.0)
    intra_b = jnp.where(diff <= 0, jnp.exp(lgb * jnp.maximum(-diff, 0.0)), 0.0)
    intra = jnp.stack([intra_f, intra_b])
    qdec = jnp.stack([jnp.exp(lg[0][:, None] * (pos + 1.0)), jnp.exp(lg[1][:, None] * (c - pos))])[..., None]
    kdec = jnp.stack([jnp.exp(lg[0][:, None] * (c - 1.0 - pos)), jnp.exp(lg[1][:, None] * pos)])[..., None]
    cdec = jnp.exp(lg * c)[..., None, None]
    ncb = n_heads // hps

    def bwd_chunk(t):
        return jnp.where(t < n_ctx_chunks, n_ctx_chunks - 1 - t, n_chunks - 1 - (t - n_ctx_chunks))

    fwd = lambda off: (lambda bi, h, t: (bi, t, off * ncb + h))
    bwd = lambda off: (lambda bi, h, t: (bi, bwd_chunk(t), off * ncb + h))
    tab = lambda bi, h, t: (0, h, 0, 0)
    blk = (None, c, hps * dh)
    return pl.pallas_call(
        functools.partial(_retention_kernel, k_scale=dh ** -0.5, heads=hps, dh=dh),
        out_shape=(jax.ShapeDtypeStruct((b, s, d), BF16), jax.ShapeDtypeStruct((b, s, d), BF16)),
        grid=(b, ncb, n_chunks),
        in_specs=[pl.BlockSpec(blk, fwd(0)), pl.BlockSpec(blk, fwd(1)), pl.BlockSpec(blk, fwd(2)),
                  pl.BlockSpec((c, dh), lambda bi, h, t: (t, 0)), pl.BlockSpec((c, dh), lambda bi, h, t: (t, 0)),
                  pl.BlockSpec(blk, bwd(0)), pl.BlockSpec(blk, bwd(1)), pl.BlockSpec(blk, bwd(2)),
                  pl.BlockSpec((c, dh), lambda bi, h, t: (bwd_chunk(t), 0)),
                  pl.BlockSpec((c, dh), lambda bi, h, t: (bwd_chunk(t), 0)),
                  pl.BlockSpec((2, hps, c, c), tab),
                  pl.BlockSpec((2, hps, c, 1), tab),
                  pl.BlockSpec((2, hps, c, 1), tab),
                  pl.BlockSpec((2, hps, 1, 1), tab)],
        out_specs=(pl.BlockSpec(blk, lambda bi, h, t: (bi, t, h)),
                   pl.BlockSpec(blk, lambda bi, h, t: (bi, bwd_chunk(t), h))),
        scratch_shapes=[pltpu.VMEM((2, hps, dh, dh), F32)],
        compiler_params=_params(("parallel", "parallel", "arbitrary")),
        name="retention_scan",
    )(proj, proj, proj, cos, sin, proj, proj, proj, cos, sin, intra, qdec, kdec, cdec)


def _ret_merge_kernel(yf_ref, yb_ref, gf_ref, gb_ref, o_ref):
    gf = gf_ref[...].astype(F32)
    gb = gb_ref[...].astype(F32)
    o = gf * jax.nn.sigmoid(gf) * yf_ref[...].astype(F32) + gb * jax.nn.sigmoid(gb) * yb_ref[...].astype(F32)
    o_ref[...] = o.astype(o_ref.dtype)


def retention_merge(y_f, y_b, proj):
    b, s, d = y_f.shape
    tr = _pick(s, (1408, 1024, 768, 512, 256))
    bc = _pick(d, (1024, 512, 256, 128))
    nc = d // bc
    return pl.pallas_call(
        _ret_merge_kernel,
        out_shape=jax.ShapeDtypeStruct((b, s, d), BF16),
        grid=(b, s // tr, nc),
        in_specs=[pl.BlockSpec((None, tr, bc), lambda bi, i, j: (bi, i, j)),
                  pl.BlockSpec((None, tr, bc), lambda bi, i, j: (bi, i, j)),
                  pl.BlockSpec((None, tr, bc), lambda bi, i, j: (bi, i, 3 * nc + j)),
                  pl.BlockSpec((None, tr, bc), lambda bi, i, j: (bi, i, 4 * nc + j))],
        out_specs=pl.BlockSpec((None, tr, bc), lambda bi, i, j: (bi, i, j)),
        compiler_params=_params(("parallel", "parallel", "parallel")),
        name="retention_merge",
    )(y_f, y_b, proj, proj)


def _moe_kernel(be_ref, first_ref, nused_ref, x_ref, w1_ref, perm_ref, b1_ref, w2_ref, b2_ref, g_ref, o_ref,
                w1p_sc, w2p_sc):
    i = pl.program_id(0)
    f = w2_ref.shape[0]
    fp = w2p_sc.shape[0]

    @pl.when(i == 0)
    def _():
        w2p_sc[...] = jnp.zeros(w2p_sc.shape, w2p_sc.dtype)

    @pl.when(first_ref[i] == 1)
    def _():
        w1p_sc[...] = jnp.dot(perm_ref[...], w1_ref[...].astype(BF16), preferred_element_type=F32).astype(BF16)
        w2p_sc[0:f, :] = w2_ref[...].astype(BF16)

    @pl.when(i < nused_ref[0])
    def _():
        u = _dot_nt(x_ref[...], w1p_sc[...]) + b1_ref[...]
        glu = jnp.minimum(u[:, :fp], SWIGLU_LIMIT)
        lin = jnp.clip(u[:, fp:], -SWIGLU_LIMIT, SWIGLU_LIMIT)
        a = glu * jax.nn.sigmoid(SWIGLU_ALPHA * glu) * (lin + 1.0)
        y = jnp.dot(a.astype(BF16), w2p_sc[...], preferred_element_type=F32) + b2_ref[...]
        o_ref[...] = (y * g_ref[...]).astype(o_ref.dtype)

    @pl.when(i >= nused_ref[0])
    def _():
        o_ref[...] = jnp.zeros(o_ref.shape, o_ref.dtype)


def moe_experts(xg, blk_expert, blk_first, n_used, slot_gate, layer, w1, b1, w2, b2):
    n_slots, d = xg.shape
    _, n_e, _, f2 = w1.shape
    f = f2 // 2
    fp = -(-f // LANES) * LANES
    n_blocks = n_slots // MOE_BLOCK
    w1t = jnp.swapaxes(w1, 2, 3)
    perm = np.zeros((2 * fp, f2), np.float32)
    perm[np.arange(f), 2 * np.arange(f)] = 1.0
    perm[fp + np.arange(f), 2 * np.arange(f) + 1] = 1.0
    b1l = b1[layer]
    b1p = jnp.zeros((n_e, 1, 2 * fp), F32).at[:, 0, :f].set(b1l[:, 0::2]).at[:, 0, fp:fp + f].set(b1l[:, 1::2])

    def xrow(i, be, first, nused):
        return (jnp.minimum(i, jnp.maximum(nused[0] - 1, 0)), 0)

    def expert(i, be, first, nused):
        return (be[i], 0, 0)

    def layer_expert(i, be, first, nused):
        return (layer, be[i], 0, 0)

    grid_spec = pltpu.PrefetchScalarGridSpec(
        num_scalar_prefetch=3,
        grid=(n_blocks,),
        in_specs=[pl.BlockSpec((MOE_BLOCK, d), xrow),
                  pl.BlockSpec((None, None, f2, d), layer_expert),
                  pl.BlockSpec((2 * fp, f2), lambda i, be, first, nused: (0, 0)),
                  pl.BlockSpec((None, 1, 2 * fp), expert),
                  pl.BlockSpec((None, None, f, d), layer_expert),
                  pl.BlockSpec((None, 1, d), expert),
                  pl.BlockSpec((MOE_BLOCK, 1), xrow)],
        out_specs=pl.BlockSpec((MOE_BLOCK, d), lambda i, be, first, nused: (i, 0)),
        scratch_shapes=[pltpu.VMEM((2 * fp, d), BF16), pltpu.VMEM((fp, d), BF16)],
    )
    return pl.pallas_call(
        _moe_kernel,
        out_shape=jax.ShapeDtypeStruct((n_slots, d), BF16),
        grid_spec=grid_spec,
        compiler_params=_params(("arbitrary",)),
        name="moe_experts",
    )(blk_expert, blk_first, n_used, xg, w1t, jnp.asarray(perm, BF16), b1p, w2,
      b2[layer].astype(F32).reshape(n_e, 1, d), slot_gate)


def _combined(x_ref, g_ref, y_refs):
    y0, y1, y2, y3 = (r[...].astype(F32) for r in y_refs)
    return x_ref[...] + g_ref[0] * ((y0 + y1) + (y2 + y3))


def _moe_combine_norm_kernel(x_ref, g_ref, y0_ref, y1_ref, y2_ref, y3_ref, ng_ref, m_ref, o_ref, h_ref):
    x = _combined(x_ref, g_ref, (y0_ref, y1_ref, y2_ref, y3_ref))
    o_ref[...] = x
    h_ref[...] = _rms_mod(x, ng_ref[...], m_ref[0:1, :], m_ref[1:2, :]).astype(h_ref.dtype)


def _moe_combine_final_kernel(x_ref, g_ref, y0_ref, y1_ref, y2_ref, y3_ref, ng_ref, o_ref):
    x = _combined(x_ref, g_ref, (y0_ref, y1_ref, y2_ref, y3_ref))
    o_ref[...] = x * lax.rsqrt(jnp.mean(x * x, axis=-1, keepdims=True) + NORM_EPS) * ng_ref[...]


def moe_combine(xs, gate_groups, ysg, next_g, next_modv):
    b, s, d = xs.shape
    assert TOP_K == 4
    tr = CTX_LEN
    nblk = s // tr
    nr = b * nblk
    xs2 = xs.reshape(b * s, d)
    final = next_modv is None
    off = 1 if final else 0

    def row(bi, i):
        return bi * nblk + off + i

    yspec = [pl.BlockSpec((tr, d), functools.partial(lambda bi, i, k: (k * nr + row(bi, i), 0), k=k))
             for k in range(TOP_K)]
    in_specs = [pl.BlockSpec((tr, d), lambda bi, i: (row(bi, i), 0)),
                pl.BlockSpec((1, 1, d), lambda bi, i: (row(bi, i), 0, 0))] + yspec + [
                    pl.BlockSpec((1, d), lambda bi, i: (0, 0))]
    args = [xs2, gate_groups, ysg, ysg, ysg, ysg, next_g.reshape(1, d)]
    if final:
        return pl.pallas_call(
            _moe_combine_final_kernel,
            out_shape=jax.ShapeDtypeStruct((b, s - CTX_LEN, d), F32),
            grid=(b, nblk - 1),
            in_specs=in_specs,
            out_specs=pl.BlockSpec((None, tr, d), lambda bi, i: (bi, i, 0)),
            compiler_params=_params(("parallel", "parallel")),
            name="moe_combine_final_norm",
        )(*args)
    in_specs.append(pl.BlockSpec((None, None, N_MOD, d), lambda bi, i: (bi, jnp.minimum(i, 1), 0, 0)))
    xs2, h = pl.pallas_call(
        _moe_combine_norm_kernel,
        out_shape=(jax.ShapeDtypeStruct((b * s, d), F32), jax.ShapeDtypeStruct((b * s, d), BF16)),
        grid=(b, nblk),
        in_specs=in_specs,
        out_specs=(pl.BlockSpec((tr, d), lambda bi, i: (row(bi, i), 0)),
                   pl.BlockSpec((tr, d), lambda bi, i: (row(bi, i), 0))),
        input_output_aliases={0: 0},
        compiler_params=_params(("parallel", "parallel")),
        name="moe_combine_norm",
    )(*args, next_modv)
    return xs2.reshape(b, s, d), h.reshape(b, s, d)


def moe_layer(xs, gate_groups, h, top_idx, top_gate, layer, w1, b1, w2, b2, next_g, next_modv):
    t_tok = top_idx.shape[1]
    n_e = w1.shape[1]
    n_assign = t_tok * TOP_K
    n_blocks = -(-n_assign // MOE_BLOCK) + n_e
    n_slots = n_blocks * MOE_BLOCK
    n_extra = n_slots - n_assign
    e_flat = top_idx.reshape(n_assign)
    counts = jnp.sum((e_flat[:, None] == jnp.arange(n_e, dtype=jnp.int32)[None, :]).astype(jnp.int32), axis=0)
    padded = (counts + MOE_BLOCK - 1) // MOE_BLOCK * MOE_BLOCK
    n_used = (jnp.sum(padded) // MOE_BLOCK).astype(jnp.int32)
    pad_cum = jnp.cumsum(padded - counts)
    pad_key = jnp.sum((jnp.arange(n_extra, dtype=jnp.int32)[:, None] >= pad_cum[None, :]).astype(jnp.int32), axis=1)
    keys = jnp.concatenate([e_flat, pad_key])
    ids = jnp.arange(n_slots, dtype=jnp.int32)
    tok = jnp.concatenate([jnp.tile(jnp.arange(t_tok, dtype=jnp.int32), TOP_K),
                           jnp.arange(n_extra, dtype=jnp.int32) % t_tok])
    gate = jnp.concatenate([top_gate.reshape(n_assign), jnp.zeros((n_extra,), F32)])
    slot_key, slot_id, slot_tok, slot_gate = lax.sort((keys, ids, tok, gate), dimension=0, is_stable=True, num_keys=1)
    _, slot_of = lax.sort_key_val(slot_id, ids)
    blk_expert = jnp.minimum(slot_key.reshape(n_blocks, MOE_BLOCK)[:, 0], n_e - 1)
    prev = jnp.concatenate([jnp.full((1,), -1, jnp.int32), blk_expert[:-1]])
    blk_first = ((blk_expert != prev) & (jnp.arange(n_blocks) < n_used)).astype(jnp.int32)
    xg = h.at[slot_tok].get(mode="promise_in_bounds")
    ys = moe_experts(xg, blk_expert, blk_first, n_used.reshape(1), slot_gate.reshape(n_slots, 1), layer,
                     w1, b1, w2, b2)
    ysg = ys.at[slot_of[:n_assign]].get(mode="promise_in_bounds")
    return moe_combine(xs, gate_groups, ysg, next_g, next_modv)


def _group_rows(vec_ctx, vec_lat, s):
    b, d = vec_lat.shape
    ng = s // CTX_LEN
    g = jnp.concatenate([jnp.broadcast_to(vec_ctx[None, None, :], (b, 1, d)),
                         jnp.broadcast_to(vec_lat[:, None, :], (b, ng - 1, d))], axis=1)
    return g.reshape(b * ng, 1, d)


def kernel(x, c, ctx, c_ctx, w_mod, b_mod, g_mix, g_ffn, g_final, na_w_qkv, na_w_o, na_rpb, ret_w_in, ret_w_o, ret_decay_exp, gqa_w_qkv, gqa_w_o, gqa_q_gain, gqa_k_gain, swa_w_qkv, swa_w_o, swa_sink, moe_w_router, moe_b_router, moe_w1, moe_b1, moe_w2, moe_b2):
    b, l, d = x.shape
    assert ctx.shape[1] == CTX_LEN and l % CTX_LEN == 0
    s = CTX_LEN + l
    depth = w_mod.shape[0]
    xs = jnp.concatenate([ctx, x], axis=1)

    n_rows = -(-(b + 1) // 16) * 16
    s_rows = jnp.zeros((n_rows, d), F32).at[:b].set(jax.nn.silu(c)).at[b].set(jax.nn.silu(c_ctx))
    mod = modulation_all_layers(s_rows, w_mod, b_mod).reshape(depth, n_rows, N_MOD, d)

    def layer_mod(li):
        mod_lat, mod_ctx = mod[li, :b], mod[li, b]
        modv = jnp.stack([jnp.broadcast_to(mod_ctx[None], (b, N_MOD, d)), mod_lat], axis=1)
        return mod_lat, mod_ctx, modv

    ones = jnp.ones((1, LANES), F32)
    h = norm_modulate(xs, g_mix[0], layer_mod(0)[2], 0)
    for li in range(depth):
        mix, j = li % N_MIXERS, li // N_MIXERS
        need_ctx = li < depth - 1
        mod_lat, mod_ctx, modv = layer_mod(li)
        h2 = h.reshape(b * s, d)
        if mix == 0:
            qkv = matmul(h2, na_w_qkv[j]).reshape(b, s, 3 * d)
            o = neighbourhood_attention(qkv, na_rpb[j], NA_HEADS)
            o = ctx_self_attention(o, qkv, 0, qkv, NA_HEADS, qkv, 2 * NA_HEADS, NA_HEADS, 1, LANES ** -0.5 * LOG2E)
            w_o = na_w_o[j]
        elif mix == 1:
            proj = matmul(h2, ret_w_in[j]).reshape(b, s, 5 * d)
            cos, sin = rope_tables(l, d // RET_HEADS)
            y_f, y_b = retention_scan(proj, ret_decay_exp[j], cos, sin, RET_HEADS)
            o = retention_merge(y_f, y_b, proj)
            w_o = ret_w_o[j]
        elif mix == 2:
            dh = d // GQA_HEADS
            assert dh == LANES
            qkv = matmul(h2, gqa_w_qkv[j]).reshape(b, s, -1)
            cos, sin = rope_tables(l, dh)
            qg = (gqa_q_gain[j].astype(F32) * (dh ** -0.5 * LOG2E)).reshape(1, LANES)
            kg = gqa_k_gain[j].astype(F32).reshape(1, LANES)
            qp = qk_prepare(qkv, 0, GQA_HEADS * dh, qg, cos, sin, dh, True)
            kp = qk_prepare(qkv, GQA_HEADS * dh, GQA_KV_HEADS * dh, kg, cos, sin, dh, True)
            o = global_gqa_attention(qp, kp, qkv, GQA_HEADS, GQA_KV_HEADS)
            o = ctx_self_attention(o, qp, 0, kp, 0, qkv, GQA_HEADS + GQA_KV_HEADS, GQA_HEADS,
                                   GQA_HEADS // GQA_KV_HEADS, 1.0)
            w_o = gqa_w_o[j]
        else:
            dh = d // SWA_HEADS
            qkv = matmul(h2, swa_w_qkv[j]).reshape(b, s, -1)
            cos, sin = rope_tables(l, dh)
            reps = LANES // dh
            cos, sin = jnp.tile(cos, (1, reps)), jnp.tile(sin, (1, reps))
            unit = jnp.zeros((s, LANES), F32)
            qp = qk_prepare(qkv, 0, SWA_HEADS * dh, ones * (dh ** -0.5 * LOG2E), cos, sin, dh, False)
            k2 = qk_prepare(qkv, SWA_HEADS * dh, SWA_KV_HEADS * dh, ones, cos, sin, dh, False, dup=True)
            v2 = qk_prepare(qkv, (SWA_HEADS + SWA_KV_HEADS) * dh, SWA_KV_HEADS * dh, ones,
                            jnp.ones((s, LANES), F32), unit, dh, False, dup=True)
            o = window_gqa_attention(qp, k2, v2, swa_sink[j].astype(F32) * LOG2E, SWA_HEADS, SWA_KV_HEADS, dh)
            w_o = swa_w_o[j]
        if mix in (0, 2) and not need_ctx:
            o = o.at[:, :CTX_LEN].set(0)
        xs = matmul_residual(o.reshape(b * s, d), w_o, xs.reshape(b * s, d),
                             _group_rows(mod_ctx[2], mod_lat[:, 2], s)).reshape(b, s, d)

        hf, top_idx, top_gate = norm_modulate_route(xs, g_ffn[li], modv, 3, moe_w_router[li], moe_b_router[li])
        top_idx = top_idx[:, :TOP_K].transpose(1, 0, 2).reshape(TOP_K, b * s)
        top_gate = top_gate[:, :TOP_K].transpose(1, 0, 2).reshape(TOP_K, b * s)
        last = li == depth - 1
        out = moe_layer(xs, _group_rows(mod_ctx[5], mod_lat[:, 5], s), hf.reshape(b * s, d), top_idx, top_gate,
                        li, moe_w1, moe_b1, moe_w2, moe_b2,
                        g_final if last else g_mix[li + 1], None if last else layer_mod(li + 1)[2])
        if last:
            return out
        xs, h = out
```

```python
import functools

import jax
import jax.numpy as jnp
import numpy as np
from jax import lax
from jax.experimental import pallas as pl
from jax.experimental.pallas import tpu as pltpu

DEPTH = 4
GRID_W = 64
CTX_LEN = 256
N_MIXERS = 4
N_MOD = 6
NORM_EPS = 1e-6
NEG_INF = -1e30
ROPE_THETA = 10000.0

NA_HEADS = 32
NA_KH = 8
NA_KW = 16
NA_ROW_BLOCK = 4
NA_KEY_ROWS = 12
NA_HEADS_PER_STEP = 4

RET_HEADS = 16
RET_CHUNK = 128
RET_HEADS_PER_STEP = 8

GQA_HEADS = 32
GQA_KV_HEADS = 8

SWA_HEADS = 64
SWA_KV_HEADS = 8
SWA_WINDOW = 128
SWA_BLOCK = 128
SWA_BLOCKS_PER_STEP = 2
SWA_GROUPS_PER_CHAIN = 2

N_EXPERTS = 32
TOP_K = 4
SWIGLU_ALPHA = 1.702
SWIGLU_LIMIT = 7.0
MOE_BLOCK = 256

LOG2E = 1.4426950408889634
LANES = 128
SUBLANES = 8
VMEM_LIMIT_BYTES = 56 * 1024 * 1024

F32 = jnp.float32
BF16 = jnp.bfloat16


def _params(semantics, vmem=VMEM_LIMIT_BYTES):
    return pltpu.CompilerParams(dimension_semantics=semantics, vmem_limit_bytes=vmem)


def _pick(n, candidates):
    for c in candidates:
        if n % c == 0:
            return c
    return n


def _mod_kernel(s_ref, w_ref, b_ref, o_ref):
    acc = jnp.dot(s_ref[...], w_ref[...].astype(BF16), preferred_element_type=F32)
    o_ref[...] = acc + b_ref[...]


def modulation_layer(s_rows, w_mod, b_mod, layer):
    depth, d, n = w_mod.shape
    rows = s_rows.shape[0]
    bn = _pick(n, (512, 256, 128))
    return pl.pallas_call(
        _mod_kernel,
        out_shape=jax.ShapeDtypeStruct((rows, n), F32),
        grid=(n // bn,),
        in_specs=[
            pl.BlockSpec((rows, d), lambda j: (0, 0)),
            pl.BlockSpec((None, d, bn), lambda j: (layer, 0, j)),
            pl.BlockSpec((None, 1, bn), lambda j: (layer, 0, j)),
        ],
        out_specs=pl.BlockSpec((rows, bn), lambda j: (0, j)),
        compiler_params=_params(("parallel",)),
        name="modulation",
    )(s_rows, w_mod, b_mod.reshape(depth, 1, n))


def _rms_mod(x, g, shift, scale):
    y = x * lax.rsqrt(jnp.mean(x * x, axis=-1, keepdims=True) + NORM_EPS) * g
    return y * (1.0 + scale) + shift


def _norm_mod_kernel(x_ref, g_ref, m_ref, o_ref, *, shift_idx):
    h = _rms_mod(x_ref[...], g_ref[...], m_ref[shift_idx:shift_idx + 1, :], m_ref[shift_idx + 1:shift_idx + 2, :])
    o_ref[...] = h.astype(o_ref.dtype)


def _norm_route_kernel(x_ref, g_ref, m_ref, wh_ref, wl_ref, br_ref, o_ref, idx_ref, gate_ref, *, shift_idx):
    h = _rms_mod(x_ref[...], g_ref[...], m_ref[shift_idx:shift_idx + 1, :], m_ref[shift_idx + 1:shift_idx + 2, :])
    h_hi = h.astype(BF16)
    o_ref[...] = h_hi.astype(o_ref.dtype)
    h_lo = (h - h_hi.astype(F32)).astype(BF16)
    logits = (jnp.dot(h_hi, wh_ref[...], preferred_element_type=F32)
              + (jnp.dot(h_lo, wh_ref[...], preferred_element_type=F32)
                 + jnp.dot(h_hi, wl_ref[...], preferred_element_type=F32))) + br_ref[...]
    rows, n_e = logits.shape
    e_iota = lax.broadcasted_iota(jnp.int32, (rows, n_e), 1).astype(F32)
    lane = lax.broadcasted_iota(jnp.int32, (rows, LANES), 1)
    idx_out = jnp.zeros((rows, LANES), F32)
    val_out = jnp.zeros((rows, LANES), F32)
    top0 = None
    denom = jnp.zeros((rows, 1), F32)
    work = logits
    for k in range(TOP_K):
        mx = jnp.max(work, axis=-1, keepdims=True)
        ix = jnp.min(jnp.where(work == mx, e_iota, float(n_e)), axis=-1, keepdims=True)
        if k == 0:
            top0 = mx
        ex = jnp.exp(mx - top0)
        denom = denom + ex
        idx_out = jnp.where(lane == k, ix, idx_out)
        val_out = jnp.where(lane == k, ex, val_out)
        work = jnp.where(e_iota == ix, -jnp.inf, work)
    idx_ref[...] = idx_out.T[0:SUBLANES, :].astype(jnp.int32)
    gate_ref[...] = (val_out / denom).T[0:SUBLANES, :]


def norm_modulate(xs, g, modv, shift_idx):
    b, s, d = xs.shape
    tr = CTX_LEN
    nblk = s // tr
    return pl.pallas_call(
        functools.partial(_norm_mod_kernel, shift_idx=shift_idx),
        out_shape=jax.ShapeDtypeStruct((b, s, d), BF16),
        grid=(b, nblk),
        in_specs=[
            pl.BlockSpec((None, tr, d), lambda bi, i: (bi, i, 0)),
            pl.BlockSpec((1, d), lambda bi, i: (0, 0)),
            pl.BlockSpec((None, None, N_MOD, d), lambda bi, i: (bi, jnp.minimum(i, 1), 0, 0)),
        ],
        out_specs=pl.BlockSpec((None, tr, d), lambda bi, i: (bi, i, 0)),
        compiler_params=_params(("parallel", "parallel")),
        name="norm_modulate",
    )(xs, g.reshape(1, d), modv)


def norm_modulate_route(xs, g, modv, shift_idx, w_router, b_router):
    b, s, d = xs.shape
    tr = CTX_LEN
    nblk = s // tr
    n_e = w_router.shape[1]
    w_hi = w_router.astype(BF16)
    w_lo = (w_router.astype(F32) - w_hi.astype(F32)).astype(BF16)
    return pl.pallas_call(
        functools.partial(_norm_route_kernel, shift_idx=shift_idx),
        out_shape=(jax.ShapeDtypeStruct((b, s, d), BF16),
                   jax.ShapeDtypeStruct((b, SUBLANES, s), jnp.int32),
                   jax.ShapeDtypeStruct((b, SUBLANES, s), F32)),
        grid=(b, nblk),
        in_specs=[
            pl.BlockSpec((None, tr, d), lambda bi, i: (bi, i, 0)),
            pl.BlockSpec((1, d), lambda bi, i: (0, 0)),
            pl.BlockSpec((None, None, N_MOD, d), lambda bi, i: (bi, jnp.minimum(i, 1), 0, 0)),
            pl.BlockSpec((d, n_e), lambda bi, i: (0, 0)),
            pl.BlockSpec((d, n_e), lambda bi, i: (0, 0)),
            pl.BlockSpec((1, n_e), lambda bi, i: (0, 0)),
        ],
        out_specs=(pl.BlockSpec((None, tr, d), lambda bi, i: (bi, i, 0)),
                   pl.BlockSpec((None, SUBLANES, tr), lambda bi, i: (bi, 0, i)),
                   pl.BlockSpec((None, SUBLANES, tr), lambda bi, i: (bi, 0, i))),
        compiler_params=_params(("parallel", "parallel")),
        name="norm_modulate_route",
    )(xs, g.reshape(1, d), modv, w_hi, w_lo, b_router.astype(F32).reshape(1, n_e))


def _mm_kernel(x_ref, w_ref, o_ref, wb_sc):
    @pl.when(pl.program_id(1) == 0)
    def _():
        wb_sc[...] = w_ref[...].astype(BF16)

    o_ref[...] = jnp.dot(x_ref[...], wb_sc[...], preferred_element_type=F32).astype(o_ref.dtype)


def _mm_residual_kernel(x_ref, w_ref, r_ref, g_ref, o_ref, wb_sc, *, groups, group_rows):
    @pl.when(pl.program_id(1) == 0)
    def _():
        wb_sc[...] = w_ref[...].astype(BF16)

    acc = jnp.dot(x_ref[...], wb_sc[...], preferred_element_type=F32)
    for gi in range(groups):
        rows = slice(gi * group_rows, (gi + 1) * group_rows)
        o_ref[rows, :] = r_ref[rows, :] + g_ref[gi] * acc[rows, :]


def matmul(x, w):
    m, k = x.shape
    n = w.shape[1]
    bm = _pick(m, (1536, 768, 512, 256, 128))
    bn = _pick(n, (512, 256, 128))
    return pl.pallas_call(
        _mm_kernel,
        out_shape=jax.ShapeDtypeStruct((m, n), BF16),
        grid=(n // bn, m // bm),
        in_specs=[pl.BlockSpec((bm, k), lambda j, i: (i, 0)),
                  pl.BlockSpec((k, bn), lambda j, i: (0, j))],
        out_specs=pl.BlockSpec((bm, bn), lambda j, i: (i, j)),
        scratch_shapes=[pltpu.VMEM((k, bn), BF16)],
        compiler_params=_params(("parallel", "arbitrary")),
        name="projection",
    )(x, w)


def matmul_residual(x, w, res, gate_groups):
    m, k = x.shape
    n = w.shape[1]
    bm = _pick(m, (768, 512, 256))
    bn = _pick(n, (512, 256, 128))
    groups = bm // CTX_LEN
    return pl.pallas_call(
        functools.partial(_mm_residual_kernel, groups=groups, group_rows=CTX_LEN),
        out_shape=jax.ShapeDtypeStruct((m, n), F32),
        grid=(n // bn, m // bm),
        in_specs=[pl.BlockSpec((bm, k), lambda j, i: (i, 0)),
                  pl.BlockSpec((k, bn), lambda j, i: (0, j)),
                  pl.BlockSpec((bm, bn), lambda j, i: (i, j)),
                  pl.BlockSpec((groups, 1, bn), lambda j, i: (i, 0, j))],
        out_specs=pl.BlockSpec((bm, bn), lambda j, i: (i, j)),
        scratch_shapes=[pltpu.VMEM((k, bn), BF16)],
        input_output_aliases={2: 0},
        compiler_params=_params(("parallel", "arbitrary")),
        name="out_projection_residual",
    )(x, w, res, gate_groups)


def rope_tables(seq, head_dim):
    half = head_dim // 2
    n = half // 2
    t = jnp.arange(seq)
    rows, cols = (t // GRID_W).astype(F32), (t % GRID_W).astype(F32)
    freqs = ROPE_THETA ** (-jnp.arange(n, dtype=F32) / n)
    ang = jnp.concatenate([rows[:, None] * freqs, cols[:, None] * freqs], axis=-1)
    cos, sin = jnp.cos(ang), jnp.sin(ang)
    cos = jnp.concatenate([jnp.ones((CTX_LEN, half), F32), cos], axis=0)
    sin = jnp.concatenate([jnp.zeros((CTX_LEN, half), F32), sin], axis=0)
    return jnp.concatenate([cos, cos], axis=-1), jnp.concatenate([-sin, sin], axis=-1)


def _swap_halves(y, head_dim):
    half = head_dim // 2
    if head_dim == LANES:
        return pltpu.roll(y, half, 1)
    lane = lax.broadcasted_iota(jnp.int32, y.shape, 1)
    first = (lane % head_dim) < half
    return jnp.where(first, pltpu.roll(y, LANES - half, 1), pltpu.roll(y, half, 1))


def _qk_prep_kernel(x_ref, gain_ref, cos_ref, sin_ref, o_ref, *, head_dim, normalise, dup):
    x = x_ref[...].astype(F32)
    cols = x.shape[1]
    cos, sin = cos_ref[...], sin_ref[...]
    gain = gain_ref[...]
    for c in range(cols // LANES):
        y = x[:, c * LANES:(c + 1) * LANES]
        if normalise:
            y = y * lax.rsqrt(jnp.mean(y * y, axis=-1, keepdims=True) + NORM_EPS)
        y = y * gain
        y = (y * cos + _swap_halves(y, head_dim) * sin).astype(o_ref.dtype)
        if dup:
            lane = lax.broadcasted_iota(jnp.int32, y.shape, 1)
            other = pltpu.roll(y.astype(F32), head_dim, 1).astype(o_ref.dtype)
            o_ref[:, (2 * c) * LANES:(2 * c + 1) * LANES] = jnp.where(lane < head_dim, y, other)
            o_ref[:, (2 * c + 1) * LANES:(2 * c + 2) * LANES] = jnp.where(lane < head_dim, other, y)
        else:
            o_ref[:, c * LANES:(c + 1) * LANES] = y


def qk_prepare(qkv, col0, ncols, gain, cos, sin, head_dim, normalise, dup=False):
    b, s, _ = qkv.shape
    tr = _pick(s, (1408, 1024, 768, 512, 256))
    bc = _pick(ncols, (512, 256, 128))
    c0 = col0 // bc
    assert col0 % bc == 0
    mult = 2 if dup else 1
    return pl.pallas_call(
        functools.partial(_qk_prep_kernel, head_dim=head_dim, normalise=normalise, dup=dup),
        out_shape=jax.ShapeDtypeStruct((b, s, ncols * mult), BF16),
        grid=(b, s // tr, ncols // bc),
        in_specs=[pl.BlockSpec((None, tr, bc), lambda bi, i, j: (bi, i, c0 + j)),
                  pl.BlockSpec((1, LANES), lambda bi, i, j: (0, 0)),
                  pl.BlockSpec((tr, LANES), lambda bi, i, j: (i, 0)),
                  pl.BlockSpec((tr, LANES), lambda bi, i, j: (i, 0))],
        out_specs=pl.BlockSpec((None, tr, bc * mult), lambda bi, i, j: (bi, i, j)),
        compiler_params=_params(("parallel", "parallel", "parallel")),
        name="qk_prepare",
    )(qkv, gain, cos, sin)


def _dot_nt(a, b):
    return lax.dot_general(a, b, (((1,), (1,)), ((), ())), preferred_element_type=F32)


def _ctx_attn_kernel(q_ref, k_ref, v_ref, buf_ref, o_ref, *, scale):
    del buf_ref
    s = _dot_nt(q_ref[...], k_ref[...]) * scale
    m = jnp.max(s, axis=-1, keepdims=True)
    p = jnp.exp2(s - m)
    l = jnp.sum(p, axis=-1, keepdims=True)
    o = jnp.dot(p.astype(BF16), v_ref[...], preferred_element_type=F32) / l
    o_ref[...] = o.astype(o_ref.dtype)


def ctx_self_attention(o_buf, q_arr, q_c0, k_arr, k_c0, v_arr, v_c0, n_heads, group, scale):
    b, s, dm = o_buf.shape
    return pl.pallas_call(
        functools.partial(_ctx_attn_kernel, scale=scale),
        out_shape=jax.ShapeDtypeStruct(o_buf.shape, o_buf.dtype),
        grid=(b, n_heads),
        in_specs=[pl.BlockSpec((None, CTX_LEN, LANES), lambda bi, h: (bi, 0, q_c0 + h)),
                  pl.BlockSpec((None, CTX_LEN, LANES), lambda bi, h: (bi, 0, k_c0 + h // group)),
                  pl.BlockSpec((None, CTX_LEN, LANES), lambda bi, h: (bi, 0, v_c0 + h // group)),
                  pl.BlockSpec(memory_space=pl.ANY)],
        out_specs=pl.BlockSpec((None, CTX_LEN, LANES), lambda bi, h: (bi, 0, h)),
        input_output_aliases={3: 0},
        compiler_params=_params(("parallel", "parallel")),
        name="ctx_self_attention",
    )(q_arr, k_arr, v_arr, o_buf)


def _na_kernel(q_ref, k_ref, v_ref, b_ref, o_ref, *, scale, n_rows):
    blk = pl.program_id(2)
    base = jnp.clip(NA_ROW_BLOCK * blk - NA_KH // 2, 0, n_rows - NA_KEY_ROWS)
    start = pl.multiple_of(CTX_LEN + base * GRID_W, GRID_W)
    nwin = NA_KEY_ROWS * GRID_W
    ones_w, ones_c = jnp.ones((nwin, LANES), BF16), jnp.ones((CTX_LEN, LANES), BF16)
    for hh in range(q_ref.shape[1] // LANES):
        cols = slice(hh * LANES, (hh + 1) * LANES)
        q = q_ref[:, cols]
        s_w = _dot_nt(q, k_ref[pl.ds(start, nwin), cols]) * scale + b_ref[hh]
        s_c = _dot_nt(q, k_ref[0:CTX_LEN, cols]) * scale
        m = jnp.maximum(jnp.max(s_w, axis=-1, keepdims=True), jnp.max(s_c, axis=-1, keepdims=True))
        v_w = jnp.concatenate([v_ref[pl.ds(start, nwin), cols], ones_w], axis=1)
        v_c = jnp.concatenate([v_ref[0:CTX_LEN, cols], ones_c], axis=1)
        pv = (jnp.dot(jnp.exp2(s_w - m).astype(BF16), v_w, preferred_element_type=F32)
              + jnp.dot(jnp.exp2(s_c - m).astype(BF16), v_c, preferred_element_type=F32))
        o_ref[:, cols] = (pv[:, :LANES] / pv[:, LANES:]).astype(o_ref.dtype)


def _na_bias_tables(rpb, n_rows):
    rb, kr, w = NA_ROW_BLOCK, NA_KEY_ROWS, GRID_W
    kh = min(NA_KH, n_rows)
    nblk = n_rows // rb
    n_h, n_ri, n_ci = rpb.shape
    qc = np.arange(w)
    c0 = np.clip(qc - NA_KW // 2, 0, w - NA_KW)
    kc = np.arange(w)
    col_ok = (kc[None, :] >= c0[:, None]) & (kc[None, :] < c0[:, None] + NA_KW)
    col_idx = np.clip(kc[None, :] - qc[:, None], 1 - NA_KW, NA_KW - 1) + NA_KW - 1
    onehot = (col_idx.reshape(1, w * w) == np.arange(n_ci)[:, None]).astype(np.float32)
    by_col = jnp.dot(rpb.astype(F32).reshape(n_h * n_ri, n_ci) * LOG2E, jnp.asarray(onehot),
                     precision=lax.Precision.HIGHEST).reshape(n_h, n_ri, w, w)
    cases = []
    for blk in range(nblk):
        base = int(np.clip(rb * blk - NA_KH // 2, 0, n_rows - kr))
        r = rb * blk + np.arange(rb)
        r0 = np.clip(r - kh // 2, 0, n_rows - kh)
        krow = base + np.arange(kr)
        row_ok = (krow[None, :] >= r0[:, None]) & (krow[None, :] < r0[:, None] + kh)
        row_idx = np.clip(krow[None, :] - r[:, None] + NA_KH - 1, 0, 2 * NA_KH - 2)
        cases.append((base - rb * blk, row_ok, row_idx))
    for blk in range(2, nblk - 1):
        assert cases[blk][0] == cases[1][0] and all(np.array_equal(cases[blk][i], cases[1][i]) for i in (1, 2))
    out = []
    for blk in (0, 1, nblk - 1):
        _, row_ok, row_idx = cases[blk]
        tab = jnp.stack([jnp.stack([by_col[:, int(row_idx[a, b2])] for b2 in range(kr)], axis=2)
                         for a in range(rb)], axis=1)
        ok = row_ok[:, None, :, None] & col_ok[None, :, None, :]
        out.append(jnp.where(ok[None], tab, NEG_INF).reshape(n_h, rb * w, kr * w))
    return jnp.stack(out, axis=0)


def neighbourhood_attention(qkv, rpb, n_heads):
    b, s, _ = qkv.shape
    n_rows = (s - CTX_LEN) // GRID_W
    nblk = n_rows // NA_ROW_BLOCK
    tq = NA_ROW_BLOCK * GRID_W
    assert tq == CTX_LEN
    nwin = NA_KEY_ROWS * GRID_W
    bias = _na_bias_tables(rpb, n_rows)
    scale = LANES ** -0.5 * LOG2E

    hps = _pick(n_heads, (NA_HEADS_PER_STEP, 1))
    ng = n_heads // hps
    wide = hps * LANES

    def bias_case(blk):
        return jnp.where(blk == 0, 0, jnp.where(blk == nblk - 1, 2, 1))

    return pl.pallas_call(
        functools.partial(_na_kernel, scale=scale, n_rows=n_rows),
        out_shape=jax.ShapeDtypeStruct((b, s, n_heads * LANES), BF16),
        grid=(b, ng, nblk),
        in_specs=[pl.BlockSpec((None, tq, wide), lambda bi, h, i: (bi, i + 1, h)),
                  pl.BlockSpec((None, s, wide), lambda bi, h, i: (bi, 0, ng + h)),
                  pl.BlockSpec((None, s, wide), lambda bi, h, i: (bi, 0, 2 * ng + h)),
                  pl.BlockSpec((None, hps, tq, nwin), lambda bi, h, i: (bias_case(i), h, 0, 0))],
        out_specs=pl.BlockSpec((None, tq, wide), lambda bi, h, i: (bi, i + 1, h)),
        compiler_params=_params(("parallel", "parallel", "arbitrary")),
        name="neighbourhood_attention",
    )(qkv, qkv, qkv, bias)


def _gqa_kernel(q_ref, k_ref, v_ref, o_ref, m_sc, acc_sc, *, tk, group):
    m_sc[...] = jnp.full(m_sc.shape, -jnp.inf, F32)
    acc_sc[...] = jnp.zeros(acc_sc.shape, F32)
    ones = jnp.ones((tk, LANES), BF16)

    def body(c, carry):
        rows = pl.ds(pl.multiple_of(c * tk, tk), tk)
        kc = k_ref[rows, :]
        vc = jnp.concatenate([v_ref[rows, :], ones], axis=1)
        for g in range(group):
            s = _dot_nt(q_ref[:, g * LANES:(g + 1) * LANES], kc)
            m_prev = m_sc[g]
            m_new = jnp.maximum(m_prev, jnp.max(s, axis=-1, keepdims=True))
            alpha = jnp.exp2(m_prev - m_new)
            p = jnp.exp2(s - m_new[:, :1])
            pv = jnp.dot(p.astype(BF16), vc, preferred_element_type=F32)
            acc_sc[g] = jnp.concatenate([alpha, alpha], axis=1) * acc_sc[g] + pv
            m_sc[g] = m_new
        return carry

    lax.fori_loop(0, k_ref.shape[0] // tk, body, 0, unroll=True)
    for g in range(group):
        a = acc_sc[g]
        o_ref[:, g * LANES:(g + 1) * LANES] = (a[:, :LANES] / a[:, LANES:]).astype(o_ref.dtype)


def global_gqa_attention(qp, kp, qkv, n_heads, n_kv):
    b, s, _ = qp.shape
    group = n_heads // n_kv
    tq = CTX_LEN
    nq = s // tq - 1
    tk = _pick(s, (768, 512, 256))
    v_c0 = n_heads + n_kv
    return pl.pallas_call(
        functools.partial(_gqa_kernel, tk=tk, group=group),
        out_shape=jax.ShapeDtypeStruct((b, s, n_heads * LANES), BF16),
        grid=(b, n_kv, nq),
        in_specs=[pl.BlockSpec((None, tq, group * LANES), lambda bi, h, i: (bi, i + 1, h)),
                  pl.BlockSpec((None, s, LANES), lambda bi, h, i: (bi, 0, h)),
                  pl.BlockSpec((None, s, LANES), lambda bi, h, i: (bi, 0, v_c0 + h))],
        out_specs=pl.BlockSpec((None, tq, group * LANES), lambda bi, h, i: (bi, i + 1, h)),
        scratch_shapes=[pltpu.VMEM((group, tq, LANES), F32), pltpu.VMEM((group, tq, 2 * LANES), F32)],
        compiler_params=_params(("parallel", "parallel", "arbitrary")),
        name="global_gqa_attention",
    )(qp, kp, qkv)


def _swa_kernel(q_ref, k_ref, v_ref, sink_ref, o_ref, *, head_dim, n_lat):
    i = pl.program_id(2)
    tq = SWA_BLOCK
    sub_blocks = q_ref.shape[0] // tq
    n_ctx_steps = CTX_LEN // q_ref.shape[0]
    band = SWA_BLOCK + 2 * SWA_WINDOW

    @pl.when(i < n_ctx_steps)
    def _():
        o_ref[...] = jnp.zeros(o_ref.shape, o_ref.dtype)

    def query_block(sb):
        rows = slice(sb * tq, (sb + 1) * tq)
        qb = (i - n_ctx_steps) * sub_blocks + sb
        row0 = jnp.clip(CTX_LEN + SWA_BLOCK * qb - SWA_WINDOW, 0, k_ref.shape[0] - band)
        row0 = pl.multiple_of(row0, SWA_BLOCK)
        nk = band + CTX_LEN
        k_all = jnp.concatenate([k_ref[pl.ds(row0, band), :], k_ref[0:CTX_LEN, :]], axis=0)
        v_all = jnp.concatenate([jnp.concatenate([v_ref[pl.ds(row0, band), :], v_ref[0:CTX_LEN, :]], axis=0),
                                 jnp.ones((nk, LANES), BF16)], axis=1)
        qpos = SWA_BLOCK * qb + lax.broadcasted_iota(jnp.int32, (tq, nk), 0)
        col = lax.broadcasted_iota(jnp.int32, (tq, nk), 1)
        kpos = row0 - CTX_LEN + col
        ok = (col >= band) | ((kpos >= 0) & (kpos < n_lat) & (jnp.abs(kpos - qpos) <= SWA_WINDOW))
        hpg = LANES // head_dim
        lane = lax.broadcasted_iota(jnp.int32, (tq, LANES), 1)
        sels = [(lane >= j * head_dim) & (lane < (j + 1) * head_dim) for j in range(hpg)]
        q_all = q_ref[rows, :]
        n_groups = q_all.shape[1] // LANES
        gpc = min(SWA_GROUPS_PER_CHAIN, n_groups)
        ok = jnp.concatenate([ok] * (hpg * gpc), axis=0)
        for c0 in range(0, n_groups, gpc):
            heads = [(c, j) for c in range(c0, c0 + gpc) for j in range(hpg)]
            qs = jnp.concatenate([jnp.where(sels[j], q_all[:, c * LANES:(c + 1) * LANES], jnp.zeros((tq, LANES), BF16))
                                  for c, j in heads], axis=0)
            sink = jnp.concatenate([jnp.broadcast_to(sink_ref[:, c * hpg + j:c * hpg + j + 1], (tq, 1))
                                    for c, j in heads], axis=0)
            s = jnp.where(ok, _dot_nt(qs, k_all), NEG_INF)
            m = jnp.maximum(jnp.max(s, axis=-1, keepdims=True), sink)
            pv = jnp.dot(jnp.exp2(s - m).astype(BF16), v_all, preferred_element_type=F32)
            o = pv[:, :LANES] / (pv[:, LANES:] + jnp.exp2(sink - m))
            for ci, c in enumerate(range(c0, c0 + gpc)):
                res = o[ci * hpg * tq:(ci * hpg + 1) * tq]
                for j in range(1, hpg):
                    res = jnp.where(sels[j], o[(ci * hpg + j) * tq:(ci * hpg + j + 1) * tq], res)
                o_ref[rows, c * LANES:(c + 1) * LANES] = res.astype(o_ref.dtype)

    @pl.when(i >= n_ctx_steps)
    def _():
        for sb in range(sub_blocks):
            query_block(sb)


def window_gqa_attention(q, k2, v2, sink, n_heads, n_kv, head_dim):
    b, s, _ = q.shape
    group = n_heads // n_kv
    gcols = group * head_dim
    assert gcols % LANES == 0 and LANES % head_dim == 0
    tq = SWA_BLOCK * SWA_BLOCKS_PER_STEP
    assert CTX_LEN % tq == 0 and s % tq == 0
    return pl.pallas_call(
        functools.partial(_swa_kernel, head_dim=head_dim, n_lat=s - CTX_LEN),
        out_shape=jax.ShapeDtypeStruct((b, s, n_heads * head_dim), BF16),
        grid=(b, n_kv, s // tq),
        in_specs=[pl.BlockSpec((None, tq, gcols), lambda bi, h, i: (bi, i, h)),
                  pl.BlockSpec((None, s, LANES), lambda bi, h, i: (bi, 0, h)),
                  pl.BlockSpec((None, s, LANES), lambda bi, h, i: (bi, 0, h)),
                  pl.BlockSpec((None, 1, group), lambda bi, h, i: (h, 0, 0))],
        out_specs=pl.BlockSpec((None, tq, gcols), lambda bi, h, i: (bi, i, h)),
        compiler_params=_params(("parallel", "parallel", "arbitrary")),
        name="window_gqa_attention",
    )(q, k2, v2, sink.astype(F32).reshape(n_kv, 1, group))


def _retention_kernel(qf_ref, kf_ref, vf_ref, cosf_ref, sinf_ref, qb_ref, kb_ref, vb_ref, cosb_ref, sinb_ref,
                      intra_ref, qdec_ref, kdec_ref, cdec_ref, of_ref, ob_ref, state_sc, *, k_scale, heads, dh):
    @pl.when(pl.program_id(2) == 0)
    def _():
        state_sc[...] = jnp.zeros(state_sc.shape, F32)

    half = dh // 2

    def rope(x, cos, sin):
        swapped = jnp.concatenate([x[:, half:], x[:, :half]], axis=1)
        return x * cos + swapped * sin

    dirs = ((qf_ref, kf_ref, vf_ref, cosf_ref, sinf_ref, of_ref), (qb_ref, kb_ref, vb_ref, cosb_ref, sinb_ref, ob_ref))
    for dr, (q_ref, k_ref, v_ref, cos_ref, sin_ref, o_ref) in enumerate(dirs):
        cos, sin = cos_ref[...], sin_ref[...]
        for hh in range(heads):
            cols = slice(hh * dh, (hh + 1) * dh)
            q = rope(q_ref[:, cols].astype(F32), cos, sin)
            k = rope(k_ref[:, cols].astype(F32), cos, sin) * k_scale
            v = v_ref[:, cols]
            state = state_sc[dr, hh]
            att = _dot_nt(q.astype(BF16), k.astype(BF16)) * intra_ref[dr, hh]
            y = (jnp.dot(att.astype(BF16), v, preferred_element_type=F32)
                 + jnp.dot((q * qdec_ref[dr, hh]).astype(BF16), state.astype(BF16), preferred_element_type=F32))
            kd = (k * kdec_ref[dr, hh]).astype(BF16)
            state_sc[dr, hh] = state * cdec_ref[dr, hh] + lax.dot_general(kd, v, (((0,), (0,)), ((), ())),
                                                                         preferred_element_type=F32)
            y = y * lax.rsqrt(jnp.mean(y * y, axis=-1, keepdims=True) + NORM_EPS)
            o_ref[:, cols] = y.astype(o_ref.dtype)


def retention_scan(proj, decay_exp, cos, sin, n_heads):
    b, s, five_d = proj.shape
    d = five_d // 5
    dh = d // n_heads
    c = RET_CHUNK
    n_chunks = s // c
    n_ctx_chunks = CTX_LEN // c
    hps = min(RET_HEADS_PER_STEP, n_heads)
    lg = jnp.log1p(-jnp.exp2(-decay_exp.astype(F32)))
    pos = jnp.arange(c, dtype=F32)
    diff = pos[:, None] - pos[None, :]
    lgf, lgb = lg[0][:, None, None], lg[1][:, None, None]
    intra_f = jnp.where(diff >= 0, jnp.exp(lgf * jnp.maximum(diff, 0.0)), 0.0)
    intra_b = jnp.where(diff <= 0, jnp.exp(lgb * jnp.maximum(-diff, 0.0)), 0.0)
    intra = jnp.stack([intra_f, intra_b])
    qdec = jnp.stack([jnp.exp(lg[0][:, None] * (pos + 1.0)), jnp.exp(lg[1][:, None] * (c - pos))])[..., None]
    kdec = jnp.stack([jnp.exp(lg[0][:, None] * (c - 1.0 - pos)), jnp.exp(lg[1][:, None] * pos)])[..., None]
    cdec = jnp.exp(lg * c)[..., None, None]
    ncb = n_heads // hps

    def bwd_chunk(t):
        return jnp.where(t < n_ctx_chunks, n_ctx_chunks - 1 - t, n_chunks - 1 - (t - n_ctx_chunks))

    fwd = lambda off: (lambda bi, h, t: (bi, t, off * ncb + h))
    bwd = lambda off: (lambda bi, h, t: (bi, bwd_chunk(t), off * ncb + h))
    tab = lambda bi, h, t: (0, h, 0, 0)
    blk = (None, c, hps * dh)
    return pl.pallas_call(
        functools.partial(_retention_kernel, k_scale=dh ** -0.5, heads=hps, dh=dh),
        out_shape=(jax.ShapeDtypeStruct((b, s, d), BF16), jax.ShapeDtypeStruct((b, s, d), BF16)),
        grid=(b, ncb, n_chunks),
        in_specs=[pl.BlockSpec(blk, fwd(0)), pl.BlockSpec(blk, fwd(1)), pl.BlockSpec(blk, fwd(2)),
                  pl.BlockSpec((c, dh), lambda bi, h, t: (t, 0)), pl.BlockSpec((c, dh), lambda bi, h, t: (t, 0)),
                  pl.BlockSpec(blk, bwd(0)), pl.BlockSpec(blk, bwd(1)), pl.BlockSpec(blk, bwd(2)),
                  pl.BlockSpec((c, dh), lambda bi, h, t: (bwd_chunk(t), 0)),
                  pl.BlockSpec((c, dh), lambda bi, h, t: (bwd_chunk(t), 0)),
                  pl.BlockSpec((2, hps, c, c), tab),
                  pl.BlockSpec((2, hps, c, 1), tab),
                  pl.BlockSpec((2, hps, c, 1), tab),
                  pl.BlockSpec((2, hps, 1, 1), tab)],
        out_specs=(pl.BlockSpec(blk, lambda bi, h, t: (bi, t, h)),
                   pl.BlockSpec(blk, lambda bi, h, t: (bi, bwd_chunk(t), h))),
        scratch_shapes=[pltpu.VMEM((2, hps, dh, dh), F32)],
        compiler_params=_params(("parallel", "parallel", "arbitrary")),
        name="retention_scan",
    )(proj, proj, proj, cos, sin, proj, proj, proj, cos, sin, intra, qdec, kdec, cdec)


def _ret_merge_kernel(yf_ref, yb_ref, gf_ref, gb_ref, o_ref):
    gf = gf_ref[...].astype(F32)
    gb = gb_ref[...].astype(F32)
    o = gf * jax.nn.sigmoid(gf) * yf_ref[...].astype(F32) + gb * jax.nn.sigmoid(gb) * yb_ref[...].astype(F32)
    o_ref[...] = o.astype(o_ref.dtype)


def retention_merge(y_f, y_b, proj):
    b, s, d = y_f.shape
    tr = _pick(s, (1408, 1024, 768, 512, 256))
    bc = _pick(d, (1024, 512, 256, 128))
    nc = d // bc
    return pl.pallas_call(
        _ret_merge_kernel,
        out_shape=jax.ShapeDtypeStruct((b, s, d), BF16),
        grid=(b, s // tr, nc),
        in_specs=[pl.BlockSpec((None, tr, bc), lambda bi, i, j: (bi, i, j)),
                  pl.BlockSpec((None, tr, bc), lambda bi, i, j: (bi, i, j)),
                  pl.BlockSpec((None, tr, bc), lambda bi, i, j: (bi, i, 3 * nc + j)),
                  pl.BlockSpec((None, tr, bc), lambda bi, i, j: (bi, i, 4 * nc + j))],
        out_specs=pl.BlockSpec((None, tr, bc), lambda bi, i, j: (bi, i, j)),
        compiler_params=_params(("parallel", "parallel", "parallel")),
        name="retention_merge",
    )(y_f, y_b, proj, proj)


def _moe_kernel(be_ref, first_ref, nused_ref, x_ref, w1_ref, perm_ref, b1_ref, w2_ref, b2_ref, g_ref, o_ref,
                w1p_sc, w2p_sc):
    i = pl.program_id(0)
    f = w2_ref.shape[0]
    fp = w2p_sc.shape[0]

    @pl.when(i == 0)
    def _():
        w2p_sc[...] = jnp.zeros(w2p_sc.shape, w2p_sc.dtype)

    @pl.when(first_ref[i] == 1)
    def _():
        w1p_sc[...] = jnp.dot(perm_ref[...], w1_ref[...].astype(BF16), preferred_element_type=F32).astype(BF16)
        w2p_sc[0:f, :] = w2_ref[...].astype(BF16)

    @pl.when(i < nused_ref[0])
    def _():
        u = _dot_nt(x_ref[...], w1p_sc[...]) + b1_ref[...]
        glu = jnp.minimum(u[:, :fp], SWIGLU_LIMIT)
        lin = jnp.clip(u[:, fp:], -SWIGLU_LIMIT, SWIGLU_LIMIT)
        a = glu * jax.nn.sigmoid(SWIGLU_ALPHA * glu) * (lin + 1.0)
        y = jnp.dot(a.astype(BF16), w2p_sc[...], preferred_element_type=F32) + b2_ref[...]
        o_ref[...] = (y * g_ref[...]).astype(o_ref.dtype)

    @pl.when(i >= nused_ref[0])
    def _():
        o_ref[...] = jnp.zeros(o_ref.shape, o_ref.dtype)


def moe_experts(xg, blk_expert, blk_first, n_used, slot_gate, layer, w1, b1, w2, b2):
    n_slots, d = xg.shape
    _, n_e, _, f2 = w1.shape
    f = f2 // 2
    fp = -(-f // LANES) * LANES
    n_blocks = n_slots // MOE_BLOCK
    w1t = jnp.swapaxes(w1, 2, 3)
    perm = np.zeros((2 * fp, f2), np.float32)
    perm[np.arange(f), 2 * np.arange(f)] = 1.0
    perm[fp + np.arange(f), 2 * np.arange(f) + 1] = 1.0
    b1l = b1[layer]
    b1p = jnp.zeros((n_e, 1, 2 * fp), F32).at[:, 0, :f].set(b1l[:, 0::2]).at[:, 0, fp:fp + f].set(b1l[:, 1::2])

    def xrow(i, be, first, nused):
        return (jnp.minimum(i, jnp.maximum(nused[0] - 1, 0)), 0)

    def expert(i, be, first, nused):
        return (be[i], 0, 0)

    def layer_expert(i, be, first, nused):
        return (layer, be[i], 0, 0)

    grid_spec = pltpu.PrefetchScalarGridSpec(
        num_scalar_prefetch=3,
        grid=(n_blocks,),
        in_specs=[pl.BlockSpec((MOE_BLOCK, d), xrow),
                  pl.BlockSpec((None, None, f2, d), layer_expert),
                  pl.BlockSpec((2 * fp, f2), lambda i, be, first, nused: (0, 0)),
                  pl.BlockSpec((None, 1, 2 * fp), expert),
                  pl.BlockSpec((None, None, f, d), layer_expert),
                  pl.BlockSpec((None, 1, d), expert),
                  pl.BlockSpec((MOE_BLOCK, 1), xrow)],
        out_specs=pl.BlockSpec((MOE_BLOCK, d), lambda i, be, first, nused: (i, 0)),
        scratch_shapes=[pltpu.VMEM((2 * fp, d), BF16), pltpu.VMEM((fp, d), BF16)],
    )
    return pl.pallas_call(
        _moe_kernel,
        out_shape=jax.ShapeDtypeStruct((n_slots, d), BF16),
        grid_spec=grid_spec,
        compiler_params=_params(("arbitrary",)),
        name="moe_experts",
    )(blk_expert, blk_first, n_used, xg, w1t, jnp.asarray(perm, BF16), b1p, w2,
      b2[layer].astype(F32).reshape(n_e, 1, d), slot_gate)


def _combined(x_ref, g_ref, y_refs):
    y0, y1, y2, y3 = (r[...].astype(F32) for r in y_refs)
    return x_ref[...] + g_ref[0] * ((y0 + y1) + (y2 + y3))


def _moe_combine_norm_kernel(x_ref, g_ref, y0_ref, y1_ref, y2_ref, y3_ref, ng_ref, m_ref, o_ref, h_ref):
    x = _combined(x_ref, g_ref, (y0_ref, y1_ref, y2_ref, y3_ref))
    o_ref[...] = x
    h_ref[...] = _rms_mod(x, ng_ref[...], m_ref[0:1, :], m_ref[1:2, :]).astype(h_ref.dtype)


def _moe_combine_final_kernel(x_ref, g_ref, y0_ref, y1_ref, y2_ref, y3_ref, ng_ref, o_ref):
    x = _combined(x_ref, g_ref, (y0_ref, y1_ref, y2_ref, y3_ref))
    o_ref[...] = x * lax.rsqrt(jnp.mean(x * x, axis=-1, keepdims=True) + NORM_EPS) * ng_ref[...]


def moe_combine(xs, gate_groups, ysg, next_g, next_modv):
    b, s, d = xs.shape
    assert TOP_K == 4
    tr = CTX_LEN
    nblk = s // tr
    nr = b * nblk
    xs2 = xs.reshape(b * s, d)
    final = next_modv is None
    off = 1 if final else 0

    def row(bi, i):
        return bi * nblk + off + i

    yspec = [pl.BlockSpec((tr, d), functools.partial(lambda bi, i, k: (k * nr + row(bi, i), 0), k=k))
             for k in range(TOP_K)]
    in_specs = [pl.BlockSpec((tr, d), lambda bi, i: (row(bi, i), 0)),
                pl.BlockSpec((1, 1, d), lambda bi, i: (row(bi, i), 0, 0))] + yspec + [
                    pl.BlockSpec((1, d), lambda bi, i: (0, 0))]
    args = [xs2, gate_groups, ysg, ysg, ysg, ysg, next_g.reshape(1, d)]
    if final:
        return pl.pallas_call(
            _moe_combine_final_kernel,
            out_shape=jax.ShapeDtypeStruct((b, s - CTX_LEN, d), F32),
            grid=(b, nblk - 1),
            in_specs=in_specs,
            out_specs=pl.BlockSpec((None, tr, d), lambda bi, i: (bi, i, 0)),
            compiler_params=_params(("parallel", "parallel")),
            name="moe_combine_final_norm",
        )(*args)
    in_specs.append(pl.BlockSpec((None, None, N_MOD, d), lambda bi, i: (bi, jnp.minimum(i, 1), 0, 0)))
    xs2, h = pl.pallas_call(
        _moe_combine_norm_kernel,
        out_shape=(jax.ShapeDtypeStruct((b * s, d), F32), jax.ShapeDtypeStruct((b * s, d), BF16)),
        grid=(b, nblk),
        in_specs=in_specs,
        out_specs=(pl.BlockSpec((tr, d), lambda bi, i: (row(bi, i), 0)),
                   pl.BlockSpec((tr, d), lambda bi, i: (row(bi, i), 0))),
        input_output_aliases={0: 0},
        compiler_params=_params(("parallel", "parallel")),
        name="moe_combine_norm",
    )(*args, next_modv)
    return xs2.reshape(b, s, d), h.reshape(b, s, d)


def moe_layer(xs, gate_groups, h, top_idx, top_gate, layer, w1, b1, w2, b2, next_g, next_modv):
    t_tok = top_idx.shape[1]
    n_e = w1.shape[1]
    n_assign = t_tok * TOP_K
    n_blocks = -(-n_assign // MOE_BLOCK) + n_e
    n_slots = n_blocks * MOE_BLOCK
    n_extra = n_slots - n_assign
    e_flat = top_idx.reshape(n_assign)
    counts = jnp.sum((e_flat[:, None] == jnp.arange(n_e, dtype=jnp.int32)[None, :]).astype(jnp.int32), axis=0)
    padded = (counts + MOE_BLOCK - 1) // MOE_BLOCK * MOE_BLOCK
    n_used = (jnp.sum(padded) // MOE_BLOCK).astype(jnp.int32)
    pad_cum = jnp.cumsum(padded - counts)
    pad_key = jnp.sum((jnp.arange(n_extra, dtype=jnp.int32)[:, None] >= pad_cum[None, :]).astype(jnp.int32), axis=1)
    keys = jnp.concatenate([e_flat, pad_key])
    ids = jnp.arange(n_slots, dtype=jnp.int32)
    tok = jnp.concatenate([jnp.tile(jnp.arange(t_tok, dtype=jnp.int32), TOP_K),
                           jnp.arange(n_extra, dtype=jnp.int32) % t_tok])
    gate = jnp.concatenate([top_gate.reshape(n_assign), jnp.zeros((n_extra,), F32)])
    slot_key, slot_id, slot_tok, slot_gate = lax.sort((keys, ids, tok, gate), dimension=0, is_stable=True, num_keys=1)
    _, slot_of = lax.sort_key_val(slot_id, ids)
    blk_expert = jnp.minimum(slot_key.reshape(n_blocks, MOE_BLOCK)[:, 0], n_e - 1)
    prev = jnp.concatenate([jnp.full((1,), -1, jnp.int32), blk_expert[:-1]])
    blk_first = ((blk_expert != prev) & (jnp.arange(n_blocks) < n_used)).astype(jnp.int32)
    xg = h.at[slot_tok].get(mode="promise_in_bounds")
    ys = moe_experts(xg, blk_expert, blk_first, n_used.reshape(1), slot_gate.reshape(n_slots, 1), layer,
                     w1, b1, w2, b2)
    ysg = ys.at[slot_of[:n_assign]].get(mode="promise_in_bounds")
    return moe_combine(xs, gate_groups, ysg, next_g, next_modv)


def _group_rows(vec_ctx, vec_lat, s):
    b, d = vec_lat.shape
    ng = s // CTX_LEN
    g = jnp.concatenate([jnp.broadcast_to(vec_ctx[None, None, :], (b, 1, d)),
                         jnp.broadcast_to(vec_lat[:, None, :], (b, ng - 1, d))], axis=1)
    return g.reshape(b * ng, 1, d)


def kernel(x, c, ctx, c_ctx, w_mod, b_mod, g_mix, g_ffn, g_final, na_w_qkv, na_w_o, na_rpb, ret_w_in, ret_w_o, ret_decay_exp, gqa_w_qkv, gqa_w_o, gqa_q_gain, gqa_k_gain, swa_w_qkv, swa_w_o, swa_sink, moe_w_router, moe_b_router, moe_w1, moe_b1, moe_w2, moe_b2):
    b, l, d = x.shape
    assert ctx.shape[1] == CTX_LEN and l % CTX_LEN == 0
    s = CTX_LEN + l
    depth = w_mod.shape[0]
    xs = jnp.concatenate([ctx, x], axis=1)

    n_rows = -(-(b + 1) // 16) * 16
    s_rows = jnp.zeros((n_rows, d), F32).at[:b].set(jax.nn.silu(c)).at[b].set(jax.nn.silu(c_ctx)).astype(BF16)

    def layer_mod(li):
        mod = modulation_layer(s_rows, w_mod, b_mod, li).reshape(n_rows, N_MOD, d)
        mod_lat, mod_ctx = mod[:b], mod[b]
        modv = jnp.stack([jnp.broadcast_to(mod_ctx[None], (b, N_MOD, d)), mod_lat], axis=1)
        return mod_lat, mod_ctx, modv

    ones = jnp.ones((1, LANES), F32)
    mods = [layer_mod(li) for li in range(depth)]
    h = norm_modulate(xs, g_mix[0], mods[0][2], 0)
    for li in range(depth):
        mix, j = li % N_MIXERS, li // N_MIXERS
        need_ctx = li < depth - 1
        mod_lat, mod_ctx, modv = mods[li]
        h2 = h.reshape(b * s, d)
        if mix == 0:
            qkv = matmul(h2, na_w_qkv[j]).reshape(b, s, 3 * d)
            o = neighbourhood_attention(qkv, na_rpb[j], NA_HEADS)
            o = ctx_self_attention(o, qkv, 0, qkv, NA_HEADS, qkv, 2 * NA_HEADS, NA_HEADS, 1, LANES ** -0.5 * LOG2E)
            w_o = na_w_o[j]
        elif mix == 1:
            proj = matmul(h2, ret_w_in[j]).reshape(b, s, 5 * d)
            cos, sin = rope_tables(l, d // RET_HEADS)
            y_f, y_b = retention_scan(proj, ret_decay_exp[j], cos, sin, RET_HEADS)
            o = retention_merge(y_f, y_b, proj)
            w_o = ret_w_o[j]
        elif mix == 2:
            dh = d // GQA_HEADS
            assert dh == LANES
            qkv = matmul(h2, gqa_w_qkv[j]).reshape(b, s, -1)
            cos, sin = rope_tables(l, dh)
            qg = (gqa_q_gain[j].astype(F32) * (dh ** -0.5 * LOG2E)).reshape(1, LANES)
            kg = gqa_k_gain[j].astype(F32).reshape(1, LANES)
            qp = qk_prepare(qkv, 0, GQA_HEADS * dh, qg, cos, sin, dh, True)
            kp = qk_prepare(qkv, GQA_HEADS * dh, GQA_KV_HEADS * dh, kg, cos, sin, dh, True)
            o = global_gqa_attention(qp, kp, qkv, GQA_HEADS, GQA_KV_HEADS)
            o = ctx_self_attention(o, qp, 0, kp, 0, qkv, GQA_HEADS + GQA_KV_HEADS, GQA_HEADS,
                                   GQA_HEADS // GQA_KV_HEADS, 1.0)
            w_o = gqa_w_o[j]
        else:
            dh = d // SWA_HEADS
            qkv = matmul(h2, swa_w_qkv[j]).reshape(b, s, -1)
            cos, sin = rope_tables(l, dh)
            reps = LANES // dh
            cos, sin = jnp.tile(cos, (1, reps)), jnp.tile(sin, (1, reps))
            unit = jnp.zeros((s, LANES), F32)
            qp = qk_prepare(qkv, 0, SWA_HEADS * dh, ones * (dh ** -0.5 * LOG2E), cos, sin, dh, False)
            k2 = qk_prepare(qkv, SWA_HEADS * dh, SWA_KV_HEADS * dh, ones, cos, sin, dh, False, dup=True)
            v2 = qk_prepare(qkv, (SWA_HEADS + SWA_KV_HEADS) * dh, SWA_KV_HEADS * dh, ones,
                            jnp.ones((s, LANES), F32), unit, dh, False, dup=True)
            o = window_gqa_attention(qp, k2, v2, swa_sink[j].astype(F32) * LOG2E, SWA_HEADS, SWA_KV_HEADS, dh)
            w_o = swa_w_o[j]
        if mix in (0, 2) and not need_ctx:
            o = o.at[:, :CTX_LEN].set(0)
        xs = matmul_residual(o.reshape(b * s, d), w_o, xs.reshape(b * s, d),
                             _group_rows(mod_ctx[2], mod_lat[:, 2], s)).reshape(b, s, d)

        hf, top_idx, top_gate = norm_modulate_route(xs, g_ffn[li], modv, 3, moe_w_router[li], moe_b_router[li])
        top_idx = top_idx[:, :TOP_K].transpose(1, 0, 2).reshape(TOP_K, b * s)
        top_gate = top_gate[:, :TOP_K].transpose(1, 0, 2).reshape(TOP_K, b * s)
        last = li == depth - 1
        out = moe_layer(xs, _group_rows(mod_ctx[5], mod_lat[:, 5], s), hf.reshape(b * s, d), top_idx, top_gate,
                        li, moe_w1, moe_b1, moe_w2, moe_b2,
                        g_final if last else g_mix[li + 1], None if last else mods[li + 1][2])
        if last:
            return out
        xs, h = out
```

```python
import functools

import jax
import jax.numpy as jnp
import numpy as np
from jax import lax
from jax.experimental import pallas as pl
from jax.experimental.pallas import tpu as pltpu

DEPTH = 4
GRID_W = 64
CTX_LEN = 256
N_MIXERS = 4
N_MOD = 6
NORM_EPS = 1e-6
NEG_INF = -1e30
ROPE_THETA = 10000.0

NA_HEADS = 32
NA_KH = 8
NA_KW = 16
NA_ROW_BLOCK = 4
NA_KEY_ROWS = 12
NA_HEADS_PER_STEP = 4

RET_HEADS = 16
RET_CHUNK = 128
RET_HEADS_PER_STEP = 8

GQA_HEADS = 32
GQA_KV_HEADS = 8

SWA_HEADS = 64
SWA_KV_HEADS = 8
SWA_WINDOW = 128
SWA_BLOCK = 128
SWA_BLOCKS_PER_STEP = 2
SWA_GROUPS_PER_CHAIN = 2

N_EXPERTS = 32
TOP_K = 4
SWIGLU_ALPHA = 1.702
SWIGLU_LIMIT = 7.0
MOE_BLOCK = 256

LOG2E = 1.4426950408889634
LANES = 128
SUBLANES = 8
VMEM_LIMIT_BYTES = 56 * 1024 * 1024

F32 = jnp.float32
BF16 = jnp.bfloat16


def _params(semantics, vmem=VMEM_LIMIT_BYTES):
    return pltpu.CompilerParams(dimension_semantics=semantics, vmem_limit_bytes=vmem)


def _pick(n, candidates):
    for c in candidates:
        if n % c == 0:
            return c
    return n


def _mod_kernel(s_ref, w_ref, b_ref, o_ref):
    acc = jnp.dot(s_ref[...], w_ref[...].astype(BF16), preferred_element_type=F32)
    o_ref[...] = acc + b_ref[...]


def modulation_layer(s_rows, w_mod, b_mod, layer):
    depth, d, n = w_mod.shape
    rows = s_rows.shape[0]
    bn = _pick(n, (512, 256, 128))
    return pl.pallas_call(
        _mod_kernel,
        out_shape=jax.ShapeDtypeStruct((rows, n), F32),
        grid=(n // bn,),
        in_specs=[
            pl.BlockSpec((rows, d), lambda j: (0, 0)),
            pl.BlockSpec((None, d, bn), lambda j: (layer, 0, j)),
            pl.BlockSpec((None, 1, bn), lambda j: (layer, 0, j)),
        ],
        out_specs=pl.BlockSpec((rows, bn), lambda j: (0, j)),
        compiler_params=_params(("parallel",)),
        name="modulation",
    )(s_rows, w_mod, b_mod.reshape(depth, 1, n))


def _rms_mod(x, g, shift, scale):
    y = x * lax.rsqrt(jnp.mean(x * x, axis=-1, keepdims=True) + NORM_EPS) * g
    return y * (1.0 + scale) + shift


def _norm_mod_kernel(x_ref, g_ref, m_ref, o_ref, *, shift_idx):
    h = _rms_mod(x_ref[...], g_ref[...], m_ref[shift_idx:shift_idx + 1, :], m_ref[shift_idx + 1:shift_idx + 2, :])
    o_ref[...] = h.astype(o_ref.dtype)


def _norm_route_kernel(x_ref, g_ref, m_ref, wh_ref, wl_ref, br_ref, o_ref, idx_ref, gate_ref, *, shift_idx):
    h = _rms_mod(x_ref[...], g_ref[...], m_ref[shift_idx:shift_idx + 1, :], m_ref[shift_idx + 1:shift_idx + 2, :])
    h_hi = h.astype(BF16)
    o_ref[...] = h_hi.astype(o_ref.dtype)
    h_lo = (h - h_hi.astype(F32)).astype(BF16)
    logits = (jnp.dot(h_hi, wh_ref[...], preferred_element_type=F32)
              + (jnp.dot(h_lo, wh_ref[...], preferred_element_type=F32)
                 + jnp.dot(h_hi, wl_ref[...], preferred_element_type=F32))) + br_ref[...]
    rows, n_e = logits.shape
    e_iota = lax.broadcasted_iota(jnp.int32, (rows, n_e), 1).astype(F32)
    lane = lax.broadcasted_iota(jnp.int32, (rows, LANES), 1)
    idx_out = jnp.zeros((rows, LANES), F32)
    val_out = jnp.zeros((rows, LANES), F32)
    top0 = None
    denom = jnp.zeros((rows, 1), F32)
    work = logits
    for k in range(TOP_K):
        mx = jnp.max(work, axis=-1, keepdims=True)
        ix = jnp.min(jnp.where(work == mx, e_iota, float(n_e)), axis=-1, keepdims=True)
        if k == 0:
            top0 = mx
        ex = jnp.exp(mx - top0)
        denom = denom + ex
        idx_out = jnp.where(lane == k, ix, idx_out)
        val_out = jnp.where(lane == k, ex, val_out)
        work = jnp.where(e_iota == ix, -jnp.inf, work)
    idx_ref[...] = idx_out.T[0:SUBLANES, :].astype(jnp.int32)
    gate_ref[...] = (val_out / denom).T[0:SUBLANES, :]


def norm_modulate(xs, g, modv, shift_idx):
    b, s, d = xs.shape
    tr = CTX_LEN
    nblk = s // tr
    return pl.pallas_call(
        functools.partial(_norm_mod_kernel, shift_idx=shift_idx),
        out_shape=jax.ShapeDtypeStruct((b, s, d), BF16),
        grid=(b, nblk),
        in_specs=[
            pl.BlockSpec((None, tr, d), lambda bi, i: (bi, i, 0)),
            pl.BlockSpec((1, d), lambda bi, i: (0, 0)),
            pl.BlockSpec((None, None, N_MOD, d), lambda bi, i: (bi, jnp.minimum(i, 1), 0, 0)),
        ],
        out_specs=pl.BlockSpec((None, tr, d), lambda bi, i: (bi, i, 0)),
        compiler_params=_params(("parallel", "parallel")),
        name="norm_modulate",
    )(xs, g.reshape(1, d), modv)


def norm_modulate_route(xs, g, modv, shift_idx, w_router, b_router):
    b, s, d = xs.shape
    tr = CTX_LEN
    nblk = s // tr
    n_e = w_router.shape[1]
    w_hi = w_router.astype(BF16)
    w_lo = (w_router.astype(F32) - w_hi.astype(F32)).astype(BF16)
    return pl.pallas_call(
        functools.partial(_norm_route_kernel, shift_idx=shift_idx),
        out_shape=(jax.ShapeDtypeStruct((b, s, d), BF16),
                   jax.ShapeDtypeStruct((b, SUBLANES, s), jnp.int32),
                   jax.ShapeDtypeStruct((b, SUBLANES, s), F32)),
        grid=(b, nblk),
        in_specs=[
            pl.BlockSpec((None, tr, d), lambda bi, i: (bi, i, 0)),
            pl.BlockSpec((1, d), lambda bi, i: (0, 0)),
            pl.BlockSpec((None, None, N_MOD, d), lambda bi, i: (bi, jnp.minimum(i, 1), 0, 0)),
            pl.BlockSpec((d, n_e), lambda bi, i: (0, 0)),
            pl.BlockSpec((d, n_e), lambda bi, i: (0, 0)),
            pl.BlockSpec((1, n_e), lambda bi, i: (0, 0)),
        ],
        out_specs=(pl.BlockSpec((None, tr, d), lambda bi, i: (bi, i, 0)),
                   pl.BlockSpec((None, SUBLANES, tr), lambda bi, i: (bi, 0, i)),
                   pl.BlockSpec((None, SUBLANES, tr), lambda bi, i: (bi, 0, i))),
        compiler_params=_params(("parallel", "parallel")),
        name="norm_modulate_route",
    )(xs, g.reshape(1, d), modv, w_hi, w_lo, b_router.astype(F32).reshape(1, n_e))


def _mm_kernel(x_ref, w_ref, o_ref, wb_sc):
    @pl.when(pl.program_id(1) == 0)
    def _():
        wb_sc[...] = w_ref[...].astype(BF16)

    o_ref[...] = jnp.dot(x_ref[...], wb_sc[...], preferred_element_type=F32).astype(o_ref.dtype)


def _mm_residual_kernel(x_ref, w_ref, r_ref, g_ref, o_ref, wb_sc, *, groups, group_rows):
    @pl.when(pl.program_id(1) == 0)
    def _():
        wb_sc[...] = w_ref[...].astype(BF16)

    acc = jnp.dot(x_ref[...], wb_sc[...], preferred_element_type=F32)
    for gi in range(groups):
        rows = slice(gi * group_rows, (gi + 1) * group_rows)
        o_ref[rows, :] = r_ref[rows, :] + g_ref[gi] * acc[rows, :]


def matmul(x, w):
    m, k = x.shape
    n = w.shape[1]
    bm = _pick(m, (1536, 768, 512, 256, 128))
    bn = _pick(n, (512, 256, 128))
    return pl.pallas_call(
        _mm_kernel,
        out_shape=jax.ShapeDtypeStruct((m, n), BF16),
        grid=(n // bn, m // bm),
        in_specs=[pl.BlockSpec((bm, k), lambda j, i: (i, 0)),
                  pl.BlockSpec((k, bn), lambda j, i: (0, j))],
        out_specs=pl.BlockSpec((bm, bn), lambda j, i: (i, j)),
        scratch_shapes=[pltpu.VMEM((k, bn), BF16)],
        compiler_params=_params(("parallel", "arbitrary")),
        name="projection",
    )(x, w)


def matmul_residual(x, w, res, gate_groups):
    m, k = x.shape
    n = w.shape[1]
    bm = _pick(m, (768, 512, 256))
    bn = _pick(n, (512, 256, 128))
    groups = bm // CTX_LEN
    return pl.pallas_call(
        functools.partial(_mm_residual_kernel, groups=groups, group_rows=CTX_LEN),
        out_shape=jax.ShapeDtypeStruct((m, n), F32),
        grid=(n // bn, m // bm),
        in_specs=[pl.BlockSpec((bm, k), lambda j, i: (i, 0)),
                  pl.BlockSpec((k, bn), lambda j, i: (0, j)),
                  pl.BlockSpec((bm, bn), lambda j, i: (i, j)),
                  pl.BlockSpec((groups, 1, bn), lambda j, i: (i, 0, j))],
        out_specs=pl.BlockSpec((bm, bn), lambda j, i: (i, j)),
        scratch_shapes=[pltpu.VMEM((k, bn), BF16)],
        input_output_aliases={2: 0},
        compiler_params=_params(("parallel", "arbitrary")),
        name="out_projection_residual",
    )(x, w, res, gate_groups)


def rope_tables(seq, head_dim):
    half = head_dim // 2
    n = half // 2
    t = jnp.arange(seq)
    rows, cols = (t // GRID_W).astype(F32), (t % GRID_W).astype(F32)
    freqs = ROPE_THETA ** (-jnp.arange(n, dtype=F32) / n)
    ang = jnp.concatenate([rows[:, None] * freqs, cols[:, None] * freqs], axis=-1)
    cos, sin = jnp.cos(ang), jnp.sin(ang)
    cos = jnp.concatenate([jnp.ones((CTX_LEN, half), F32), cos], axis=0)
    sin = jnp.concatenate([jnp.zeros((CTX_LEN, half), F32), sin], axis=0)
    return jnp.concatenate([cos, cos], axis=-1), jnp.concatenate([-sin, sin], axis=-1)


def _swap_halves(y, head_dim):
    half = head_dim // 2
    if head_dim == LANES:
        return pltpu.roll(y, half, 1)
    lane = lax.broadcasted_iota(jnp.int32, y.shape, 1)
    first = (lane % head_dim) < half
    return jnp.where(first, pltpu.roll(y, LANES - half, 1), pltpu.roll(y, half, 1))


def _qk_prep_kernel(x_ref, gain_ref, cos_ref, sin_ref, o_ref, *, head_dim, normalise, dup):
    x = x_ref[...].astype(F32)
    cols = x.shape[1]
    cos, sin = cos_ref[...], sin_ref[...]
    gain = gain_ref[...]
    for c in range(cols // LANES):
        y = x[:, c * LANES:(c + 1) * LANES]
        if normalise:
            y = y * lax.rsqrt(jnp.mean(y * y, axis=-1, keepdims=True) + NORM_EPS)
        y = y * gain
        y = (y * cos + _swap_halves(y, head_dim) * sin).astype(o_ref.dtype)
        if dup:
            lane = lax.broadcasted_iota(jnp.int32, y.shape, 1)
            other = pltpu.roll(y.astype(F32), head_dim, 1).astype(o_ref.dtype)
            o_ref[:, (2 * c) * LANES:(2 * c + 1) * LANES] = jnp.where(lane < head_dim, y, other)
            o_ref[:, (2 * c + 1) * LANES:(2 * c + 2) * LANES] = jnp.where(lane < head_dim, other, y)
        else:
            o_ref[:, c * LANES:(c + 1) * LANES] = y


def qk_prepare(qkv, col0, ncols, gain, cos, sin, head_dim, normalise, dup=False):
    b, s, _ = qkv.shape
    tr = _pick(s, (1408, 1024, 768, 512, 256))
    bc = _pick(ncols, (512, 256, 128))
    c0 = col0 // bc
    assert col0 % bc == 0
    mult = 2 if dup else 1
    return pl.pallas_call(
        functools.partial(_qk_prep_kernel, head_dim=head_dim, normalise=normalise, dup=dup),
        out_shape=jax.ShapeDtypeStruct((b, s, ncols * mult), BF16),
        grid=(b, s // tr, ncols // bc),
        in_specs=[pl.BlockSpec((None, tr, bc), lambda bi, i, j: (bi, i, c0 + j)),
                  pl.BlockSpec((1, LANES), lambda bi, i, j: (0, 0)),
                  pl.BlockSpec((tr, LANES), lambda bi, i, j: (i, 0)),
                  pl.BlockSpec((tr, LANES), lambda bi, i, j: (i, 0))],
        out_specs=pl.BlockSpec((None, tr, bc * mult), lambda bi, i, j: (bi, i, j)),
        compiler_params=_params(("parallel", "parallel", "parallel")),
        name="qk_prepare",
    )(qkv, gain, cos, sin)


def _dot_nt(a, b):
    return lax.dot_general(a, b, (((1,), (1,)), ((), ())), preferred_element_type=F32)


def _ctx_attn_kernel(q_ref, k_ref, v_ref, buf_ref, o_ref, *, scale):
    del buf_ref
    s = _dot_nt(q_ref[...], k_ref[...]) * scale
    m = jnp.max(s, axis=-1, keepdims=True)
    p = jnp.exp2(s - m)
    l = jnp.sum(p, axis=-1, keepdims=True)
    o = jnp.dot(p.astype(BF16), v_ref[...], preferred_element_type=F32) / l
    o_ref[...] = o.astype(o_ref.dtype)


def ctx_self_attention(o_buf, q_arr, q_c0, k_arr, k_c0, v_arr, v_c0, n_heads, group, scale):
    b, s, dm = o_buf.shape
    return pl.pallas_call(
        functools.partial(_ctx_attn_kernel, scale=scale),
        out_shape=jax.ShapeDtypeStruct(o_buf.shape, o_buf.dtype),
        grid=(b, n_heads),
        in_specs=[pl.BlockSpec((None, CTX_LEN, LANES), lambda bi, h: (bi, 0, q_c0 + h)),
                  pl.BlockSpec((None, CTX_LEN, LANES), lambda bi, h: (bi, 0, k_c0 + h // group)),
                  pl.BlockSpec((None, CTX_LEN, LANES), lambda bi, h: (bi, 0, v_c0 + h // group)),
                  pl.BlockSpec(memory_space=pl.ANY)],
        out_specs=pl.BlockSpec((None, CTX_LEN, LANES), lambda bi, h: (bi, 0, h)),
        input_output_aliases={3: 0},
        compiler_params=_params(("parallel", "parallel")),
        name="ctx_self_attention",
    )(q_arr, k_arr, v_arr, o_buf)


def _na_kernel(q_ref, k_ref, v_ref, b_ref, o_ref, *, scale, n_rows):
    blk = pl.program_id(2)
    base = jnp.clip(NA_ROW_BLOCK * blk - NA_KH // 2, 0, n_rows - NA_KEY_ROWS)
    start = pl.multiple_of(CTX_LEN + base * GRID_W, GRID_W)
    nwin = NA_KEY_ROWS * GRID_W
    ones_w, ones_c = jnp.ones((nwin, LANES), BF16), jnp.ones((CTX_LEN, LANES), BF16)
    for hh in range(q_ref.shape[1] // LANES):
        cols = slice(hh * LANES, (hh + 1) * LANES)
        q = q_ref[:, cols]
        s_w = _dot_nt(q, k_ref[pl.ds(start, nwin), cols]) * scale + b_ref[hh]
        s_c = _dot_nt(q, k_ref[0:CTX_LEN, cols]) * scale
        m = jnp.maximum(jnp.max(s_w, axis=-1, keepdims=True), jnp.max(s_c, axis=-1, keepdims=True))
        v_w = jnp.concatenate([v_ref[pl.ds(start, nwin), cols], ones_w], axis=1)
        v_c = jnp.concatenate([v_ref[0:CTX_LEN, cols], ones_c], axis=1)
        pv = (jnp.dot(jnp.exp2(s_w - m).astype(BF16), v_w, preferred_element_type=F32)
              + jnp.dot(jnp.exp2(s_c - m).astype(BF16), v_c, preferred_element_type=F32))
        o_ref[:, cols] = (pv[:, :LANES] / pv[:, LANES:]).astype(o_ref.dtype)


def _na_bias_tables(rpb, n_rows):
    rb, kr, w = NA_ROW_BLOCK, NA_KEY_ROWS, GRID_W
    kh = min(NA_KH, n_rows)
    nblk = n_rows // rb
    n_h, n_ri, n_ci = rpb.shape
    qc = np.arange(w)
    c0 = np.clip(qc - NA_KW // 2, 0, w - NA_KW)
    kc = np.arange(w)
    col_ok = (kc[None, :] >= c0[:, None]) & (kc[None, :] < c0[:, None] + NA_KW)
    col_idx = np.clip(kc[None, :] - qc[:, None], 1 - NA_KW, NA_KW - 1) + NA_KW - 1
    onehot = (col_idx.reshape(1, w * w) == np.arange(n_ci)[:, None]).astype(np.float32)
    by_col = jnp.dot(rpb.astype(F32).reshape(n_h * n_ri, n_ci) * LOG2E, jnp.asarray(onehot),
                     precision=lax.Precision.HIGHEST).reshape(n_h, n_ri, w, w)
    cases = []
    for blk in range(nblk):
        base = int(np.clip(rb * blk - NA_KH // 2, 0, n_rows - kr))
        r = rb * blk + np.arange(rb)
        r0 = np.clip(r - kh // 2, 0, n_rows - kh)
        krow = base + np.arange(kr)
        row_ok = (krow[None, :] >= r0[:, None]) & (krow[None, :] < r0[:, None] + kh)
        row_idx = np.clip(krow[None, :] - r[:, None] + NA_KH - 1, 0, 2 * NA_KH - 2)
        cases.append((base - rb * blk, row_ok, row_idx))
    for blk in range(2, nblk - 1):
        assert cases[blk][0] == cases[1][0] and all(np.array_equal(cases[blk][i], cases[1][i]) for i in (1, 2))
    out = []
    for blk in (0, 1, nblk - 1):
        _, row_ok, row_idx = cases[blk]
        tab = jnp.stack([jnp.stack([by_col[:, int(row_idx[a, b2])] for b2 in range(kr)], axis=2)
                         for a in range(rb)], axis=1)
        ok = row_ok[:, None, :, None] & col_ok[None, :, None, :]
        out.append(jnp.where(ok[None], tab, NEG_INF).reshape(n_h, rb * w, kr * w))
    return jnp.stack(out, axis=0)


def neighbourhood_attention(qkv, rpb, n_heads):
    b, s, _ = qkv.shape
    n_rows = (s - CTX_LEN) // GRID_W
    nblk = n_rows // NA_ROW_BLOCK
    tq = NA_ROW_BLOCK * GRID_W
    assert tq == CTX_LEN
    nwin = NA_KEY_ROWS * GRID_W
    bias = _na_bias_tables(rpb, n_rows)
    scale = LANES ** -0.5 * LOG2E

    hps = _pick(n_heads, (NA_HEADS_PER_STEP, 1))
    ng = n_heads // hps
    wide = hps * LANES

    def bias_case(blk):
        return jnp.where(blk == 0, 0, jnp.where(blk == nblk - 1, 2, 1))

    return pl.pallas_call(
        functools.partial(_na_kernel, scale=scale, n_rows=n_rows),
        out_shape=jax.ShapeDtypeStruct((b, s, n_heads * LANES), BF16),
        grid=(b, ng, nblk),
        in_specs=[pl.BlockSpec((None, tq, wide), lambda bi, h, i: (bi, i + 1, h)),
                  pl.BlockSpec((None, s, wide), lambda bi, h, i: (bi, 0, ng + h)),
                  pl.BlockSpec((None, s, wide), lambda bi, h, i: (bi, 0, 2 * ng + h)),
                  pl.BlockSpec((None, hps, tq, nwin), lambda bi, h, i: (bias_case(i), h, 0, 0))],
        out_specs=pl.BlockSpec((None, tq, wide), lambda bi, h, i: (bi, i + 1, h)),
        compiler_params=_params(("parallel", "parallel", "arbitrary")),
        name="neighbourhood_attention",
    )(qkv, qkv, qkv, bias)


def _gqa_kernel(q_ref, k_ref, v_ref, o_ref, m_sc, acc_sc, *, tk, group):
    m_sc[...] = jnp.full(m_sc.shape, -jnp.inf, F32)
    acc_sc[...] = jnp.zeros(acc_sc.shape, F32)
    ones = jnp.ones((tk, LANES), BF16)

    def body(c, carry):
        rows = pl.ds(pl.multiple_of(c * tk, tk), tk)
        kc = k_ref[rows, :]
        vc = jnp.concatenate([v_ref[rows, :], ones], axis=1)
        for g in range(group):
            s = _dot_nt(q_ref[:, g * LANES:(g + 1) * LANES], kc)
            m_prev = m_sc[g]
            m_new = jnp.maximum(m_prev, jnp.max(s, axis=-1, keepdims=True))
            alpha = jnp.exp2(m_prev - m_new)
            p = jnp.exp2(s - m_new[:, :1])
            pv = jnp.dot(p.astype(BF16), vc, preferred_element_type=F32)
            acc_sc[g] = jnp.concatenate([alpha, alpha], axis=1) * acc_sc[g] + pv
            m_sc[g] = m_new
        return carry

    lax.fori_loop(0, k_ref.shape[0] // tk, body, 0, unroll=True)
    for g in range(group):
        a = acc_sc[g]
        o_ref[:, g * LANES:(g + 1) * LANES] = (a[:, :LANES] / a[:, LANES:]).astype(o_ref.dtype)


def global_gqa_attention(qp, kp, qkv, n_heads, n_kv):
    b, s, _ = qp.shape
    group = n_heads // n_kv
    tq = CTX_LEN
    nq = s // tq - 1
    tk = _pick(s, (768, 512, 256))
    v_c0 = n_heads + n_kv
    return pl.pallas_call(
        functools.partial(_gqa_kernel, tk=tk, group=group),
        out_shape=jax.ShapeDtypeStruct((b, s, n_heads * LANES), BF16),
        grid=(b, n_kv, nq),
        in_specs=[pl.BlockSpec((None, tq, group * LANES), lambda bi, h, i: (bi, i + 1, h)),
                  pl.BlockSpec((None, s, LANES), lambda bi, h, i: (bi, 0, h)),
                  pl.BlockSpec((None, s, LANES), lambda bi, h, i: (bi, 0, v_c0 + h))],
        out_specs=pl.BlockSpec((None, tq, group * LANES), lambda bi, h, i: (bi, i + 1, h)),
        scratch_shapes=[pltpu.VMEM((group, tq, LANES), F32), pltpu.VMEM((group, tq, 2 * LANES), F32)],
        compiler_params=_params(("parallel", "parallel", "arbitrary")),
        name="global_gqa_attention",
    )(qp, kp, qkv)


def _swa_kernel(q_ref, k_ref, v_ref, sink_ref, cos_ref, sin_ref, o_ref, *, head_dim, n_lat, q_scale):
    i = pl.program_id(2)
    tq = SWA_BLOCK
    sub_blocks = q_ref.shape[0] // tq
    n_ctx_steps = CTX_LEN // q_ref.shape[0]
    band = SWA_BLOCK + 2 * SWA_WINDOW

    @pl.when(i < n_ctx_steps)
    def _():
        o_ref[...] = jnp.zeros(o_ref.shape, o_ref.dtype)

    def query_block(sb):
        rows = slice(sb * tq, (sb + 1) * tq)
        qb = (i - n_ctx_steps) * sub_blocks + sb
        row0 = jnp.clip(CTX_LEN + SWA_BLOCK * qb - SWA_WINDOW, 0, k_ref.shape[0] - band)
        row0 = pl.multiple_of(row0, SWA_BLOCK)
        nk = band + CTX_LEN
        k_all = jnp.concatenate([k_ref[pl.ds(row0, band), :], k_ref[0:CTX_LEN, :]], axis=0)
        v_all = jnp.concatenate([jnp.concatenate([v_ref[pl.ds(row0, band), :], v_ref[0:CTX_LEN, :]], axis=0),
                                 jnp.ones((nk, LANES), BF16)], axis=1)
        qpos = SWA_BLOCK * qb + lax.broadcasted_iota(jnp.int32, (tq, nk), 0)
        col = lax.broadcasted_iota(jnp.int32, (tq, nk), 1)
        kpos = row0 - CTX_LEN + col
        ok = (col >= band) | ((kpos >= 0) & (kpos < n_lat) & (jnp.abs(kpos - qpos) <= SWA_WINDOW))
        hpg = LANES // head_dim
        lane = lax.broadcasted_iota(jnp.int32, (tq, LANES), 1)
        sels = [(lane >= j * head_dim) & (lane < (j + 1) * head_dim) for j in range(hpg)]
        cos, sin = cos_ref[rows, :], sin_ref[rows, :]
        n_groups = q_ref.shape[1] // LANES
        q_groups = []
        for c in range(n_groups):
            y = q_ref[rows, c * LANES:(c + 1) * LANES].astype(F32) * q_scale
            q_groups.append((y * cos + _swap_halves(y, head_dim) * sin).astype(BF16))
        gpc = min(SWA_GROUPS_PER_CHAIN, n_groups)
        ok = jnp.concatenate([ok] * (hpg * gpc), axis=0)
        for c0 in range(0, n_groups, gpc):
            heads = [(c, j) for c in range(c0, c0 + gpc) for j in range(hpg)]
            qs = jnp.concatenate([jnp.where(sels[j], q_groups[c], jnp.zeros((tq, LANES), BF16))
                                  for c, j in heads], axis=0)
            sink = jnp.concatenate([jnp.broadcast_to(sink_ref[:, c * hpg + j:c * hpg + j + 1], (tq, 1))
                                    for c, j in heads], axis=0)
            s = jnp.where(ok, _dot_nt(qs, k_all), NEG_INF)
            m = jnp.maximum(jnp.max(s, axis=-1, keepdims=True), sink)
            pv = jnp.dot(jnp.exp2(s - m).astype(BF16), v_all, preferred_element_type=F32)
            o = pv[:, :LANES] / (pv[:, LANES:] + jnp.exp2(sink - m))
            for ci, c in enumerate(range(c0, c0 + gpc)):
                res = o[ci * hpg * tq:(ci * hpg + 1) * tq]
                for j in range(1, hpg):
                    res = jnp.where(sels[j], o[(ci * hpg + j) * tq:(ci * hpg + j + 1) * tq], res)
                o_ref[rows, c * LANES:(c + 1) * LANES] = res.astype(o_ref.dtype)

    @pl.when(i >= n_ctx_steps)
    def _():
        for sb in range(sub_blocks):
            query_block(sb)


def window_gqa_attention(q, k2, v2, sink, cos, sin, q_scale, n_heads, n_kv, head_dim):
    b, s, _ = q.shape
    group = n_heads // n_kv
    gcols = group * head_dim
    assert gcols % LANES == 0 and LANES % head_dim == 0
    tq = SWA_BLOCK * SWA_BLOCKS_PER_STEP
    assert CTX_LEN % tq == 0 and s % tq == 0
    return pl.pallas_call(
        functools.partial(_swa_kernel, head_dim=head_dim, n_lat=s - CTX_LEN, q_scale=q_scale),
        out_shape=jax.ShapeDtypeStruct((b, s, n_heads * head_dim), BF16),
        grid=(b, n_kv, s // tq),
        in_specs=[pl.BlockSpec((None, tq, gcols), lambda bi, h, i: (bi, i, h)),
                  pl.BlockSpec((None, s, LANES), lambda bi, h, i: (bi, 0, h)),
                  pl.BlockSpec((None, s, LANES), lambda bi, h, i: (bi, 0, h)),
                  pl.BlockSpec((None, 1, group), lambda bi, h, i: (h, 0, 0)),
                  pl.BlockSpec((tq, LANES), lambda bi, h, i: (i, 0)),
                  pl.BlockSpec((tq, LANES), lambda bi, h, i: (i, 0))],
        out_specs=pl.BlockSpec((None, tq, gcols), lambda bi, h, i: (bi, i, h)),
        compiler_params=_params(("parallel", "parallel", "arbitrary")),
        name="window_gqa_attention",
    )(q, k2, v2, sink.astype(F32).reshape(n_kv, 1, group), cos, sin)


def _retention_kernel(qf_ref, kf_ref, vf_ref, cosf_ref, sinf_ref, qb_ref, kb_ref, vb_ref, cosb_ref, sinb_ref,
                      intra_ref, qdec_ref, kdec_ref, cdec_ref, of_ref, ob_ref, state_sc, *, k_scale, heads, dh):
    @pl.when(pl.program_id(2) == 0)
    def _():
        state_sc[...] = jnp.zeros(state_sc.shape, F32)

    half = dh // 2

    def rope(x, cos, sin):
        swapped = jnp.concatenate([x[:, half:], x[:, :half]], axis=1)
        return x * cos + swapped * sin

    dirs = ((qf_ref, kf_ref, vf_ref, cosf_ref, sinf_ref, of_ref), (qb_ref, kb_ref, vb_ref, cosb_ref, sinb_ref, ob_ref))
    for dr, (q_ref, k_ref, v_ref, cos_ref, sin_ref, o_ref) in enumerate(dirs):
        cos, sin = cos_ref[...], sin_ref[...]
        for hh in range(heads):
            cols = slice(hh * dh, (hh + 1) * dh)
            q = rope(q_ref[:, cols].astype(F32), cos, sin)
            k = rope(k_ref[:, cols].astype(F32), cos, sin) * k_scale
            v = v_ref[:, cols]
            state = state_sc[dr, hh]
            att = _dot_nt(q.astype(BF16), k.astype(BF16)) * intra_ref[dr, hh]
            y = (jnp.dot(att.astype(BF16), v, preferred_element_type=F32)
                 + jnp.dot((q * qdec_ref[dr, hh]).astype(BF16), state.astype(BF16), preferred_element_type=F32))
            kd = (k * kdec_ref[dr, hh]).astype(BF16)
            state_sc[dr, hh] = state * cdec_ref[dr, hh] + lax.dot_general(kd, v, (((0,), (0,)), ((), ())),
                                                                         preferred_element_type=F32)
            y = y * lax.rsqrt(jnp.mean(y * y, axis=-1, keepdims=True) + NORM_EPS)
            o_ref[:, cols] = y.astype(o_ref.dtype)


def retention_scan(proj, decay_exp, cos, sin, n_heads):
    b, s, five_d = proj.shape
    d = five_d // 5
    dh = d // n_heads
    c = RET_CHUNK
    n_chunks = s // c
    n_ctx_chunks = CTX_LEN // c
    hps = min(RET_HEADS_PER_STEP, n_heads)
    lg = jnp.log1p(-jnp.exp2(-decay_exp.astype(F32)))
    pos = jnp.arange(c, dtype=F32)
    diff = pos[:, None] - pos[None, :]
    lgf, lgb = lg[0][:, None, None], lg[1][:, None, None]
    intra_f = jnp.where(diff >= 0, jnp.exp(lgf * jnp.maximum(diff, 0.0)), 0.0)
    intra_b = jnp.where(diff <= 0, jnp.exp(lgb * jnp.maximum(-diff, 0.0)), 0.0)
    intra = jnp.stack([intra_f, intra_b])
    qdec = jnp.stack([jnp.exp(lg[0][:, None] * (pos + 1.0)), jnp.exp(lg[1][:, None] * (c - pos))])[..., None]
    kdec = jnp.stack([jnp.exp(lg[0][:, None] * (c - 1.0 - pos)), jnp.exp(lg[1][:, None] * pos)])[..., None]
    cdec = jnp.exp(lg * c)[..., None, None]
    ncb = n_heads // hps

    def bwd_chunk(t):
        return jnp.where(t < n_ctx_chunks, n_ctx_chunks - 1 - t, n_chunks - 1 - (t - n_ctx_chunks))

    fwd = lambda off: (lambda bi, h, t: (bi, t, off * ncb + h))
    bwd = lambda off: (lambda bi, h, t: (bi, bwd_chunk(t), off * ncb + h))
    tab = lambda bi, h, t: (0, h, 0, 0)
    blk = (None, c, hps * dh)
    return pl.pallas_call(
        functools.partial(_retention_kernel, k_scale=dh ** -0.5, heads=hps, dh=dh),
        out_shape=(jax.ShapeDtypeStruct((b, s, d), BF16), jax.ShapeDtypeStruct((b, s, d), BF16)),
        grid=(b, ncb, n_chunks),
        in_specs=[pl.BlockSpec(blk, fwd(0)), pl.BlockSpec(blk, fwd(1)), pl.BlockSpec(blk, fwd(2)),
                  pl.BlockSpec((c, dh), lambda bi, h, t: (t, 0)), pl.BlockSpec((c, dh), lambda bi, h, t: (t, 0)),
                  pl.BlockSpec(blk, bwd(0)), pl.BlockSpec(blk, bwd(1)), pl.BlockSpec(blk, bwd(2)),
                  pl.BlockSpec((c, dh), lambda bi, h, t: (bwd_chunk(t), 0)),
                  pl.BlockSpec((c, dh), lambda bi, h, t: (bwd_chunk(t), 0)),
                  pl.BlockSpec((2, hps, c, c), tab),
                  pl.BlockSpec((2, hps, c, 1), tab),
                  pl.BlockSpec((2, hps, c, 1), tab),
                  pl.BlockSpec((2, hps, 1, 1), tab)],
        out_specs=(pl.BlockSpec(blk, lambda bi, h, t: (bi, t, h)),
                   pl.BlockSpec(blk, lambda bi, h, t: (bi, bwd_chunk(t), h))),
        scratch_shapes=[pltpu.VMEM((2, hps, dh, dh), F32)],
        compiler_params=_params(("parallel", "parallel", "arbitrary")),
        name="retention_scan",
    )(proj, proj, proj, cos, sin, proj, proj, proj, cos, sin, intra, qdec, kdec, cdec)


def _ret_merge_kernel(yf_ref, yb_ref, gf_ref, gb_ref, o_ref):
    gf = gf_ref[...].astype(F32)
    gb = gb_ref[...].astype(F32)
    o = gf * jax.nn.sigmoid(gf) * yf_ref[...].astype(F32) + gb * jax.nn.sigmoid(gb) * yb_ref[...].astype(F32)
    o_ref[...] = o.astype(o_ref.dtype)


def retention_merge(y_f, y_b, proj):
    b, s, d = y_f.shape
    tr = _pick(s, (1408, 1024, 768, 512, 256))
    bc = _pick(d, (1024, 512, 256, 128))
    nc = d // bc
    return pl.pallas_call(
        _ret_merge_kernel,
        out_shape=jax.ShapeDtypeStruct((b, s, d), BF16),
        grid=(b, s // tr, nc),
        in_specs=[pl.BlockSpec((None, tr, bc), lambda bi, i, j: (bi, i, j)),
                  pl.BlockSpec((None, tr, bc), lambda bi, i, j: (bi, i, j)),
                  pl.BlockSpec((None, tr, bc), lambda bi, i, j: (bi, i, 3 * nc + j)),
                  pl.BlockSpec((None, tr, bc), lambda bi, i, j: (bi, i, 4 * nc + j))],
        out_specs=pl.BlockSpec((None, tr, bc), lambda bi, i, j: (bi, i, j)),
        compiler_params=_params(("parallel", "parallel", "parallel")),
        name="retention_merge",
    )(y_f, y_b, proj, proj)


def _moe_kernel(be_ref, first_ref, nused_ref, x_ref, w1_ref, perm_ref, b1_ref, w2_ref, b2_ref, g_ref, o_ref,
                w1p_sc, w2p_sc):
    i = pl.program_id(0)
    f = w2_ref.shape[0]
    fp = w2p_sc.shape[0]

    @pl.when(i == 0)
    def _():
        w2p_sc[...] = jnp.zeros(w2p_sc.shape, w2p_sc.dtype)

    @pl.when(first_ref[i] == 1)
    def _():
        w1p_sc[...] = jnp.dot(perm_ref[...], w1_ref[...].astype(BF16), preferred_element_type=F32).astype(BF16)
        w2p_sc[0:f, :] = w2_ref[...].astype(BF16)

    @pl.when(i < nused_ref[0])
    def _():
        u = _dot_nt(x_ref[...], w1p_sc[...]) + b1_ref[...]
        glu = jnp.minimum(u[:, :fp], SWIGLU_LIMIT)
        lin = jnp.clip(u[:, fp:], -SWIGLU_LIMIT, SWIGLU_LIMIT)
        a = glu * jax.nn.sigmoid(SWIGLU_ALPHA * glu) * (lin + 1.0)
        y = jnp.dot(a.astype(BF16), w2p_sc[...], preferred_element_type=F32) + b2_ref[...]
        o_ref[...] = (y * g_ref[...]).astype(o_ref.dtype)

    @pl.when(i >= nused_ref[0])
    def _():
        o_ref[...] = jnp.zeros(o_ref.shape, o_ref.dtype)


def moe_experts(xg, blk_expert, blk_first, n_used, slot_gate, layer, w1, b1, w2, b2):
    n_slots, d = xg.shape
    _, n_e, _, f2 = w1.shape
    f = f2 // 2
    fp = -(-f // LANES) * LANES
    n_blocks = n_slots // MOE_BLOCK
    w1t = jnp.swapaxes(w1, 2, 3)
    perm = np.zeros((2 * fp, f2), np.float32)
    perm[np.arange(f), 2 * np.arange(f)] = 1.0
    perm[fp + np.arange(f), 2 * np.arange(f) + 1] = 1.0
    b1l = b1[layer]
    b1p = jnp.zeros((n_e, 1, 2 * fp), F32).at[:, 0, :f].set(b1l[:, 0::2]).at[:, 0, fp:fp + f].set(b1l[:, 1::2])

    def xrow(i, be, first, nused):
        return (jnp.minimum(i, jnp.maximum(nused[0] - 1, 0)), 0)

    def expert(i, be, first, nused):
        return (be[i], 0, 0)

    def layer_expert(i, be, first, nused):
        return (layer, be[i], 0, 0)

    grid_spec = pltpu.PrefetchScalarGridSpec(
        num_scalar_prefetch=3,
        grid=(n_blocks,),
        in_specs=[pl.BlockSpec((MOE_BLOCK, d), xrow),
                  pl.BlockSpec((None, None, f2, d), layer_expert),
                  pl.BlockSpec((2 * fp, f2), lambda i, be, first, nused: (0, 0)),
                  pl.BlockSpec((None, 1, 2 * fp), expert),
                  pl.BlockSpec((None, None, f, d), layer_expert),
                  pl.BlockSpec((None, 1, d), expert),
                  pl.BlockSpec((MOE_BLOCK, 1), xrow)],
        out_specs=pl.BlockSpec((MOE_BLOCK, d), lambda i, be, first, nused: (i, 0)),
        scratch_shapes=[pltpu.VMEM((2 * fp, d), BF16), pltpu.VMEM((fp, d), BF16)],
    )
    return pl.pallas_call(
        _moe_kernel,
        out_shape=jax.ShapeDtypeStruct((n_slots, d), BF16),
        grid_spec=grid_spec,
        compiler_params=_params(("arbitrary",)),
        name="moe_experts",
    )(blk_expert, blk_first, n_used, xg, w1t, jnp.asarray(perm, BF16), b1p, w2,
      b2[layer].astype(F32).reshape(n_e, 1, d), slot_gate)


def _combined(x_ref, g_ref, y_refs):
    y0, y1, y2, y3 = (r[...].astype(F32) for r in y_refs)
    return x_ref[...] + g_ref[0] * ((y0 + y1) + (y2 + y3))


def _moe_combine_norm_kernel(x_ref, g_ref, y0_ref, y1_ref, y2_ref, y3_ref, ng_ref, m_ref, o_ref, h_ref):
    x = _combined(x_ref, g_ref, (y0_ref, y1_ref, y2_ref, y3_ref))
    o_ref[...] = x
    h_ref[...] = _rms_mod(x, ng_ref[...], m_ref[0:1, :], m_ref[1:2, :]).astype(h_ref.dtype)


def _moe_combine_final_kernel(x_ref, g_ref, y0_ref, y1_ref, y2_ref, y3_ref, ng_ref, o_ref):
    x = _combined(x_ref, g_ref, (y0_ref, y1_ref, y2_ref, y3_ref))
    o_ref[...] = x * lax.rsqrt(jnp.mean(x * x, axis=-1, keepdims=True) + NORM_EPS) * ng_ref[...]


def moe_combine(xs, gate_groups, ysg, next_g, next_modv):
    b, s, d = xs.shape
    assert TOP_K == 4
    tr = CTX_LEN
    nblk = s // tr
    nr = b * nblk
    xs2 = xs.reshape(b * s, d)
    final = next_modv is None
    off = 1 if final else 0

    def row(bi, i):
        return bi * nblk + off + i

    yspec = [pl.BlockSpec((tr, d), functools.partial(lambda bi, i, k: (k * nr + row(bi, i), 0), k=k))
             for k in range(TOP_K)]
    in_specs = [pl.BlockSpec((tr, d), lambda bi, i: (row(bi, i), 0)),
                pl.BlockSpec((1, 1, d), lambda bi, i: (row(bi, i), 0, 0))] + yspec + [
                    pl.BlockSpec((1, d), lambda bi, i: (0, 0))]
    args = [xs2, gate_groups, ysg, ysg, ysg, ysg, next_g.reshape(1, d)]
    if final:
        return pl.pallas_call(
            _moe_combine_final_kernel,
            out_shape=jax.ShapeDtypeStruct((b, s - CTX_LEN, d), F32),
            grid=(b, nblk - 1),
            in_specs=in_specs,
            out_specs=pl.BlockSpec((None, tr, d), lambda bi, i: (bi, i, 0)),
            compiler_params=_params(("parallel", "parallel")),
            name="moe_combine_final_norm",
        )(*args)
    in_specs.append(pl.BlockSpec((None, None, N_MOD, d), lambda bi, i: (bi, jnp.minimum(i, 1), 0, 0)))
    xs2, h = pl.pallas_call(
        _moe_combine_norm_kernel,
        out_shape=(jax.ShapeDtypeStruct((b * s, d), F32), jax.ShapeDtypeStruct((b * s, d), BF16)),
        grid=(b, nblk),
        in_specs=in_specs,
        out_specs=(pl.BlockSpec((tr, d), lambda bi, i: (row(bi, i), 0)),
                   pl.BlockSpec((tr, d), lambda bi, i: (row(bi, i), 0))),
        input_output_aliases={0: 0},
        compiler_params=_params(("parallel", "parallel")),
        name="moe_combine_norm",
    )(*args, next_modv)
    return xs2.reshape(b, s, d), h.reshape(b, s, d)


def moe_layer(xs, gate_groups, h, top_idx, top_gate, layer, w1, b1, w2, b2, next_g, next_modv):
    t_tok = top_idx.shape[1]
    n_e = w1.shape[1]
    n_assign = t_tok * TOP_K
    n_blocks = -(-n_assign // MOE_BLOCK) + n_e
    n_slots = n_blocks * MOE_BLOCK
    n_extra = n_slots - n_assign
    e_flat = top_idx.reshape(n_assign)
    counts = jnp.sum((e_flat[:, None] == jnp.arange(n_e, dtype=jnp.int32)[None, :]).astype(jnp.int32), axis=0)
    padded = (counts + MOE_BLOCK - 1) // MOE_BLOCK * MOE_BLOCK
    n_used = (jnp.sum(padded) // MOE_BLOCK).astype(jnp.int32)
    pad_cum = jnp.cumsum(padded - counts)
    pad_key = jnp.sum((jnp.arange(n_extra, dtype=jnp.int32)[:, None] >= pad_cum[None, :]).astype(jnp.int32), axis=1)
    keys = jnp.concatenate([e_flat, pad_key])
    ids = jnp.arange(n_slots, dtype=jnp.int32)
    tok = jnp.concatenate([jnp.tile(jnp.arange(t_tok, dtype=jnp.int32), TOP_K),
                           jnp.arange(n_extra, dtype=jnp.int32) % t_tok])
    gate = jnp.concatenate([top_gate.reshape(n_assign), jnp.zeros((n_extra,), F32)])
    slot_key, slot_id, slot_tok, slot_gate = lax.sort((keys, ids, tok, gate), dimension=0, is_stable=True, num_keys=1)
    _, slot_of = lax.sort_key_val(slot_id, ids)
    blk_expert = jnp.minimum(slot_key.reshape(n_blocks, MOE_BLOCK)[:, 0], n_e - 1)
    prev = jnp.concatenate([jnp.full((1,), -1, jnp.int32), blk_expert[:-1]])
    blk_first = ((blk_expert != prev) & (jnp.arange(n_blocks) < n_used)).astype(jnp.int32)
    xg = h.at[slot_tok].get(mode="promise_in_bounds")
    ys = moe_experts(xg, blk_expert, blk_first, n_used.reshape(1), slot_gate.reshape(n_slots, 1), layer,
                     w1, b1, w2, b2)
    ysg = ys.at[slot_of[:n_assign]].get(mode="promise_in_bounds")
    return moe_combine(xs, gate_groups, ysg, next_g, next_modv)


def _group_rows(vec_ctx, vec_lat, s):
    b, d = vec_lat.shape
    ng = s // CTX_LEN
    g = jnp.concatenate([jnp.broadcast_to(vec_ctx[None, None, :], (b, 1, d)),
                         jnp.broadcast_to(vec_lat[:, None, :], (b, ng - 1, d))], axis=1)
    return g.reshape(b * ng, 1, d)


def kernel(x, c, ctx, c_ctx, w_mod, b_mod, g_mix, g_ffn, g_final, na_w_qkv, na_w_o, na_rpb, ret_w_in, ret_w_o, ret_decay_exp, gqa_w_qkv, gqa_w_o, gqa_q_gain, gqa_k_gain, swa_w_qkv, swa_w_o, swa_sink, moe_w_router, moe_b_router, moe_w1, moe_b1, moe_w2, moe_b2):
    b, l, d = x.shape
    assert ctx.shape[1] == CTX_LEN and l % CTX_LEN == 0
    s = CTX_LEN + l
    depth = w_mod.shape[0]
    xs = jnp.concatenate([ctx, x], axis=1)

    n_rows = -(-(b + 1) // 16) * 16
    s_rows = jnp.zeros((n_rows, d), F32).at[:b].set(jax.nn.silu(c)).at[b].set(jax.nn.silu(c_ctx)).astype(BF16)

    def layer_mod(li):
        mod = modulation_layer(s_rows, w_mod, b_mod, li).reshape(n_rows, N_MOD, d)
        mod_lat, mod_ctx = mod[:b], mod[b]
        modv = jnp.stack([jnp.broadcast_to(mod_ctx[None], (b, N_MOD, d)), mod_lat], axis=1)
        return mod_lat, mod_ctx, modv

    ones = jnp.ones((1, LANES), F32)
    mods = [layer_mod(li) for li in range(depth)]
    h = norm_modulate(xs, g_mix[0], mods[0][2], 0)
    for li in range(depth):
        mix, j = li % N_MIXERS, li // N_MIXERS
        need_ctx = li < depth - 1
        mod_lat, mod_ctx, modv = mods[li]
        h2 = h.reshape(b * s, d)
        if mix == 0:
            qkv = matmul(h2, na_w_qkv[j]).reshape(b, s, 3 * d)
            o = neighbourhood_attention(qkv, na_rpb[j], NA_HEADS)
            o = ctx_self_attention(o, qkv, 0, qkv, NA_HEADS, qkv, 2 * NA_HEADS, NA_HEADS, 1, LANES ** -0.5 * LOG2E)
            w_o = na_w_o[j]
        elif mix == 1:
            proj = matmul(h2, ret_w_in[j]).reshape(b, s, 5 * d)
            cos, sin = rope_tables(l, d // RET_HEADS)
            y_f, y_b = retention_scan(proj, ret_decay_exp[j], cos, sin, RET_HEADS)
            o = retention_merge(y_f, y_b, proj)
            w_o = ret_w_o[j]
        elif mix == 2:
            dh = d // GQA_HEADS
            assert dh == LANES
            qkv = matmul(h2, gqa_w_qkv[j]).reshape(b, s, -1)
            cos, sin = rope_tables(l, dh)
            qg = (gqa_q_gain[j].astype(F32) * (dh ** -0.5 * LOG2E)).reshape(1, LANES)
            kg = gqa_k_gain[j].astype(F32).reshape(1, LANES)
            qp = qk_prepare(qkv, 0, GQA_HEADS * dh, qg, cos, sin, dh, True)
            kp = qk_prepare(qkv, GQA_HEADS * dh, GQA_KV_HEADS * dh, kg, cos, sin, dh, True)
            o = global_gqa_attention(qp, kp, qkv, GQA_HEADS, GQA_KV_HEADS)
            o = ctx_self_attention(o, qp, 0, kp, 0, qkv, GQA_HEADS + GQA_KV_HEADS, GQA_HEADS,
                                   GQA_HEADS // GQA_KV_HEADS, 1.0)
            w_o = gqa_w_o[j]
        else:
            dh = d // SWA_HEADS
            qkv = matmul(h2, swa_w_qkv[j]).reshape(b, s, -1)
            cos, sin = rope_tables(l, dh)
            reps = LANES // dh
            cos, sin = jnp.tile(cos, (1, reps)), jnp.tile(sin, (1, reps))
            unit = jnp.zeros((s, LANES), F32)
            k2 = qk_prepare(qkv, SWA_HEADS * dh, SWA_KV_HEADS * dh, ones, cos, sin, dh, False, dup=True)
            v2 = qk_prepare(qkv, (SWA_HEADS + SWA_KV_HEADS) * dh, SWA_KV_HEADS * dh, ones,
                            jnp.ones((s, LANES), F32), unit, dh, False, dup=True)
            o = window_gqa_attention(qkv, k2, v2, swa_sink[j].astype(F32) * LOG2E, cos, sin, dh ** -0.5 * LOG2E,
                                     SWA_HEADS, SWA_KV_HEADS, dh)
            w_o = swa_w_o[j]
        if mix in (0, 2) and not need_ctx:
            o = o.at[:, :CTX_LEN].set(0)
        xs = matmul_residual(o.reshape(b * s, d), w_o, xs.reshape(b * s, d),
                             _group_rows(mod_ctx[2], mod_lat[:, 2], s)).reshape(b, s, d)

        hf, top_idx, top_gate = norm_modulate_route(xs, g_ffn[li], modv, 3, moe_w_router[li], moe_b_router[li])
        top_idx = top_idx[:, :TOP_K].transpose(1, 0, 2).reshape(TOP_K, b * s)
        top_gate = top_gate[:, :TOP_K].transpose(1, 0, 2).reshape(TOP_K, b * s)
        last = li == depth - 1
        out = moe_layer(xs, _group_rows(mod_ctx[5], mod_lat[:, 5], s), hf.reshape(b * s, d), top_idx, top_gate,
                        li, moe_w1, moe_b1, moe_w2, moe_b2,
                        g_final if last else g_mix[li + 1], None if last else mods[li + 1][2])
        if last:
            return out
        xs, h = out
```

```python
import functools

import jax
import jax.numpy as jnp
import numpy as np
from jax import lax
from jax.experimental import pallas as pl
from jax.experimental.pallas import tpu as pltpu

DEPTH = 4
GRID_W = 64
CTX_LEN = 256
N_MIXERS = 4
N_MOD = 6
NORM_EPS = 1e-6
NEG_INF = -1e30
ROPE_THETA = 10000.0

NA_HEADS = 32
NA_KH = 8
NA_KW = 16
NA_ROW_BLOCK = 4
NA_KEY_ROWS = 12
NA_HEADS_PER_STEP = 4

RET_HEADS = 16
RET_CHUNK = 128
RET_HEADS_PER_STEP = 8

GQA_HEADS = 32
GQA_KV_HEADS = 8

SWA_HEADS = 64
SWA_KV_HEADS = 8
SWA_WINDOW = 128
SWA_BLOCK = 128
SWA_BLOCKS_PER_STEP = 2
SWA_GROUPS_PER_CHAIN = 2

N_EXPERTS = 32
TOP_K = 4
SWIGLU_ALPHA = 1.702
SWIGLU_LIMIT = 7.0
MOE_BLOCK = 256
OUTPROJ_ROWS = 128

LOG2E = 1.4426950408889634
LANES = 128
SUBLANES = 8
VMEM_LIMIT_BYTES = 56 * 1024 * 1024

F32 = jnp.float32
BF16 = jnp.bfloat16


def _params(semantics, vmem=VMEM_LIMIT_BYTES):
    return pltpu.CompilerParams(dimension_semantics=semantics, vmem_limit_bytes=vmem)


def _pick(n, candidates):
    for c in candidates:
        if n % c == 0:
            return c
    return n


def _mod_kernel(s_ref, w_ref, b_ref, o_ref):
    acc = jnp.dot(s_ref[...], w_ref[...].astype(BF16), preferred_element_type=F32)
    o_ref[...] = acc + b_ref[...]


def modulation_layer(s_rows, w_mod, b_mod, layer):
    depth, d, n = w_mod.shape
    rows = s_rows.shape[0]
    bn = _pick(n, (512, 256, 128))
    return pl.pallas_call(
        _mod_kernel,
        out_shape=jax.ShapeDtypeStruct((rows, n), F32),
        grid=(n // bn,),
        in_specs=[
            pl.BlockSpec((rows, d), lambda j: (0, 0)),
            pl.BlockSpec((None, d, bn), lambda j: (layer, 0, j)),
            pl.BlockSpec((None, 1, bn), lambda j: (layer, 0, j)),
        ],
        out_specs=pl.BlockSpec((rows, bn), lambda j: (0, j)),
        compiler_params=_params(("parallel",)),
        name="modulation",
    )(s_rows, w_mod, b_mod.reshape(depth, 1, n))


def _rms_mod(x, g, shift, scale):
    y = x * lax.rsqrt(jnp.mean(x * x, axis=-1, keepdims=True) + NORM_EPS) * g
    return y * (1.0 + scale) + shift


def _norm_mod_kernel(x_ref, g_ref, m_ref, o_ref, *, shift_idx):
    h = _rms_mod(x_ref[...], g_ref[...], m_ref[shift_idx:shift_idx + 1, :], m_ref[shift_idx + 1:shift_idx + 2, :])
    o_ref[...] = h.astype(o_ref.dtype)


def _norm_route_kernel(x_ref, g_ref, m_ref, wh_ref, wl_ref, br_ref, o_ref, idx_ref, gate_ref, *, shift_idx):
    _norm_route(x_ref[...], g_ref, m_ref, wh_ref, wl_ref, br_ref, o_ref, idx_ref, gate_ref, shift_idx)


def _outproj_norm_route_kernel(a_ref, w_ref, r_ref, gt_ref, g_ref, m_ref, wh_ref, wl_ref, br_ref,
                               xs_ref, o_ref, idx_ref, gate_ref, *, shift_idx):
    x = r_ref[...] + gt_ref[0] * jnp.dot(a_ref[...], w_ref[...], preferred_element_type=F32)
    xs_ref[...] = x
    _norm_route(x, g_ref, m_ref, wh_ref, wl_ref, br_ref, o_ref, idx_ref, gate_ref, shift_idx)


def _norm_route(x, g_ref, m_ref, wh_ref, wl_ref, br_ref, o_ref, idx_ref, gate_ref, shift_idx):
    h = _rms_mod(x, g_ref[...], m_ref[shift_idx:shift_idx + 1, :], m_ref[shift_idx + 1:shift_idx + 2, :])
    h_hi = h.astype(BF16)
    o_ref[...] = h_hi.astype(o_ref.dtype)
    h_lo = (h - h_hi.astype(F32)).astype(BF16)
    logits = (jnp.dot(h_hi, wh_ref[...], preferred_element_type=F32)
              + (jnp.dot(h_lo, wh_ref[...], preferred_element_type=F32)
                 + jnp.dot(h_hi, wl_ref[...], preferred_element_type=F32))) + br_ref[...]
    rows, n_e = logits.shape
    e_iota = lax.broadcasted_iota(jnp.int32, (rows, n_e), 1).astype(F32)
    lane = lax.broadcasted_iota(jnp.int32, (rows, LANES), 1)
    idx_out = jnp.zeros((rows, LANES), F32)
    val_out = jnp.zeros((rows, LANES), F32)
    top0 = None
    denom = jnp.zeros((rows, 1), F32)
    work = logits
    for k in range(TOP_K):
        mx = jnp.max(work, axis=-1, keepdims=True)
        ix = jnp.min(jnp.where(work == mx, e_iota, float(n_e)), axis=-1, keepdims=True)
        if k == 0:
            top0 = mx
        ex = jnp.exp(mx - top0)
        denom = denom + ex
        idx_out = jnp.where(lane == k, ix, idx_out)
        val_out = jnp.where(lane == k, ex, val_out)
        work = jnp.where(e_iota == ix, -jnp.inf, work)
    idx_ref[...] = idx_out.T[0:SUBLANES, :].astype(jnp.int32)
    gate_ref[...] = (val_out / denom).T[0:SUBLANES, :]


def norm_modulate(xs, g, modv, shift_idx):
    b, s, d = xs.shape
    tr = CTX_LEN
    nblk = s // tr
    return pl.pallas_call(
        functools.partial(_norm_mod_kernel, shift_idx=shift_idx),
        out_shape=jax.ShapeDtypeStruct((b, s, d), BF16),
        grid=(b, nblk),
        in_specs=[
            pl.BlockSpec((None, tr, d), lambda bi, i: (bi, i, 0)),
            pl.BlockSpec((1, d), lambda bi, i: (0, 0)),
            pl.BlockSpec((None, None, N_MOD, d), lambda bi, i: (bi, jnp.minimum(i, 1), 0, 0)),
        ],
        out_specs=pl.BlockSpec((None, tr, d), lambda bi, i: (bi, i, 0)),
        compiler_params=_params(("parallel", "parallel")),
        name="norm_modulate",
    )(xs, g.reshape(1, d), modv)


def norm_modulate_route(xs, g, modv, shift_idx, w_router, b_router):
    b, s, d = xs.shape
    tr = CTX_LEN
    nblk = s // tr
    n_e = w_router.shape[1]
    w_hi = w_router.astype(BF16)
    w_lo = (w_router.astype(F32) - w_hi.astype(F32)).astype(BF16)
    return pl.pallas_call(
        functools.partial(_norm_route_kernel, shift_idx=shift_idx),
        out_shape=(jax.ShapeDtypeStruct((b, s, d), BF16),
                   jax.ShapeDtypeStruct((b, SUBLANES, s), jnp.int32),
                   jax.ShapeDtypeStruct((b, SUBLANES, s), F32)),
        grid=(b, nblk),
        in_specs=[
            pl.BlockSpec((None, tr, d), lambda bi, i: (bi, i, 0)),
            pl.BlockSpec((1, d), lambda bi, i: (0, 0)),
            pl.BlockSpec((None, None, N_MOD, d), lambda bi, i: (bi, jnp.minimum(i, 1), 0, 0)),
            pl.BlockSpec((d, n_e), lambda bi, i: (0, 0)),
            pl.BlockSpec((d, n_e), lambda bi, i: (0, 0)),
            pl.BlockSpec((1, n_e), lambda bi, i: (0, 0)),
        ],
        out_specs=(pl.BlockSpec((None, tr, d), lambda bi, i: (bi, i, 0)),
                   pl.BlockSpec((None, SUBLANES, tr), lambda bi, i: (bi, 0, i)),
                   pl.BlockSpec((None, SUBLANES, tr), lambda bi, i: (bi, 0, i))),
        compiler_params=_params(("parallel", "parallel")),
        name="norm_modulate_route",
    )(xs, g.reshape(1, d), modv, w_hi, w_lo, b_router.astype(F32).reshape(1, n_e))


def outproj_norm_route(a, w_o, xs, gate_groups, g, modv, shift_idx, w_router, b_router):
    b, s, d = xs.shape
    tm = OUTPROJ_ROWS
    nb = s // tm
    per_group = CTX_LEN // tm
    n_e = w_router.shape[1]
    w_hi = w_router.astype(BF16)
    w_lo = (w_router.astype(F32) - w_hi.astype(F32)).astype(BF16)
    const = lambda i: (0, 0)
    xs2, h, idx, gate = pl.pallas_call(
        functools.partial(_outproj_norm_route_kernel, shift_idx=shift_idx),
        out_shape=(jax.ShapeDtypeStruct((b * s, d), F32), jax.ShapeDtypeStruct((b * s, d), BF16),
                   jax.ShapeDtypeStruct((b, SUBLANES, s), jnp.int32), jax.ShapeDtypeStruct((b, SUBLANES, s), F32)),
        grid=(b * nb,),
        in_specs=[pl.BlockSpec((tm, d), lambda i: (i, 0)),
                  pl.BlockSpec((d, d), const, pipeline_mode=pl.Buffered(1)),
                  pl.BlockSpec((tm, d), lambda i: (i, 0)),
                  pl.BlockSpec((1, 1, d), lambda i: (i // per_group, 0, 0)),
                  pl.BlockSpec((1, d), const),
                  pl.BlockSpec((None, None, N_MOD, d), lambda i: (i // nb, jnp.minimum((i % nb) // per_group, 1), 0, 0)),
                  pl.BlockSpec((d, n_e), const),
                  pl.BlockSpec((d, n_e), const),
                  pl.BlockSpec((1, n_e), const)],
        out_specs=(pl.BlockSpec((tm, d), lambda i: (i, 0)),
                   pl.BlockSpec((tm, d), lambda i: (i, 0)),
                   pl.BlockSpec((None, SUBLANES, tm), lambda i: (i // nb, 0, i % nb)),
                   pl.BlockSpec((None, SUBLANES, tm), lambda i: (i // nb, 0, i % nb))),
        input_output_aliases={2: 0},
        compiler_params=_params(("parallel",)),
        name="outproj_norm_route",
    )(a.reshape(b * s, d), w_o.astype(BF16), xs.reshape(b * s, d), gate_groups, g.reshape(1, d), modv,
      w_hi, w_lo, b_router.astype(F32).reshape(1, n_e))
    return xs2.reshape(b, s, d), h.reshape(b, s, d), idx, gate


def _mm_kernel(x_ref, w_ref, o_ref, wb_sc):
    @pl.when(pl.program_id(1) == 0)
    def _():
        wb_sc[...] = w_ref[...].astype(BF16)

    o_ref[...] = jnp.dot(x_ref[...], wb_sc[...], preferred_element_type=F32).astype(o_ref.dtype)


def _mm_residual_kernel(x_ref, w_ref, r_ref, g_ref, o_ref, wb_sc, *, groups, group_rows):
    @pl.when(pl.program_id(1) == 0)
    def _():
        wb_sc[...] = w_ref[...].astype(BF16)

    acc = jnp.dot(x_ref[...], wb_sc[...], preferred_element_type=F32)
    for gi in range(groups):
        rows = slice(gi * group_rows, (gi + 1) * group_rows)
        o_ref[rows, :] = r_ref[rows, :] + g_ref[gi] * acc[rows, :]


def matmul(x, w):
    m, k = x.shape
    n = w.shape[1]
    bm = _pick(m, (1536, 768, 512, 256, 128))
    bn = _pick(n, (512, 256, 128))
    return pl.pallas_call(
        _mm_kernel,
        out_shape=jax.ShapeDtypeStruct((m, n), BF16),
        grid=(n // bn, m // bm),
        in_specs=[pl.BlockSpec((bm, k), lambda j, i: (i, 0)),
                  pl.BlockSpec((k, bn), lambda j, i: (0, j))],
        out_specs=pl.BlockSpec((bm, bn), lambda j, i: (i, j)),
        scratch_shapes=[pltpu.VMEM((k, bn), BF16)],
        compiler_params=_params(("parallel", "arbitrary")),
        name="projection",
    )(x, w)


def matmul_residual(x, w, res, gate_groups):
    m, k = x.shape
    n = w.shape[1]
    bm = _pick(m, (768, 512, 256))
    bn = _pick(n, (512, 256, 128))
    groups = bm // CTX_LEN
    return pl.pallas_call(
        functools.partial(_mm_residual_kernel, groups=groups, group_rows=CTX_LEN),
        out_shape=jax.ShapeDtypeStruct((m, n), F32),
        grid=(n // bn, m // bm),
        in_specs=[pl.BlockSpec((bm, k), lambda j, i: (i, 0)),
                  pl.BlockSpec((k, bn), lambda j, i: (0, j)),
                  pl.BlockSpec((bm, bn), lambda j, i: (i, j)),
                  pl.BlockSpec((groups, 1, bn), lambda j, i: (i, 0, j))],
        out_specs=pl.BlockSpec((bm, bn), lambda j, i: (i, j)),
        scratch_shapes=[pltpu.VMEM((k, bn), BF16)],
        input_output_aliases={2: 0},
        compiler_params=_params(("parallel", "arbitrary")),
        name="out_projection_residual",
    )(x, w, res, gate_groups)


def rope_tables(seq, head_dim):
    half = head_dim // 2
    n = half // 2
    t = jnp.arange(seq)
    rows, cols = (t // GRID_W).astype(F32), (t % GRID_W).astype(F32)
    freqs = ROPE_THETA ** (-jnp.arange(n, dtype=F32) / n)
    ang = jnp.concatenate([rows[:, None] * freqs, cols[:, None] * freqs], axis=-1)
    cos, sin = jnp.cos(ang), jnp.sin(ang)
    cos = jnp.concatenate([jnp.ones((CTX_LEN, half), F32), cos], axis=0)
    sin = jnp.concatenate([jnp.zeros((CTX_LEN, half), F32), sin], axis=0)
    return jnp.concatenate([cos, cos], axis=-1), jnp.concatenate([-sin, sin], axis=-1)


def _swap_halves(y, head_dim):
    half = head_dim // 2
    if head_dim == LANES:
        return pltpu.roll(y, half, 1)
    lane = lax.broadcasted_iota(jnp.int32, y.shape, 1)
    first = (lane % head_dim) < half
    return jnp.where(first, pltpu.roll(y, LANES - half, 1), pltpu.roll(y, half, 1))


def _qk_prep_kernel(x_ref, gain_ref, cos_ref, sin_ref, o_ref, *, head_dim, normalise, dup):
    x = x_ref[...].astype(F32)
    cols = x.shape[1]
    cos, sin = cos_ref[...], sin_ref[...]
    gain = gain_ref[...]
    for c in range(cols // LANES):
        y = x[:, c * LANES:(c + 1) * LANES]
        if normalise:
            y = y * lax.rsqrt(jnp.mean(y * y, axis=-1, keepdims=True) + NORM_EPS)
        y = y * gain
        y = (y * cos + _swap_halves(y, head_dim) * sin).astype(o_ref.dtype)
        if dup:
            lane = lax.broadcasted_iota(jnp.int32, y.shape, 1)
            other = pltpu.roll(y.astype(F32), head_dim, 1).astype(o_ref.dtype)
            o_ref[:, (2 * c) * LANES:(2 * c + 1) * LANES] = jnp.where(lane < head_dim, y, other)
            o_ref[:, (2 * c + 1) * LANES:(2 * c + 2) * LANES] = jnp.where(lane < head_dim, other, y)
        else:
            o_ref[:, c * LANES:(c + 1) * LANES] = y


def qk_prepare(qkv, col0, ncols, gain, cos, sin, head_dim, normalise, dup=False):
    b, s, _ = qkv.shape
    tr = _pick(s, (1408, 1024, 768, 512, 256))
    bc = _pick(ncols, (512, 256, 128))
    c0 = col0 // bc
    assert col0 % bc == 0
    mult = 2 if dup else 1
    return pl.pallas_call(
        functools.partial(_qk_prep_kernel, head_dim=head_dim, normalise=normalise, dup=dup),
        out_shape=jax.ShapeDtypeStruct((b, s, ncols * mult), BF16),
        grid=(b, s // tr, ncols // bc),
        in_specs=[pl.BlockSpec((None, tr, bc), lambda bi, i, j: (bi, i, c0 + j)),
                  pl.BlockSpec((1, LANES), lambda bi, i, j: (0, 0)),
                  pl.BlockSpec((tr, LANES), lambda bi, i, j: (i, 0)),
                  pl.BlockSpec((tr, LANES), lambda bi, i, j: (i, 0))],
        out_specs=pl.BlockSpec((None, tr, bc * mult), lambda bi, i, j: (bi, i, j)),
        compiler_params=_params(("parallel", "parallel", "parallel")),
        name="qk_prepare",
    )(qkv, gain, cos, sin)


def _dot_nt(a, b):
    return lax.dot_general(a, b, (((1,), (1,)), ((), ())), preferred_element_type=F32)


def _ctx_attn_kernel(q_ref, k_ref, v_ref, buf_ref, o_ref, *, scale):
    del buf_ref
    s = _dot_nt(q_ref[...], k_ref[...]) * scale
    m = jnp.max(s, axis=-1, keepdims=True)
    p = jnp.exp2(s - m)
    l = jnp.sum(p, axis=-1, keepdims=True)
    o = jnp.dot(p.astype(BF16), v_ref[...], preferred_element_type=F32) / l
    o_ref[...] = o.astype(o_ref.dtype)


def ctx_self_attention(o_buf, q_arr, q_c0, k_arr, k_c0, v_arr, v_c0, n_heads, group, scale):
    b, s, dm = o_buf.shape
    return pl.pallas_call(
        functools.partial(_ctx_attn_kernel, scale=scale),
        out_shape=jax.ShapeDtypeStruct(o_buf.shape, o_buf.dtype),
        grid=(b, n_heads),
        in_specs=[pl.BlockSpec((None, CTX_LEN, LANES), lambda bi, h: (bi, 0, q_c0 + h)),
                  pl.BlockSpec((None, CTX_LEN, LANES), lambda bi, h: (bi, 0, k_c0 + h // group)),
                  pl.BlockSpec((None, CTX_LEN, LANES), lambda bi, h: (bi, 0, v_c0 + h // group)),
                  pl.BlockSpec(memory_space=pl.ANY)],
        out_specs=pl.BlockSpec((None, CTX_LEN, LANES), lambda bi, h: (bi, 0, h)),
        input_output_aliases={3: 0},
        compiler_params=_params(("parallel", "parallel")),
        name="ctx_self_attention",
    )(q_arr, k_arr, v_arr, o_buf)


def _na_kernel(q_ref, k_ref, v_ref, b_ref, o_ref, *, scale, n_rows):
    blk = pl.program_id(2)
    base = jnp.clip(NA_ROW_BLOCK * blk - NA_KH // 2, 0, n_rows - NA_KEY_ROWS)
    start = pl.multiple_of(CTX_LEN + base * GRID_W, GRID_W)
    nwin = NA_KEY_ROWS * GRID_W
    ones_w, ones_c = jnp.ones((nwin, LANES), BF16), jnp.ones((CTX_LEN, LANES), BF16)
    for hh in range(q_ref.shape[1] // LANES):
        cols = slice(hh * LANES, (hh + 1) * LANES)
        q = q_ref[:, cols]
        s_w = _dot_nt(q, k_ref[pl.ds(start, nwin), cols]) * scale + b_ref[hh]
        s_c = _dot_nt(q, k_ref[0:CTX_LEN, cols]) * scale
        m = jnp.maximum(jnp.max(s_w, axis=-1, keepdims=True), jnp.max(s_c, axis=-1, keepdims=True))
        v_w = jnp.concatenate([v_ref[pl.ds(start, nwin), cols], ones_w], axis=1)
        v_c = jnp.concatenate([v_ref[0:CTX_LEN, cols], ones_c], axis=1)
        pv = (jnp.dot(jnp.exp2(s_w - m).astype(BF16), v_w, preferred_element_type=F32)
              + jnp.dot(jnp.exp2(s_c - m).astype(BF16), v_c, preferred_element_type=F32))
        o_ref[:, cols] = (pv[:, :LANES] / pv[:, LANES:]).astype(o_ref.dtype)


def _na_bias_tables(rpb, n_rows):
    rb, kr, w = NA_ROW_BLOCK, NA_KEY_ROWS, GRID_W
    kh = min(NA_KH, n_rows)
    nblk = n_rows // rb
    n_h, n_ri, n_ci = rpb.shape
    qc = np.arange(w)
    c0 = np.clip(qc - NA_KW // 2, 0, w - NA_KW)
    kc = np.arange(w)
    col_ok = (kc[None, :] >= c0[:, None]) & (kc[None, :] < c0[:, None] + NA_KW)
    col_idx = np.clip(kc[None, :] - qc[:, None], 1 - NA_KW, NA_KW - 1) + NA_KW - 1
    onehot = (col_idx.reshape(1, w * w) == np.arange(n_ci)[:, None]).astype(np.float32)
    by_col = jnp.dot(rpb.astype(F32).reshape(n_h * n_ri, n_ci) * LOG2E, jnp.asarray(onehot),
                     precision=lax.Precision.HIGHEST).reshape(n_h, n_ri, w, w)
    cases = []
    for blk in range(nblk):
        base = int(np.clip(rb * blk - NA_KH // 2, 0, n_rows - kr))
        r = rb * blk + np.arange(rb)
        r0 = np.clip(r - kh // 2, 0, n_rows - kh)
        krow = base + np.arange(kr)
        row_ok = (krow[None, :] >= r0[:, None]) & (krow[None, :] < r0[:, None] + kh)
        row_idx = np.clip(krow[None, :] - r[:, None] + NA_KH - 1, 0, 2 * NA_KH - 2)
        cases.append((base - rb * blk, row_ok, row_idx))
    for blk in range(2, nblk - 1):
        assert cases[blk][0] == cases[1][0] and all(np.array_equal(cases[blk][i], cases[1][i]) for i in (1, 2))
    out = []
    for blk in (0, 1, nblk - 1):
        _, row_ok, row_idx = cases[blk]
        tab = jnp.stack([jnp.stack([by_col[:, int(row_idx[a, b2])] for b2 in range(kr)], axis=2)
                         for a in range(rb)], axis=1)
        ok = row_ok[:, None, :, None] & col_ok[None, :, None, :]
        out.append(jnp.where(ok[None], tab, NEG_INF).reshape(n_h, rb * w, kr * w))
    return jnp.stack(out, axis=0)


def neighbourhood_attention(qkv, rpb, n_heads):
    b, s, _ = qkv.shape
    n_rows = (s - CTX_LEN) // GRID_W
    nblk = n_rows // NA_ROW_BLOCK
    tq = NA_ROW_BLOCK * GRID_W
    assert tq == CTX_LEN
    nwin = NA_KEY_ROWS * GRID_W
    bias = _na_bias_tables(rpb, n_rows)
    scale = LANES ** -0.5 * LOG2E

    hps = _pick(n_heads, (NA_HEADS_PER_STEP, 1))
    ng = n_heads // hps
    wide = hps * LANES

    def bias_case(blk):
        return jnp.where(blk == 0, 0, jnp.where(blk == nblk - 1, 2, 1))

    return pl.pallas_call(
        functools.partial(_na_kernel, scale=scale, n_rows=n_rows),
        out_shape=jax.ShapeDtypeStruct((b, s, n_heads * LANES), BF16),
        grid=(b, ng, nblk),
        in_specs=[pl.BlockSpec((None, tq, wide), lambda bi, h, i: (bi, i + 1, h)),
                  pl.BlockSpec((None, s, wide), lambda bi, h, i: (bi, 0, ng + h)),
                  pl.BlockSpec((None, s, wide), lambda bi, h, i: (bi, 0, 2 * ng + h)),
                  pl.BlockSpec((None, hps, tq, nwin), lambda bi, h, i: (bias_case(i), h, 0, 0))],
        out_specs=pl.BlockSpec((None, tq, wide), lambda bi, h, i: (bi, i + 1, h)),
        compiler_params=_params(("parallel", "parallel", "arbitrary")),
        name="neighbourhood_attention",
    )(qkv, qkv, qkv, bias)


def _gqa_kernel(q_ref, k_ref, v_ref, o_ref, m_sc, acc_sc, *, tk, group):
    m_sc[...] = jnp.full(m_sc.shape, -jnp.inf, F32)
    acc_sc[...] = jnp.zeros(acc_sc.shape, F32)
    ones = jnp.ones((tk, LANES), BF16)

    def body(c, carry):
        rows = pl.ds(pl.multiple_of(c * tk, tk), tk)
        kc = k_ref[rows, :]
        vc = jnp.concatenate([v_ref[rows, :], ones], axis=1)
        for g in range(group):
            s = _dot_nt(q_ref[:, g * LANES:(g + 1) * LANES], kc)
            m_prev = m_sc[g]
            m_new = jnp.maximum(m_prev, jnp.max(s, axis=-1, keepdims=True))
            alpha = jnp.exp2(m_prev - m_new)
            p = jnp.exp2(s - m_new[:, :1])
            pv = jnp.dot(p.astype(BF16), vc, preferred_element_type=F32)
            acc_sc[g] = jnp.concatenate([alpha, alpha], axis=1) * acc_sc[g] + pv
            m_sc[g] = m_new
        return carry

    lax.fori_loop(0, k_ref.shape[0] // tk, body, 0, unroll=True)
    for g in range(group):
        a = acc_sc[g]
        o_ref[:, g * LANES:(g + 1) * LANES] = (a[:, :LANES] / a[:, LANES:]).astype(o_ref.dtype)


def global_gqa_attention(qp, kp, qkv, n_heads, n_kv):
    b, s, _ = qp.shape
    group = n_heads // n_kv
    tq = CTX_LEN
    nq = s // tq - 1
    tk = _pick(s, (768, 512, 256))
    v_c0 = n_heads + n_kv
    return pl.pallas_call(
        functools.partial(_gqa_kernel, tk=tk, group=group),
        out_shape=jax.ShapeDtypeStruct((b, s, n_heads * LANES), BF16),
        grid=(b, n_kv, nq),
        in_specs=[pl.BlockSpec((None, tq, group * LANES), lambda bi, h, i: (bi, i + 1, h)),
                  pl.BlockSpec((None, s, LANES), lambda bi, h, i: (bi, 0, h)),
                  pl.BlockSpec((None, s, LANES), lambda bi, h, i: (bi, 0, v_c0 + h))],
        out_specs=pl.BlockSpec((None, tq, group * LANES), lambda bi, h, i: (bi, i + 1, h)),
        scratch_shapes=[pltpu.VMEM((group, tq, LANES), F32), pltpu.VMEM((group, tq, 2 * LANES), F32)],
        compiler_params=_params(("parallel", "parallel", "arbitrary")),
        name="global_gqa_attention",
    )(qp, kp, qkv)


def _swa_kernel(q_ref, k_ref, v_ref, sink_ref, cos_ref, sin_ref, o_ref, *, head_dim, n_lat, q_scale):
    i = pl.program_id(2)
    tq = SWA_BLOCK
    sub_blocks = q_ref.shape[0] // tq
    n_ctx_steps = CTX_LEN // q_ref.shape[0]
    band = SWA_BLOCK + 2 * SWA_WINDOW

    @pl.when(i < n_ctx_steps)
    def _():
        o_ref[...] = jnp.zeros(o_ref.shape, o_ref.dtype)

    def query_block(sb):
        rows = slice(sb * tq, (sb + 1) * tq)
        qb = (i - n_ctx_steps) * sub_blocks + sb
        row0 = jnp.clip(CTX_LEN + SWA_BLOCK * qb - SWA_WINDOW, 0, k_ref.shape[0] - band)
        row0 = pl.multiple_of(row0, SWA_BLOCK)
        nk = band + CTX_LEN
        k_all = jnp.concatenate([k_ref[pl.ds(row0, band), :], k_ref[0:CTX_LEN, :]], axis=0)
        v_all = jnp.concatenate([jnp.concatenate([v_ref[pl.ds(row0, band), :], v_ref[0:CTX_LEN, :]], axis=0),
                                 jnp.ones((nk, LANES), BF16)], axis=1)
        qpos = SWA_BLOCK * qb + lax.broadcasted_iota(jnp.int32, (tq, nk), 0)
        col = lax.broadcasted_iota(jnp.int32, (tq, nk), 1)
        kpos = row0 - CTX_LEN + col
        ok = (col >= band) | ((kpos >= 0) & (kpos < n_lat) & (jnp.abs(kpos - qpos) <= SWA_WINDOW))
        hpg = LANES // head_dim
        lane = lax.broadcasted_iota(jnp.int32, (tq, LANES), 1)
        sels = [(lane >= j * head_dim) & (lane < (j + 1) * head_dim) for j in range(hpg)]
        cos, sin = cos_ref[rows, :], sin_ref[rows, :]
        n_groups = q_ref.shape[1] // LANES
        q_groups = []
        for c in range(n_groups):
            y = q_ref[rows, c * LANES:(c + 1) * LANES].astype(F32) * q_scale
            q_groups.append((y * cos + _swap_halves(y, head_dim) * sin).astype(BF16))
        gpc = min(SWA_GROUPS_PER_CHAIN, n_groups)
        ok = jnp.concatenate([ok] * (hpg * gpc), axis=0)
        for c0 in range(0, n_groups, gpc):
            heads = [(c, j) for c in range(c0, c0 + gpc) for j in range(hpg)]
            qs = jnp.concatenate([jnp.where(sels[j], q_groups[c], jnp.zeros((tq, LANES), BF16))
                                  for c, j in heads], axis=0)
            sink = jnp.concatenate([jnp.broadcast_to(sink_ref[:, c * hpg + j:c * hpg + j + 1], (tq, 1))
                                    for c, j in heads], axis=0)
            s = jnp.where(ok, _dot_nt(qs, k_all), NEG_INF)
            m = jnp.maximum(jnp.max(s, axis=-1, keepdims=True), sink)
            pv = jnp.dot(jnp.exp2(s - m).astype(BF16), v_all, preferred_element_type=F32)
            o = pv[:, :LANES] / (pv[:, LANES:] + jnp.exp2(sink - m))
            for ci, c in enumerate(range(c0, c0 + gpc)):
                res = o[ci * hpg * tq:(ci * hpg + 1) * tq]
                for j in range(1, hpg):
                    res = jnp.where(sels[j], o[(ci * hpg + j) * tq:(ci * hpg + j + 1) * tq], res)
                o_ref[rows, c * LANES:(c + 1) * LANES] = res.astype(o_ref.dtype)

    @pl.when(i >= n_ctx_steps)
    def _():
        for sb in range(sub_blocks):
            query_block(sb)


def window_gqa_attention(q, k2, v2, sink, cos, sin, q_scale, n_heads, n_kv, head_dim):
    b, s, _ = q.shape
    group = n_heads // n_kv
    gcols = group * head_dim
    assert gcols % LANES == 0 and LANES % head_dim == 0
    tq = SWA_BLOCK * SWA_BLOCKS_PER_STEP
    assert CTX_LEN % tq == 0 and s % tq == 0
    return pl.pallas_call(
        functools.partial(_swa_kernel, head_dim=head_dim, n_lat=s - CTX_LEN, q_scale=q_scale),
        out_shape=jax.ShapeDtypeStruct((b, s, n_heads * head_dim), BF16),
        grid=(b, n_kv, s // tq),
        in_specs=[pl.BlockSpec((None, tq, gcols), lambda bi, h, i: (bi, i, h)),
                  pl.BlockSpec((None, s, LANES), lambda bi, h, i: (bi, 0, h)),
                  pl.BlockSpec((None, s, LANES), lambda bi, h, i: (bi, 0, h)),
                  pl.BlockSpec((None, 1, group), lambda bi, h, i: (h, 0, 0)),
                  pl.BlockSpec((tq, LANES), lambda bi, h, i: (i, 0)),
                  pl.BlockSpec((tq, LANES), lambda bi, h, i: (i, 0))],
        out_specs=pl.BlockSpec((None, tq, gcols), lambda bi, h, i: (bi, i, h)),
        compiler_params=_params(("parallel", "parallel", "arbitrary")),
        name="window_gqa_attention",
    )(q, k2, v2, sink.astype(F32).reshape(n_kv, 1, group), cos, sin)


def _retention_kernel(qf_ref, kf_ref, vf_ref, cosf_ref, sinf_ref, qb_ref, kb_ref, vb_ref, cosb_ref, sinb_ref,
                      intra_ref, qdec_ref, kdec_ref, cdec_ref, of_ref, ob_ref, state_sc, *, k_scale, heads, dh):
    @pl.when(pl.program_id(2) == 0)
    def _():
        state_sc[...] = jnp.zeros(state_sc.shape, F32)

    half = dh // 2

    def rope(x, cos, sin):
        swapped = jnp.concatenate([x[:, half:], x[:, :half]], axis=1)
        return x * cos + swapped * sin

    dirs = ((qf_ref, kf_ref, vf_ref, cosf_ref, sinf_ref, of_ref), (qb_ref, kb_ref, vb_ref, cosb_ref, sinb_ref, ob_ref))
    for dr, (q_ref, k_ref, v_ref, cos_ref, sin_ref, o_ref) in enumerate(dirs):
        cos, sin = cos_ref[...], sin_ref[...]
        for hh in range(heads):
            cols = slice(hh * dh, (hh + 1) * dh)
            q = rope(q_ref[:, cols].astype(F32), cos, sin)
            k = rope(k_ref[:, cols].astype(F32), cos, sin) * k_scale
            v = v_ref[:, cols]
            state = state_sc[dr, hh]
            att = _dot_nt(q.astype(BF16), k.astype(BF16)) * intra_ref[dr, hh]
            y = (jnp.dot(att.astype(BF16), v, preferred_element_type=F32)
                 + jnp.dot((q * qdec_ref[dr, hh]).astype(BF16), state.astype(BF16), preferred_element_type=F32))
            kd = (k * kdec_ref[dr, hh]).astype(BF16)
            state_sc[dr, hh] = state * cdec_ref[dr, hh] + lax.dot_general(kd, v, (((0,), (0,)), ((), ())),
                                                                         preferred_element_type=F32)
            y = y * lax.rsqrt(jnp.mean(y * y, axis=-1, keepdims=True) + NORM_EPS)
            o_ref[:, cols] = y.astype(o_ref.dtype)


def retention_scan(proj, decay_exp, cos, sin, n_heads):
    b, s, five_d = proj.shape
    d = five_d // 5
    dh = d // n_heads
    c = RET_CHUNK
    n_chunks = s // c
    n_ctx_chunks = CTX_LEN // c
    hps = min(RET_HEADS_PER_STEP, n_heads)
    lg = jnp.log1p(-jnp.exp2(-decay_exp.astype(F32)))
    pos = jnp.arange(c, dtype=F32)
    diff = pos[:, None] - pos[None, :]
    lgf, lgb = lg[0][:, None, None], lg[1][:, None, None]
    intra_f = jnp.where(diff >= 0, jnp.exp(lgf * jnp.maximum(diff, 0.0)), 0.0)
    intra_b = jnp.where(diff <= 0, jnp.exp(lgb * jnp.maximum(-diff, 0.0)), 0.0)
    intra = jnp.stack([intra_f, intra_b])
    qdec = jnp.stack([jnp.exp(lg[0][:, None] * (pos + 1.0)), jnp.exp(lg[1][:, None] * (c - pos))])[..., None]
    kdec = jnp.stack([jnp.exp(lg[0][:, None] * (c - 1.0 - pos)), jnp.exp(lg[1][:, None] * pos)])[..., None]
    cdec = jnp.exp(lg * c)[..., None, None]
    ncb = n_heads // hps

    def bwd_chunk(t):
        return jnp.where(t < n_ctx_chunks, n_ctx_chunks - 1 - t, n_chunks - 1 - (t - n_ctx_chunks))

    fwd = lambda off: (lambda bi, h, t: (bi, t, off * ncb + h))
    bwd = lambda off: (lambda bi, h, t: (bi, bwd_chunk(t), off * ncb + h))
    tab = lambda bi, h, t: (0, h, 0, 0)
    blk = (None, c, hps * dh)
    return pl.pallas_call(
        functools.partial(_retention_kernel, k_scale=dh ** -0.5, heads=hps, dh=dh),
        out_shape=(jax.ShapeDtypeStruct((b, s, d), BF16), jax.ShapeDtypeStruct((b, s, d), BF16)),
        grid=(b, ncb, n_chunks),
        in_specs=[pl.BlockSpec(blk, fwd(0)), pl.BlockSpec(blk, fwd(1)), pl.BlockSpec(blk, fwd(2)),
                  pl.BlockSpec((c, dh), lambda bi, h, t: (t, 0)), pl.BlockSpec((c, dh), lambda bi, h, t: (t, 0)),
                  pl.BlockSpec(blk, bwd(0)), pl.BlockSpec(blk, bwd(1)), pl.BlockSpec(blk, bwd(2)),
                  pl.BlockSpec((c, dh), lambda bi, h, t: (bwd_chunk(t), 0)),
                  pl.BlockSpec((c, dh), lambda bi, h, t: (bwd_chunk(t), 0)),
                  pl.BlockSpec((2, hps, c, c), tab),
                  pl.BlockSpec((2, hps, c, 1), tab),
                  pl.BlockSpec((2, hps, c, 1), tab),
                  pl.BlockSpec((2, hps, 1, 1), tab)],
        out_specs=(pl.BlockSpec(blk, lambda bi, h, t: (bi, t, h)),
                   pl.BlockSpec(blk, lambda bi, h, t: (bi, bwd_chunk(t), h))),
        scratch_shapes=[pltpu.VMEM((2, hps, dh, dh), F32)],
        compiler_params=_params(("parallel", "parallel", "arbitrary")),
        name="retention_scan",
    )(proj, proj, proj, cos, sin, proj, proj, proj, cos, sin, intra, qdec, kdec, cdec)


def _ret_merge_kernel(yf_ref, yb_ref, gf_ref, gb_ref, o_ref):
    gf = gf_ref[...].astype(F32)
    gb = gb_ref[...].astype(F32)
    o = gf * jax.nn.sigmoid(gf) * yf_ref[...].astype(F32) + gb * jax.nn.sigmoid(gb) * yb_ref[...].astype(F32)
    o_ref[...] = o.astype(o_ref.dtype)


def retention_merge(y_f, y_b, proj):
    b, s, d = y_f.shape
    tr = _pick(s, (1408, 1024, 768, 512, 256))
    bc = _pick(d, (1024, 512, 256, 128))
    nc = d // bc
    return pl.pallas_call(
        _ret_merge_kernel,
        out_shape=jax.ShapeDtypeStruct((b, s, d), BF16),
        grid=(b, s // tr, nc),
        in_specs=[pl.BlockSpec((None, tr, bc), lambda bi, i, j: (bi, i, j)),
                  pl.BlockSpec((None, tr, bc), lambda bi, i, j: (bi, i, j)),
                  pl.BlockSpec((None, tr, bc), lambda bi, i, j: (bi, i, 3 * nc + j)),
                  pl.BlockSpec((None, tr, bc), lambda bi, i, j: (bi, i, 4 * nc + j))],
        out_specs=pl.BlockSpec((None, tr, bc), lambda bi, i, j: (bi, i, j)),
        compiler_params=_params(("parallel", "parallel", "parallel")),
        name="retention_merge",
    )(y_f, y_b, proj, proj)


def _moe_kernel(be_ref, first_ref, nused_ref, x_ref, w1_ref, perm_ref, b1_ref, w2_ref, b2_ref, g_ref, o_ref,
                w1p_sc, w2p_sc):
    i = pl.program_id(0)
    f = w2_ref.shape[0]
    fp = w2p_sc.shape[0]

    @pl.when(i == 0)
    def _():
        w2p_sc[...] = jnp.zeros(w2p_sc.shape, w2p_sc.dtype)

    @pl.when(first_ref[i] == 1)
    def _():
        w1p_sc[...] = jnp.dot(perm_ref[...], w1_ref[...].astype(BF16), preferred_element_type=F32).astype(BF16)
        w2p_sc[0:f, :] = w2_ref[...].astype(BF16)

    @pl.when(i < nused_ref[0])
    def _():
        u = _dot_nt(x_ref[...], w1p_sc[...]) + b1_ref[...]
        glu = jnp.minimum(u[:, :fp], SWIGLU_LIMIT)
        lin = jnp.clip(u[:, fp:], -SWIGLU_LIMIT, SWIGLU_LIMIT)
        a = glu * jax.nn.sigmoid(SWIGLU_ALPHA * glu) * (lin + 1.0)
        y = jnp.dot(a.astype(BF16), w2p_sc[...], preferred_element_type=F32) + b2_ref[...]
        o_ref[...] = (y * g_ref[...]).astype(o_ref.dtype)

    @pl.when(i >= nused_ref[0])
    def _():
        o_ref[...] = jnp.zeros(o_ref.shape, o_ref.dtype)


def moe_experts(xg, blk_expert, blk_first, n_used, slot_gate, layer, w1, b1, w2, b2):
    n_slots, d = xg.shape
    _, n_e, _, f2 = w1.shape
    f = f2 // 2
    fp = -(-f // LANES) * LANES
    n_blocks = n_slots // MOE_BLOCK
    w1t = jnp.swapaxes(w1, 2, 3)
    perm = np.zeros((2 * fp, f2), np.float32)
    perm[np.arange(f), 2 * np.arange(f)] = 1.0
    perm[fp + np.arange(f), 2 * np.arange(f) + 1] = 1.0
    b1l = b1[layer]
    b1p = jnp.zeros((n_e, 1, 2 * fp), F32).at[:, 0, :f].set(b1l[:, 0::2]).at[:, 0, fp:fp + f].set(b1l[:, 1::2])

    def xrow(i, be, first, nused):
        return (jnp.minimum(i, jnp.maximum(nused[0] - 1, 0)), 0)

    def expert(i, be, first, nused):
        return (be[i], 0, 0)

    def layer_expert(i, be, first, nused):
        return (layer, be[i], 0, 0)

    grid_spec = pltpu.PrefetchScalarGridSpec(
        num_scalar_prefetch=3,
        grid=(n_blocks,),
        in_specs=[pl.BlockSpec((MOE_BLOCK, d), xrow),
                  pl.BlockSpec((None, None, f2, d), layer_expert),
                  pl.BlockSpec((2 * fp, f2), lambda i, be, first, nused: (0, 0)),
                  pl.BlockSpec((None, 1, 2 * fp), expert),
                  pl.BlockSpec((None, None, f, d), layer_expert),
                  pl.BlockSpec((None, 1, d), expert),
                  pl.BlockSpec((MOE_BLOCK, 1), xrow)],
        out_specs=pl.BlockSpec((MOE_BLOCK, d), lambda i, be, first, nused: (i, 0)),
        scratch_shapes=[pltpu.VMEM((2 * fp, d), BF16), pltpu.VMEM((fp, d), BF16)],
    )
    return pl.pallas_call(
        _moe_kernel,
        out_shape=jax.ShapeDtypeStruct((n_slots, d), BF16),
        grid_spec=grid_spec,
        compiler_params=_params(("arbitrary",)),
        name="moe_experts",
    )(blk_expert, blk_first, n_used, xg, w1t, jnp.asarray(perm, BF16), b1p, w2,
      b2[layer].astype(F32).reshape(n_e, 1, d), slot_gate)


def _combined(x_ref, g_ref, y_refs):
    y0, y1, y2, y3 = (r[...].astype(F32) for r in y_refs)
    return x_ref[...] + g_ref[0] * ((y0 + y1) + (y2 + y3))


def _moe_combine_norm_kernel(x_ref, g_ref, y0_ref, y1_ref, y2_ref, y3_ref, ng_ref, m_ref, o_ref, h_ref):
    x = _combined(x_ref, g_ref, (y0_ref, y1_ref, y2_ref, y3_ref))
    o_ref[...] = x
    h_ref[...] = _rms_mod(x, ng_ref[...], m_ref[0:1, :], m_ref[1:2, :]).astype(h_ref.dtype)


def _moe_combine_final_kernel(x_ref, g_ref, y0_ref, y1_ref, y2_ref, y3_ref, ng_ref, o_ref):
    x = _combined(x_ref, g_ref, (y0_ref, y1_ref, y2_ref, y3_ref))
    o_ref[...] = x * lax.rsqrt(jnp.mean(x * x, axis=-1, keepdims=True) + NORM_EPS) * ng_ref[...]


def moe_combine(xs, gate_groups, ysg, next_g, next_modv):
    b, s, d = xs.shape
    assert TOP_K == 4
    tr = CTX_LEN
    nblk = s // tr
    nr = b * nblk
    xs2 = xs.reshape(b * s, d)
    final = next_modv is None
    off = 1 if final else 0

    def row(bi, i):
        return bi * nblk + off + i

    yspec = [pl.BlockSpec((tr, d), functools.partial(lambda bi, i, k: (k * nr + row(bi, i), 0), k=k))
             for k in range(TOP_K)]
    in_specs = [pl.BlockSpec((tr, d), lambda bi, i: (row(bi, i), 0)),
                pl.BlockSpec((1, 1, d), lambda bi, i: (row(bi, i), 0, 0))] + yspec + [
                    pl.BlockSpec((1, d), lambda bi, i: (0, 0))]
    args = [xs2, gate_groups, ysg, ysg, ysg, ysg, next_g.reshape(1, d)]
    if final:
        return pl.pallas_call(
            _moe_combine_final_kernel,
            out_shape=jax.ShapeDtypeStruct((b, s - CTX_LEN, d), F32),
            grid=(b, nblk - 1),
            in_specs=in_specs,
            out_specs=pl.BlockSpec((None, tr, d), lambda bi, i: (bi, i, 0)),
            compiler_params=_params(("parallel", "parallel")),
            name="moe_combine_final_norm",
        )(*args)
    in_specs.append(pl.BlockSpec((None, None, N_MOD, d), lambda bi, i: (bi, jnp.minimum(i, 1), 0, 0)))
    xs2, h = pl.pallas_call(
        _moe_combine_norm_kernel,
        out_shape=(jax.ShapeDtypeStruct((b * s, d), F32), jax.ShapeDtypeStruct((b * s, d), BF16)),
        grid=(b, nblk),
        in_specs=in_specs,
        out_specs=(pl.BlockSpec((tr, d), lambda bi, i: (row(bi, i), 0)),
                   pl.BlockSpec((tr, d), lambda bi, i: (row(bi, i), 0))),
        input_output_aliases={0: 0},
        compiler_params=_params(("parallel", "parallel")),
        name="moe_combine_norm",
    )(*args, next_modv)
    return xs2.reshape(b, s, d), h.reshape(b, s, d)


def moe_layer(xs, gate_groups, h, top_idx, top_gate, layer, w1, b1, w2, b2, next_g, next_modv):
    t_tok = top_idx.shape[1]
    n_e = w1.shape[1]
    n_assign = t_tok * TOP_K
    n_blocks = -(-n_assign // MOE_BLOCK) + n_e
    n_slots = n_blocks * MOE_BLOCK
    n_extra = n_slots - n_assign
    e_flat = top_idx.reshape(n_assign)
    counts = jnp.sum((e_flat[:, None] == jnp.arange(n_e, dtype=jnp.int32)[None, :]).astype(jnp.int32), axis=0)
    padded = (counts + MOE_BLOCK - 1) // MOE_BLOCK * MOE_BLOCK
    n_used = (jnp.sum(padded) // MOE_BLOCK).astype(jnp.int32)
    pad_cum = jnp.cumsum(padded - counts)
    pad_key = jnp.sum((jnp.arange(n_extra, dtype=jnp.int32)[:, None] >= pad_cum[None, :]).astype(jnp.int32), axis=1)
    keys = jnp.concatenate([e_flat, pad_key])
    ids = jnp.arange(n_slots, dtype=jnp.int32)
    tok = jnp.concatenate([jnp.tile(jnp.arange(t_tok, dtype=jnp.int32), TOP_K),
                           jnp.arange(n_extra, dtype=jnp.int32) % t_tok])
    gate = jnp.concatenate([top_gate.reshape(n_assign), jnp.zeros((n_extra,), F32)])
    slot_key, slot_id, slot_tok, slot_gate = lax.sort((keys, ids, tok, gate), dimension=0, is_stable=True, num_keys=1)
    _, slot_of = lax.sort_key_val(slot_id, ids)
    blk_expert = jnp.minimum(slot_key.reshape(n_blocks, MOE_BLOCK)[:, 0], n_e - 1)
    prev = jnp.concatenate([jnp.full((1,), -1, jnp.int32), blk_expert[:-1]])
    blk_first = ((blk_expert != prev) & (jnp.arange(n_blocks) < n_used)).astype(jnp.int32)
    xg = h.at[slot_tok].get(mode="promise_in_bounds")
    ys = moe_experts(xg, blk_expert, blk_first, n_used.reshape(1), slot_gate.reshape(n_slots, 1), layer,
                     w1, b1, w2, b2)
    ysg = ys.at[slot_of[:n_assign]].get(mode="promise_in_bounds")
    return moe_combine(xs, gate_groups, ysg, next_g, next_modv)


def _group_rows(vec_ctx, vec_lat, s):
    b, d = vec_lat.shape
    ng = s // CTX_LEN
    g = jnp.concatenate([jnp.broadcast_to(vec_ctx[None, None, :], (b, 1, d)),
                         jnp.broadcast_to(vec_lat[:, None, :], (b, ng - 1, d))], axis=1)
    return g.reshape(b * ng, 1, d)


def kernel(x, c, ctx, c_ctx, w_mod, b_mod, g_mix, g_ffn, g_final, na_w_qkv, na_w_o, na_rpb, ret_w_in, ret_w_o, ret_decay_exp, gqa_w_qkv, gqa_w_o, gqa_q_gain, gqa_k_gain, swa_w_qkv, swa_w_o, swa_sink, moe_w_router, moe_b_router, moe_w1, moe_b1, moe_w2, moe_b2):
    b, l, d = x.shape
    assert ctx.shape[1] == CTX_LEN and l % CTX_LEN == 0
    s = CTX_LEN + l
    depth = w_mod.shape[0]
    xs = jnp.concatenate([ctx, x], axis=1)

    n_rows = -(-(b + 1) // 16) * 16
    s_rows = jnp.zeros((n_rows, d), F32).at[:b].set(jax.nn.silu(c)).at[b].set(jax.nn.silu(c_ctx)).astype(BF16)

    def layer_mod(li):
        mod = modulation_layer(s_rows, w_mod, b_mod, li).reshape(n_rows, N_MOD, d)
        mod_lat, mod_ctx = mod[:b], mod[b]
        modv = jnp.stack([jnp.broadcast_to(mod_ctx[None], (b, N_MOD, d)), mod_lat], axis=1)
        return mod_lat, mod_ctx, modv

    ones = jnp.ones((1, LANES), F32)
    mods = [layer_mod(li) for li in range(depth)]
    h = norm_modulate(xs, g_mix[0], mods[0][2], 0)
    for li in range(depth):
        mix, j = li % N_MIXERS, li // N_MIXERS
        need_ctx = li < depth - 1
        mod_lat, mod_ctx, modv = mods[li]
        h2 = h.reshape(b * s, d)
        if mix == 0:
            qkv = matmul(h2, na_w_qkv[j]).reshape(b, s, 3 * d)
            o = neighbourhood_attention(qkv, na_rpb[j], NA_HEADS)
            o = ctx_self_attention(o, qkv, 0, qkv, NA_HEADS, qkv, 2 * NA_HEADS, NA_HEADS, 1, LANES ** -0.5 * LOG2E)
            w_o = na_w_o[j]
        elif mix == 1:
            proj = matmul(h2, ret_w_in[j]).reshape(b, s, 5 * d)
            cos, sin = rope_tables(l, d // RET_HEADS)
            y_f, y_b = retention_scan(proj, ret_decay_exp[j], cos, sin, RET_HEADS)
            o = retention_merge(y_f, y_b, proj)
            w_o = ret_w_o[j]
        elif mix == 2:
            dh = d // GQA_HEADS
            assert dh == LANES
            qkv = matmul(h2, gqa_w_qkv[j]).reshape(b, s, -1)
            cos, sin = rope_tables(l, dh)
            qg = (gqa_q_gain[j].astype(F32) * (dh ** -0.5 * LOG2E)).reshape(1, LANES)
            kg = gqa_k_gain[j].astype(F32).reshape(1, LANES)
            qp = qk_prepare(qkv, 0, GQA_HEADS * dh, qg, cos, sin, dh, True)
            kp = qk_prepare(qkv, GQA_HEADS * dh, GQA_KV_HEADS * dh, kg, cos, sin, dh, True)
            o = global_gqa_attention(qp, kp, qkv, GQA_HEADS, GQA_KV_HEADS)
            o = ctx_self_attention(o, qp, 0, kp, 0, qkv, GQA_HEADS + GQA_KV_HEADS, GQA_HEADS,
                                   GQA_HEADS // GQA_KV_HEADS, 1.0)
            w_o = gqa_w_o[j]
        else:
            dh = d // SWA_HEADS
            qkv = matmul(h2, swa_w_qkv[j]).reshape(b, s, -1)
            cos, sin = rope_tables(l, dh)
            reps = LANES // dh
            cos, sin = jnp.tile(cos, (1, reps)), jnp.tile(sin, (1, reps))
            unit = jnp.zeros((s, LANES), F32)
            k2 = qk_prepare(qkv, SWA_HEADS * dh, SWA_KV_HEADS * dh, ones, cos, sin, dh, False, dup=True)
            v2 = qk_prepare(qkv, (SWA_HEADS + SWA_KV_HEADS) * dh, SWA_KV_HEADS * dh, ones,
                            jnp.ones((s, LANES), F32), unit, dh, False, dup=True)
            o = window_gqa_attention(qkv, k2, v2, swa_sink[j].astype(F32) * LOG2E, cos, sin, dh ** -0.5 * LOG2E,
                                     SWA_HEADS, SWA_KV_HEADS, dh)
            w_o = swa_w_o[j]
        if mix in (0, 2) and not need_ctx:
            o = o.at[:, :CTX_LEN].set(0)
        xs, hf, top_idx, top_gate = outproj_norm_route(o, w_o, xs, _group_rows(mod_ctx[2], mod_lat[:, 2], s),
                                                       g_ffn[li], modv, 3, moe_w_router[li], moe_b_router[li])
        top_idx = top_idx[:, :TOP_K].transpose(1, 0, 2).reshape(TOP_K, b * s)
        top_gate = top_gate[:, :TOP_K].transpose(1, 0, 2).reshape(TOP_K, b * s)
        last = li == depth - 1
        out = moe_layer(xs, _group_rows(mod_ctx[5], mod_lat[:, 5], s), hf.reshape(b * s, d), top_idx, top_gate,
                        li, moe_w1, moe_b1, moe_w2, moe_b2,
                        g_final if last else g_mix[li + 1], None if last else mods[li + 1][2])
        if last:
            return out
        xs, h = out
```
